```python
import jax, jax.numpy as jnp
from jax import lax
import numpy as np

D_MODEL = 1024
BATCH = 32
SEQ = 256
DEPTH = 4
DEC_BATCH = 8
DEC_SEQ = 1024
PAST_LEN = 256

GRID_W = 64
N_EVEN = (DEPTH + 1) // 2
N_ODD = DEPTH // 2
EPS = 1e-6
D_A = D_MODEL // 2
A_BLOCKS = 8
A_BS = D_A // A_BLOCKS
A_CONV = 4
A_C = 8.0
D_B = D_MODEL // 2
POOL_WINDOWS = (2, 4, 8, 16)
B_GROUPS = len(POOL_WINDOWS)
B_GS = D_B // B_GROUPS
D_C = D_MODEL // 2
C_CONV = 31
HEAD_DIM = 64
N_Q_HEADS = (D_MODEL // 2) // HEAD_DIM
N_KV_HEADS = 2
D_ATT = N_Q_HEADS * HEAD_DIM
D_KV = N_KV_HEADS * HEAD_DIM
Q_BLOCK = 128
ROPE_THETA = 10000.0
D_FF = 2816
N_EXPERTS = 8
TOP_K = 2
D_FF_E = 1408
D_IN_EVEN = 2 * D_A + D_B
D_IN_ODD = 2 * D_C + D_ATT + 2 * D_KV

kernel_name = 'hybrid_diffusion_prefix_trunk_step'


def rms_norm(x, g):
    xf = x.astype(jnp.float32)
    y = xf * lax.rsqrt(jnp.mean(xf * xf, axis=-1, keepdims=True) + EPS)
    return (y * g.astype(jnp.float32)).astype(x.dtype)


def layer_norm(x, g, b):
    xf = x.astype(jnp.float32)
    mu = jnp.mean(xf, axis=-1, keepdims=True)
    xc = xf - mu
    var = jnp.mean(xc * xc, axis=-1, keepdims=True)
    return (xc * lax.rsqrt(var + EPS) * g.astype(jnp.float32) + b.astype(jnp.float32)).astype(x.dtype)


def ada_params(cond, w_mod, b_mod):
    m = jax.nn.silu(cond) @ w_mod + b_mod
    return jnp.split(m, 6, axis=-1)


def modulate(h, shift, scale):
    return h * (1 + scale) + shift


def depthwise_conv(x, w, b, pad_left, pad_right):
    C = x.shape[-1]
    y = lax.conv_general_dilated(x, w[:, None, :].astype(x.dtype), window_strides=(1,),
                                 padding=[(pad_left, pad_right)],
                                 dimension_numbers=('NWC', 'WIO', 'NWC'),
                                 feature_group_count=C)
    return y + b


def rglru_mixer(u, h0, conv_w, conv_b, w_r, b_r, w_i, b_i, lam):
    B, T, _ = u.shape
    xc = depthwise_conv(u, conv_w, conv_b, A_CONV // 2, A_CONV - 1 - A_CONV // 2)
    xd = jnp.stack([xc, xc[:, ::-1]], axis=2)
    xb = xd.reshape(B, T, 2, A_BLOCKS, A_BS)
    r = jax.nn.sigmoid(jnp.einsum('btnki,nkij->btnkj', xb, w_r).reshape(B, T, 2, D_A) + b_r)
    i = jax.nn.sigmoid(jnp.einsum('btnki,nkij->btnkj', xb, w_i).reshape(B, T, 2, D_A) + b_i)
    log_a = -A_C * r * jax.nn.softplus(-lam)
    a = jnp.exp(log_a)
    bx = jnp.sqrt(-jnp.expm1(2 * log_a)) * (i * xd)

    def step(h, ab):
        a_t, b_t = ab
        h = a_t * h + b_t
        return h, h

    h_last, hs = lax.scan(step, h0.astype(a.dtype),
                          (jnp.moveaxis(a, 1, 0), jnp.moveaxis(bx, 1, 0)))
    hs = jnp.moveaxis(hs, 0, 1)
    y = hs[:, :, 0] + hs[:, ::-1, 1]
    return y, h_last


def pool_mixer(u, w_pool, s_pool):
    B, T, _ = u.shape
    uf = u.astype(jnp.float32)
    cs = jnp.concatenate([jnp.zeros((B, 1, D_B), jnp.float32), jnp.cumsum(uf, axis=1)], axis=1)
    t = np.arange(T)
    outs = []
    for g, w in enumerate(POOL_WINDOWS):
        lo = np.maximum(t - w // 2, 0)
        hi = np.minimum(t + w // 2, T)
        sl = slice(g * B_GS, (g + 1) * B_GS)
        csg = cs[:, :, sl]
        cnt = jnp.asarray((hi - lo).astype(np.float32))[None, :, None]
        outs.append((csg[:, hi] - csg[:, lo]) / cnt - uf[:, :, sl])
    d = jnp.stack(outs, axis=2).astype(u.dtype)
    y = jnp.einsum('btgi,gij->btgj', d, w_pool).reshape(B, T, D_B)
    return y * s_pool


def even_mixer(h, h0, w_in, conv_w, conv_b, w_r, b_r, w_i, b_i, lam, w_pool, s_pool, w_out):
    z = h @ w_in
    u_a, g_a, u_b = z[..., :D_A], z[..., D_A:2 * D_A], z[..., 2 * D_A:]
    y_a, h_last = rglru_mixer(u_a, h0, conv_w, conv_b, w_r, b_r, w_i, b_i, lam)
    y_b = pool_mixer(u_b, w_pool, s_pool)
    y = jnp.concatenate([y_a * jax.nn.gelu(g_a), y_b], axis=-1) @ w_out
    return y, h_last


def odd_project(h, w_in, q_g, k_g):
    B, T, _ = h.shape
    z = h @ w_in
    o1 = 2 * D_C
    o2 = o1 + D_ATT
    o3 = o2 + D_KV
    u_c = z[..., :o1]
    q = rms_norm(z[..., o1:o2].reshape(B, T, N_Q_HEADS, HEAD_DIM), q_g)
    k = rms_norm(z[..., o2:o3].reshape(B, T, N_KV_HEADS, HEAD_DIM), k_g)
    v = z[..., o3:].reshape(B, T, N_KV_HEADS, HEAD_DIM)
    return u_c, q, k, v


def conformer_conv(u, conv_w, conv_b, ln_g, ln_b):
    val, gate = jnp.split(u, 2, axis=-1)
    g = val * jax.nn.sigmoid(gate)
    h = depthwise_conv(g, conv_w, conv_b, C_CONV // 2, C_CONV // 2)
    return jax.nn.silu(layer_norm(h, ln_g, ln_b))


def axial_rope_tables(T):
    rows = T // GRID_W
    pos = jnp.arange(rows * GRID_W)
    row = (pos // GRID_W).astype(jnp.float32)
    col = (pos % GRID_W).astype(jnp.float32)
    n_freq = HEAD_DIM // 4
    inv = ROPE_THETA ** (-jnp.arange(n_freq, dtype=jnp.float32) / n_freq)
    ang = jnp.stack([row[:, None] * inv, col[:, None] * inv], axis=1)
    return jnp.cos(ang), jnp.sin(ang)


def apply_axial_rope(x, cos, sin):
    B, T, H, _ = x.shape
    xa = x.astype(jnp.float32).reshape(B, T, H, 2, 2, HEAD_DIM // 4)
    x1, x2 = xa[..., 0, :], xa[..., 1, :]
    c = cos[None, :, None]
    s = sin[None, :, None]
    out = jnp.stack([x1 * c - x2 * s, x2 * c + x1 * s], axis=-2)
    return out.reshape(B, T, H, HEAD_DIM).astype(x.dtype)


def block_attention(q, k, v):
    B, Tq, _, _ = q.shape
    G = N_Q_HEADS // N_KV_HEADS
    nb = Tq // Q_BLOCK
    qb = q.reshape(B, nb, Q_BLOCK, N_KV_HEADS, G, HEAD_DIM).transpose(1, 0, 2, 3, 4, 5)
    scale = HEAD_DIM ** -0.5

    def one_block(qblk):
        s = jnp.einsum('bqhgd,bkhd->bhgqk', qblk, k, preferred_element_type=jnp.float32) * scale
        p = jax.nn.softmax(s, axis=-1).astype(v.dtype)
        return jnp.einsum('bhgqk,bkhd->bqhgd', p, v)

    o = lax.map(one_block, qb)
    return o.transpose(1, 0, 2, 3, 4, 5).reshape(B, Tq, D_ATT)


def odd_output(u_c, attn, conv_w, conv_b, ln_g, ln_b, w_out):
    y_c = conformer_conv(u_c, conv_w, conv_b, ln_g, ln_b)
    return jnp.concatenate([y_c, attn], axis=-1) @ w_out


def swiglu(h, w1, w3, w2):
    return (jax.nn.silu(h @ w1) * (h @ w3)) @ w2


def moe_swiglu(h, w_router, b_router, w1, w3, w2):
    logits = (h @ w_router).astype(jnp.float32) + b_router.astype(jnp.float32)
    top_v, top_i = lax.top_k(logits, TOP_K)
    gates = jax.nn.softmax(top_v, axis=-1)
    combine = jnp.sum(jax.nn.one_hot(top_i, N_EXPERTS, dtype=jnp.float32) * gates[..., None], axis=-2)
    combine = combine.astype(h.dtype)
    out = jnp.zeros_like(h)
    for e in range(N_EXPERTS):
        out = out + combine[..., e:e + 1] * swiglu(h, w1[e], w3[e], w2[e])
    return out


def setup_inputs(seed: int = 0) -> dict:
    key = jax.random.key(seed)
    ks = iter(jax.random.split(key, 64))
    f32 = jnp.float32

    def nrm(shape, scale):
        return jax.random.normal(next(ks), shape, f32) * scale

    def gain(shape):
        return 1.0 + nrm(shape, 0.02)

    D = D_MODEL
    u = jax.random.uniform(next(ks), (N_EVEN, 2, D_A), f32, minval=0.9, maxval=0.999)
    a0 = u ** (1.0 / A_C)
    lam = jnp.log(a0) - jnp.log1p(-a0)
    return {
        'x_prompt': nrm((BATCH, SEQ, D), 1.0),
        'x_sample': nrm((DEC_BATCH, DEC_SEQ, D), 1.0),
        'c': nrm((DEC_BATCH, D), 1.0),
        'state_rglru': nrm((DEC_BATCH, N_EVEN, 2, D_A), 0.5),
        'cache_k': nrm((DEC_BATCH, N_ODD, PAST_LEN, N_KV_HEADS, HEAD_DIM), 1.0),
        'cache_v': nrm((DEC_BATCH, N_ODD, PAST_LEN, N_KV_HEADS, HEAD_DIM), 1.0),
        'c_ctx': nrm((D,), 1.0),
        'w_mod': nrm((DEPTH, D, 6 * D), 0.5 * D ** -0.5),
        'b_mod': nrm((DEPTH, 6 * D), 0.01),
        'norm1': gain((DEPTH, D)),
        'norm2': gain((DEPTH, D)),
        'ev_w_in': nrm((N_EVEN, D, D_IN_EVEN), D ** -0.5),
        'a_conv_w': nrm((N_EVEN, A_CONV, D_A), A_CONV ** -0.5),
        'a_conv_b': nrm((N_EVEN, D_A), 0.01),
        'a_w_r': nrm((N_EVEN, 2, A_BLOCKS, A_BS, A_BS), A_BS ** -0.5),
        'a_b_r': nrm((N_EVEN, 2, D_A), 0.01),
        'a_w_i': nrm((N_EVEN, 2, A_BLOCKS, A_BS, A_BS), A_BS ** -0.5),
        'a_b_i': nrm((N_EVEN, 2, D_A), 0.01),
        'a_lam': lam,
        'b_w_pool': nrm((N_EVEN, B_GROUPS, B_GS, B_GS), B_GS ** -0.5),
        'b_scale': gain((N_EVEN, D_B)),
        'ev_w_out': nrm((N_EVEN, D, D), D ** -0.5),
        'od_w_in': nrm((N_ODD, D, D_IN_ODD), D ** -0.5),
        'c_conv_w': nrm((N_ODD, C_CONV, D_C), C_CONV ** -0.5),
        'c_conv_b': nrm((N_ODD, D_C), 0.01),
        'c_ln_g': gain((N_ODD, D_C)),
        'c_ln_b': nrm((N_ODD, D_C), 0.01),
        'q_norm': gain((N_ODD, HEAD_DIM)),
        'k_norm': gain((N_ODD, HEAD_DIM)),
        'od_w_out': nrm((N_ODD, D, D), D ** -0.5),
        'ff_w1': nrm((N_EVEN, D, D_FF), D ** -0.5),
        'ff_w3': nrm((N_EVEN, D, D_FF), D ** -0.5),
        'ff_w2': nrm((N_EVEN, D_FF, D), D_FF ** -0.5),
        'moe_w_router': nrm((N_ODD, D, N_EXPERTS), D ** -0.5),
        'moe_b_router': nrm((N_ODD, N_EXPERTS), 0.01),
        'moe_w1': nrm((N_ODD, N_EXPERTS, D, D_FF_E), D ** -0.5),
        'moe_w3': nrm((N_ODD, N_EXPERTS, D, D_FF_E), D ** -0.5),
        'moe_w2': nrm((N_ODD, N_EXPERTS, D_FF_E, D), D_FF_E ** -0.5),
        'norm_f': gain((D,)),
    }


def reference(x_prompt, x_sample, c, state_rglru, cache_k, cache_v, c_ctx, w_mod, b_mod,
              norm1, norm2, ev_w_in, a_conv_w, a_conv_b, a_w_r, a_b_r, a_w_i, a_b_i, a_lam,
              b_w_pool, b_scale, ev_w_out, od_w_in, c_conv_w, c_conv_b, c_ln_g, c_ln_b,
              q_norm, k_norm, od_w_out, ff_w1, ff_w3, ff_w2, moe_w_router, moe_b_router,
              moe_w1, moe_w3, moe_w2, norm_f):
    xp, xs = x_prompt, x_sample
    Bp = xp.shape[0]
    Ts = xs.shape[1]
    rope_cos, rope_sin = axial_rope_tables(Ts)
    h0_ctx = jnp.zeros((Bp, 2, D_A), xp.dtype)
    new_states, new_k, new_v = [], [], []
    for layer in range(DEPTH):
        li = layer // 2
        mp = ada_params(c_ctx, w_mod[layer], b_mod[layer])
        ms = [m[:, None, :] for m in ada_params(c, w_mod[layer], b_mod[layer])]
        hp = modulate(rms_norm(xp, norm1[layer]), mp[0], mp[1])
        hs = modulate(rms_norm(xs, norm1[layer]), ms[0], ms[1])
        if layer % 2 == 0:
            ev = (ev_w_in[li], a_conv_w[li], a_conv_b[li], a_w_r[li], a_b_r[li], a_w_i[li],
                  a_b_i[li], a_lam[li], b_w_pool[li], b_scale[li], ev_w_out[li])
            yp, h_ctx = even_mixer(hp, h0_ctx, *ev)
            ys, _ = even_mixer(hs, state_rglru[:, li], *ev)
            new_states.append(h_ctx)
        else:
            u_p, q_p, k_p, v_p = odd_project(hp, od_w_in[li], q_norm[li], k_norm[li])
            att_p = block_attention(q_p, k_p, v_p)
            yp = odd_output(u_p, att_p, c_conv_w[li], c_conv_b[li], c_ln_g[li], c_ln_b[li], od_w_out[li])
            new_k.append(k_p)
            new_v.append(v_p)
            u_s, q_s, k_s, v_s = odd_project(hs, od_w_in[li], q_norm[li], k_norm[li])
            q_s = apply_axial_rope(q_s, rope_cos, rope_sin)
            k_s = apply_axial_rope(k_s, rope_cos, rope_sin)
            k_all = jnp.concatenate([cache_k[:, li].astype(k_s.dtype), k_s], axis=1)
            v_all = jnp.concatenate([cache_v[:, li].astype(v_s.dtype), v_s], axis=1)
            att_s = block_attention(q_s, k_all, v_all)
            ys = odd_output(u_s, att_s, c_conv_w[li], c_conv_b[li], c_ln_g[li], c_ln_b[li], od_w_out[li])
        xp = xp + mp[2] * yp
        xs = xs + ms[2] * ys
        hp = modulate(rms_norm(xp, norm2[layer]), mp[3], mp[4])
        hs = modulate(rms_norm(xs, norm2[layer]), ms[3], ms[4])
        if layer % 2 == 0:
            fp = swiglu(hp, ff_w1[li], ff_w3[li], ff_w2[li])
            fs = swiglu(hs, ff_w1[li], ff_w3[li], ff_w2[li])
        else:
            fp = moe_swiglu(hp, moe_w_router[li], moe_b_router[li], moe_w1[li], moe_w3[li], moe_w2[li])
            fs = moe_swiglu(hs, moe_w_router[li], moe_b_router[li], moe_w1[li], moe_w3[li], moe_w2[li])
        xp = xp + mp[5] * fp
        xs = xs + ms[5] * fs
    y_prompt = rms_norm(xp, norm_f)
    y_sample = rms_norm(xs, norm_f)
    return (y_prompt, y_sample, jnp.stack(new_states, axis=1), jnp.stack(new_k, axis=1), jnp.stack(new_v, axis=1))
```

```python
import functools

import jax
import jax.numpy as jnp
from jax import lax
from jax.experimental import pallas as pl
from jax.experimental.pallas import tpu as pltpu

F32 = jnp.float32
BF16 = jnp.bfloat16

D = 1024
BATCH = 32
SEQ = 256
DEPTH = 4
DEC_BATCH = 8
DEC_SEQ = 1024
PAST_LEN = 256
GRID_W = 64
EPS = 1e-6
D_A = 512
A_BLOCKS = 8
A_BS = 64
A_CONV = 4
A_C = 8.0
D_B = 512
POOL_WINDOWS = (2, 4, 8, 16)
B_GS = 128
D_C = 512
C_CONV = 31
HEAD_DIM = 64
N_Q_HEADS = 8
N_KV_HEADS = 2
D_ATT = 512
D_KV = 128
ROPE_THETA = 10000.0
D_FF = 2816
N_EXPERTS = 8
D_FF_E = 1408
D_IN_EVEN = 1536
D_IN_ODD = 1792

SUB = 8
LANES = 128
N_PROMPT = BATCH * SEQ
N_SAMPLE = DEC_BATCH * DEC_SEQ
N_TOK = N_PROMPT + N_SAMPLE
P_GROUPS = BATCH // SUB
R_PROMPT = SEQ * SUB
R_SAMPLE = DEC_SEQ * SUB
TM = 512
TILES_PER_STREAM = N_PROMPT // TM
MOE_TM = 512
MOE_TILES = 2 * N_TOK // MOE_TM + N_EXPERTS
MOE_ROWS = MOE_TILES * MOE_TM
GATHER_RB = 1024
NEG_BIG = -3.0e38
VMEM_LIMIT = 56 * 1024 * 1024


def _params(n_axes, vmem=VMEM_LIMIT):
    return pltpu.CompilerParams(dimension_semantics=("arbitrary",) * n_axes, vmem_limit_bytes=vmem)


def _rms(x, g):
    ms = jnp.mean(x * x, axis=-1, keepdims=True)
    return x * lax.rsqrt(ms + EPS) * g


def _modulate(xn, shift, scale):
    tm = xn.shape[0]
    h = xn.reshape(tm // SUB, SUB, D) * (1.0 + scale)[None] + shift[None]
    return h.reshape(tm, D)


def _gated_add(x, gate, y):
    tm = x.shape[0]
    return x + (y.reshape(tm // SUB, SUB, D) * gate[None]).reshape(tm, D)


def _mod_spec():
    return pl.BlockSpec((None, SUB, D), lambda i, *_: (i // TILES_PER_STREAM, 0, 0))


def _row_spec(width, tm=TM):
    return pl.BlockSpec((tm, width), lambda i, *_: (i, 0))


def _full_spec(shape):
    nd = len(shape)
    return pl.BlockSpec(shape, lambda i, *_: (0,) * nd)


def _ada_kernel(c_ref, w_ref, b_ref, o_ref):
    c = c_ref[...]
    s = (c * jax.nn.sigmoid(c)).astype(BF16)
    o_ref[...] = jnp.dot(s, w_ref[...].astype(BF16), preferred_element_type=F32) + b_ref[...]


def _ada_params(cond16, w_mod, b_mod):
    return pl.pallas_call(
        _ada_kernel,
        grid=(DEPTH, 6),
        in_specs=[
            pl.BlockSpec((16, D), lambda l, j: (0, 0)),
            pl.BlockSpec((None, D, D), lambda l, j: (l, 0, j)),
            pl.BlockSpec((None, None, 1, D), lambda l, j: (l, j, 0, 0)),
        ],
        out_specs=pl.BlockSpec((None, None, 16, D), lambda l, j: (l, j, 0, 0)),
        out_shape=jax.ShapeDtypeStruct((DEPTH, 6, 16, D), F32),
        compiler_params=_params(2),
        name="ada_params",
    )(cond16, w_mod, b_mod.reshape(DEPTH, 6, 1, D))


def _even_in_kernel(x_ref, g_ref, sh_ref, sc_ref, w_ref, z_ref):
    h = _modulate(_rms(x_ref[...], g_ref[...]), sh_ref[...], sc_ref[...])
    z_ref[...] = jnp.dot(h.astype(BF16), w_ref[...], preferred_element_type=F32)


def _even_in(x, g, shift, scale, w_bf16):
    return pl.pallas_call(
        _even_in_kernel,
        grid=(N_TOK // TM,),
        in_specs=[_row_spec(D), _full_spec((1, D)), _mod_spec(), _mod_spec(), _full_spec((D, D_IN_EVEN))],
        out_specs=_row_spec(D_IN_EVEN),
        out_shape=jax.ShapeDtypeStruct((N_TOK, D_IN_EVEN), F32),
        compiler_params=_params(1),
        name="even_in",
    )(x, g, shift, scale, w_bf16)


RG_CHUNK = 256
RG_PAD = 16


def _rglru_kernel(u_ref, ga_ref, cw_ref, cb_ref, wg_ref, bg_ref, lam_ref, h0_ref,
                  y_ref, hl_ref, xpad, a0, b0, a1, b1, *, rows):
    steps = rows // SUB
    n_chunks = rows // RG_CHUNK

    xpad[0:RG_PAD, :] = jnp.zeros((RG_PAD, LANES), F32)
    xpad[RG_PAD + rows:RG_PAD + rows + RG_PAD, :] = jnp.zeros((RG_PAD, LANES), F32)

    def copy_chunk(c, carry):
        r = pl.multiple_of(c * RG_CHUNK, RG_CHUNK)
        xpad[pl.ds(RG_PAD + r, RG_CHUNK), :] = u_ref[pl.ds(r, RG_CHUNK), :]
        return carry

    lax.fori_loop(0, n_chunks, copy_chunk, 0)

    lam = lam_ref[...]
    softplus_neg = jnp.maximum(-lam, 0.0) + jnp.log1p(jnp.exp(-jnp.abs(lam)))
    a_refs = (a0, a1)
    b_refs = (b0, b1)

    def gate_chunk(c, carry):
        r = pl.multiple_of(c * RG_CHUNK, RG_CHUNK)
        xc = jnp.zeros((RG_CHUNK, LANES), F32) + cb_ref[...]
        for k in range(A_CONV):
            xc = xc + cw_ref[k:k + 1, :] * xpad[pl.ds(r + SUB * k, RG_CHUNK), :]
        pre = jnp.dot(xc.astype(BF16), wg_ref[...], preferred_element_type=F32) + bg_ref[...]
        for d in range(2):
            rg = jax.nn.sigmoid(pre[:, d * 256:d * 256 + LANES])
            ig = jax.nn.sigmoid(pre[:, d * 256 + LANES:(d + 1) * 256])
            log_a = (-A_C) * rg * softplus_neg[:, d * LANES:(d + 1) * LANES]
            a = jnp.exp(log_a)
            one_minus_a2 = -jnp.tanh(log_a) * (a * a + 1.0)
            a_refs[d][pl.ds(r, RG_CHUNK), :] = a
            b_refs[d][pl.ds(r, RG_CHUNK), :] = jnp.sqrt(one_minus_a2) * (ig * xc)
        return carry

    lax.fori_loop(0, n_chunks, gate_chunk, 0)

    def step(t, carry):
        hf, hb = carry
        rf = pl.multiple_of(t * SUB, SUB)
        rb = pl.multiple_of((steps - 1 - t) * SUB, SUB)
        hf = a0[pl.ds(rf, SUB), :] * hf + b0[pl.ds(rf, SUB), :]
        b0[pl.ds(rf, SUB), :] = hf
        hb = a1[pl.ds(rb, SUB), :] * hb + b1[pl.ds(rb, SUB), :]
        b1[pl.ds(rb, SUB), :] = hb
        return hf, hb

    hf, hb = lax.fori_loop(0, steps, step, (h0_ref[0], h0_ref[1]), unroll=8)
    hl_ref[0] = hf
    hl_ref[1] = hb

    def out_chunk(c, carry):
        r = pl.multiple_of(c * RG_CHUNK, RG_CHUNK)
        y = (b0[pl.ds(r, RG_CHUNK), :] + b1[pl.ds(r, RG_CHUNK), :]) * jax.nn.gelu(ga_ref[pl.ds(r, RG_CHUNK), :])
        y_ref[pl.ds(r, RG_CHUNK), :] = y.astype(BF16)
        return carry

    lax.fori_loop(0, n_chunks, out_chunk, 0)


def _rglru_call(z, cw, cb, wg, bg, lam, h0, *, rows, groups, row_block0):
    n_cb = D_A // LANES
    in_specs = [
        pl.BlockSpec((rows, LANES), lambda g, j: (row_block0 + g, j)),
        pl.BlockSpec((rows, LANES), lambda g, j: (row_block0 + g, n_cb + j)),
        pl.BlockSpec((A_CONV, LANES), lambda g, j: (0, j)),
        pl.BlockSpec((1, LANES), lambda g, j: (0, j)),
        pl.BlockSpec((None, LANES, 4 * LANES), lambda g, j: (j, 0, 0)),
        pl.BlockSpec((None, 1, 4 * LANES), lambda g, j: (j, 0, 0)),
        pl.BlockSpec((None, 1, 2 * LANES), lambda g, j: (j, 0, 0)),
        pl.BlockSpec((None, 2, SUB, LANES), lambda g, j: (g, 0, 0, j)),
    ]
    return pl.pallas_call(
        functools.partial(_rglru_kernel, rows=rows),
        grid=(groups, n_cb),
        in_specs=in_specs,
        out_specs=[
            pl.BlockSpec((rows, LANES), lambda g, j: (g, j)),
            pl.BlockSpec((None, 2, SUB, LANES), lambda g, j: (g, 0, 0, j)),
        ],
        out_shape=[
            jax.ShapeDtypeStruct((groups * rows, D_A), BF16),
            jax.ShapeDtypeStruct((groups, 2, SUB, D_A), F32),
        ],
        scratch_shapes=[pltpu.VMEM((rows + 2 * RG_PAD, LANES), F32)] + [pltpu.VMEM((rows, LANES), F32)] * 4,
        compiler_params=_params(2),
        name=f"rglru_{rows}",
    )(z, z, cw, cb, wg, bg, lam, h0)


POOL_CHUNK = 256
POOL_PAD = 64


def _pool_kernel(u_ref, w_ref, s_ref, y_ref, xpad, *, rows):
    steps = rows // SUB
    n_chunks = rows // POOL_CHUNK
    j = pl.program_id(1)

    xpad[0:POOL_PAD, :] = jnp.zeros((POOL_PAD, LANES), F32)
    xpad[POOL_PAD + rows:POOL_PAD + rows + POOL_PAD, :] = jnp.zeros((POOL_PAD, LANES), F32)

    def copy_chunk(c, carry):
        r = pl.multiple_of(c * POOL_CHUNK, POOL_CHUNK)
        xpad[pl.ds(POOL_PAD + r, POOL_CHUNK), :] = u_ref[pl.ds(r, POOL_CHUNK), :]
        return carry

    lax.fori_loop(0, n_chunks, copy_chunk, 0)

    for gi, win in enumerate(POOL_WINDOWS):
        half = win // 2

        @pl.when(j == gi)
        def _(half=half, win=win):
            def chunk(c, carry):
                r = pl.multiple_of(c * POOL_CHUNK, POOL_CHUNK)
                acc = xpad[pl.ds(POOL_PAD + r - SUB * half, POOL_CHUNK), :]
                for s in range(1, win):
                    acc = acc + xpad[pl.ds(POOL_PAD + r + SUB * (s - half), POOL_CHUNK), :]
                t = (r + lax.broadcasted_iota(jnp.int32, (POOL_CHUNK, 1), 0)) // SUB
                cnt = jnp.minimum(t + half, steps) - jnp.maximum(t - half, 0)
                dlt = acc / cnt.astype(F32) - xpad[pl.ds(POOL_PAD + r, POOL_CHUNK), :]
                y = jnp.dot(dlt.astype(BF16), w_ref[...], preferred_element_type=F32) * s_ref[...]
                y_ref[pl.ds(r, POOL_CHUNK), :] = y.astype(BF16)
                return carry

            lax.fori_loop(0, n_chunks, chunk, 0)


def _pool_call(z, w_pool_bf16, s_pool, *, rows, groups, row_block0):
    n_cb = D_B // LANES
    col0 = 2 * D_A // LANES
    in_specs = [
        pl.BlockSpec((rows, LANES), lambda g, j: (row_block0 + g, col0 + j)),
        pl.BlockSpec((None, B_GS, B_GS), lambda g, j: (j, 0, 0)),
        pl.BlockSpec((1, LANES), lambda g, j: (0, j)),
    ]
    return pl.pallas_call(
        functools.partial(_pool_kernel, rows=rows),
        grid=(groups, n_cb),
        in_specs=in_specs,
        out_specs=pl.BlockSpec((rows, LANES), lambda g, j: (g, j)),
        out_shape=jax.ShapeDtypeStruct((groups * rows, D_B), BF16),
        scratch_shapes=[pltpu.VMEM((rows + 2 * POOL_PAD, LANES), F32)],
        compiler_params=_params(2),
        name=f"pool_{rows}",
    )(z, w_pool_bf16, s_pool)


def _stream_specs(width):
    prompt = pl.BlockSpec((TM, width), lambda i, *_: (jnp.minimum(i, TILES_PER_STREAM - 1), 0))
    sample = pl.BlockSpec((TM, width), lambda i, *_: (jnp.maximum(i - TILES_PER_STREAM, 0), 0))
    return [prompt, sample]


def _pick_stream(prompt_ref, sample_ref):
    return jnp.where(pl.program_id(0) >= TILES_PER_STREAM, sample_ref[...], prompt_ref[...])


def _mix_out_kernel(x_ref, yap_ref, yas_ref, ybp_ref, ybs_ref, wa_ref, wb_ref, gate_ref, o_ref):
    y = jnp.dot(_pick_stream(yap_ref, yas_ref), wa_ref[...], preferred_element_type=F32)
    y = y + jnp.dot(_pick_stream(ybp_ref, ybs_ref), wb_ref[...], preferred_element_type=F32)
    o_ref[...] = _gated_add(x_ref[...], gate_ref[...], y)


def _even_out(x, ya, yb, wa, wb, gate):
    half = D // 2
    return pl.pallas_call(
        _mix_out_kernel,
        grid=(N_TOK // TM,),
        in_specs=[_row_spec(D)] + _stream_specs(half) + _stream_specs(half) +
                 [_full_spec((half, D)), _full_spec((half, D)), _mod_spec()],
        out_specs=_row_spec(D),
        out_shape=jax.ShapeDtypeStruct((N_TOK, D), F32),
        compiler_params=_params(1),
        name="even_out",
    )(x, *ya, *yb, wa, wb, gate)


def _ffn_kernel(x_ref, g_ref, sh_ref, sc_ref, gate_ref, w1_ref, w3_ref, w2_ref, o_ref, h_scr, acc_scr):
    j = pl.program_id(1)

    @pl.when(j == 0)
    def _():
        h = _modulate(_rms(x_ref[...], g_ref[...]), sh_ref[...], sc_ref[...])
        h_scr[...] = h.astype(BF16)

    h = h_scr[...]
    a = jnp.dot(h, w1_ref[...], preferred_element_type=F32)
    b = jnp.dot(h, w3_ref[...], preferred_element_type=F32)
    u = (a * jax.nn.sigmoid(a) * b).astype(BF16)
    y = jnp.dot(u, w2_ref[...], preferred_element_type=F32)

    @pl.when(j == 0)
    def _():
        acc_scr[...] = y

    @pl.when(j > 0)
    def _():
        acc_scr[...] += y

    @pl.when(j == pl.num_programs(1) - 1)
    def _():
        o_ref[...] = _gated_add(x_ref[...], gate_ref[...], acc_scr[...])


def _ffn(x, g, shift, scale, gate, w1, w3, w2):
    n_ff = D_FF // D_FF_E
    return pl.pallas_call(
        _ffn_kernel,
        grid=(N_TOK // TM, n_ff),
        in_specs=[
            pl.BlockSpec((TM, D), lambda i, j: (i, 0)),
            pl.BlockSpec((1, D), lambda i, j: (0, 0)),
            pl.BlockSpec((None, SUB, D), lambda i, j: (i // TILES_PER_STREAM, 0, 0)),
            pl.BlockSpec((None, SUB, D), lambda i, j: (i // TILES_PER_STREAM, 0, 0)),
            pl.BlockSpec((None, SUB, D), lambda i, j: (i // TILES_PER_STREAM, 0, 0)),
            pl.BlockSpec((D, D_FF_E), lambda i, j: (0, j)),
            pl.BlockSpec((D, D_FF_E), lambda i, j: (0, j)),
            pl.BlockSpec((D_FF_E, D), lambda i, j: (j, 0)),
        ],
        out_specs=pl.BlockSpec((TM, D), lambda i, j: (i, 0)),
        out_shape=jax.ShapeDtypeStruct((N_TOK, D), F32),
        scratch_shapes=[pltpu.VMEM((TM, D), BF16), pltpu.VMEM((TM, D), F32)],
        compiler_params=_params(2),
        name="ffn",
    )(x, g, shift, scale, gate, w1, w3, w2)


def _head_rms(x, ones_bd, g):
    ms = jnp.dot(x * x, ones_bd, preferred_element_type=F32, precision=lax.Precision.HIGHEST)
    return x * lax.rsqrt(ms + EPS) * g


def _rope(x, cos, sin_signed):
    w = x.shape[-1]
    lane = lax.broadcasted_iota(jnp.int32, x.shape, 1)
    first = (lane % 32) < 16
    partner = jnp.where(first, pltpu.roll(x, w - 16, 1), pltpu.roll(x, 16, 1))
    return x * cos + partner * sin_signed


def _odd_in_kernel(x_ref, g_ref, sh_ref, sc_ref, w_ref, bdq_ref, bdk_ref, qg_ref, kg_ref, cos_ref, sin_ref,
                   glu_ref, q_ref, k_ref, v_ref):
    i = pl.program_id(0)
    h = _modulate(_rms(x_ref[...], g_ref[...]), sh_ref[...], sc_ref[...])
    z = jnp.dot(h.astype(BF16), w_ref[...], preferred_element_type=F32)
    glu_ref[...] = z[:, :D_C] * jax.nn.sigmoid(z[:, D_C:2 * D_C])
    o1 = 2 * D_C
    o2 = o1 + D_ATT
    o3 = o2 + D_KV
    q = _head_rms(z[:, o1:o2], bdq_ref[...], qg_ref[...])
    k = _head_rms(z[:, o2:o3], bdk_ref[...], kg_ref[...])
    cos = cos_ref[...]
    sin = sin_ref[...]
    is_sample = i >= TILES_PER_STREAM
    q_r = _rope(q, jnp.concatenate([cos] * 4, axis=1), jnp.concatenate([sin] * 4, axis=1))
    k_r = _rope(k, cos, sin)
    q_ref[...] = jnp.where(is_sample, q_r, q)
    k_ref[...] = jnp.where(is_sample, k_r, k)
    v_ref[...] = z[:, o3:]


def _odd_in(x, g, shift, scale, w_bf16, bdq, bdk, qg, kg, cos_t, sin_t):
    rope_spec = pl.BlockSpec((TM, D_KV), lambda i: (jnp.maximum(i - TILES_PER_STREAM, 0), 0))
    return pl.pallas_call(
        _odd_in_kernel,
        grid=(N_TOK // TM,),
        in_specs=[_row_spec(D), _full_spec((1, D)), _mod_spec(), _mod_spec(), _full_spec((D, D_IN_ODD)),
                  _full_spec((D_ATT, D_ATT)), _full_spec((D_KV, D_KV)), _full_spec((1, D_ATT)),
                  _full_spec((1, D_KV)), rope_spec, rope_spec],
        out_specs=[_row_spec(D_C), _row_spec(D_ATT), _row_spec(D_KV), _row_spec(D_KV)],
        out_shape=[jax.ShapeDtypeStruct((N_TOK, D_C), F32), jax.ShapeDtypeStruct((N_TOK, D_ATT), F32),
                   jax.ShapeDtypeStruct((N_TOK, D_KV), F32), jax.ShapeDtypeStruct((N_TOK, D_KV), F32)],
        compiler_params=_params(1),
        name="odd_in",
    )(x, g, shift, scale, w_bf16, bdq, bdk, qg, kg, cos_t, sin_t)


def _attend(q, k_all, v_all, o_ref):
    scale = HEAD_DIM ** -0.5
    group = N_Q_HEADS // N_KV_HEADS
    for h in range(N_KV_HEADS):
        kh = k_all[:, h * HEAD_DIM:(h + 1) * HEAD_DIM].astype(BF16)
        vh = v_all[:, h * HEAD_DIM:(h + 1) * HEAD_DIM].astype(BF16)
        for gq in range(group):
            hd = h * group + gq
            qh = (q[:, hd * HEAD_DIM:(hd + 1) * HEAD_DIM] * scale).astype(BF16)
            s = lax.dot_general(qh, kh, (((1,), (1,)), ((), ())), preferred_element_type=F32)
            m = jnp.max(s, axis=-1, keepdims=True)
            p = jnp.exp(s - m)
            l = jnp.sum(p, axis=-1, keepdims=True)
            o = jnp.dot(p.astype(BF16), vh, preferred_element_type=F32) / l
            o_ref[:, hd * HEAD_DIM:(hd + 1) * HEAD_DIM] = o.astype(BF16)


def _attn_prompt_kernel(q_ref, k_ref, v_ref, o_ref):
    _attend(q_ref[...], k_ref[...], v_ref[...], o_ref)


def _attn_sample_kernel(q_ref, k_ref, v_ref, ck_ref, cv_ref, o_ref):
    k_all = jnp.concatenate([ck_ref[...], k_ref[...].reshape(DEC_SEQ, D_KV)], axis=0)
    v_all = jnp.concatenate([cv_ref[...], v_ref[...].reshape(DEC_SEQ, D_KV)], axis=0)
    _attend(q_ref[...], k_all, v_all, o_ref)


def _attention(q, k, v, cache_k_l, cache_v_l):
    n_chunks = N_TOK // R_PROMPT
    q3 = q.reshape(n_chunks, SEQ, SUB * D_ATT)
    k3 = k.reshape(n_chunks, SEQ, SUB * D_KV)
    v3 = v.reshape(n_chunks, SEQ, SUB * D_KV)
    out_shape = jax.ShapeDtypeStruct((P_GROUPS, SEQ, SUB * D_ATT), BF16)
    att_p = pl.pallas_call(
        _attn_prompt_kernel,
        grid=(BATCH,),
        in_specs=[
            pl.BlockSpec((None, SEQ, D_ATT), lambda s: (s // SUB, 0, s % SUB)),
            pl.BlockSpec((None, SEQ, D_KV), lambda s: (s // SUB, 0, s % SUB)),
            pl.BlockSpec((None, SEQ, D_KV), lambda s: (s // SUB, 0, s % SUB)),
        ],
        out_specs=pl.BlockSpec((None, SEQ, D_ATT), lambda s: (s // SUB, 0, s % SUB)),
        out_shape=out_shape,
        compiler_params=_params(1),
        name="attn_prompt",
    )(q3, k3, v3)
    q_chunks = DEC_SEQ // SEQ
    att_s = pl.pallas_call(
        _attn_sample_kernel,
        grid=(DEC_BATCH, q_chunks),
        in_specs=[
            pl.BlockSpec((None, SEQ, D_ATT), lambda b, c: (P_GROUPS + c, 0, b)),
            pl.BlockSpec((q_chunks, SEQ, D_KV), lambda b, c: (1, 0, b)),
            pl.BlockSpec((q_chunks, SEQ, D_KV), lambda b, c: (1, 0, b)),
            pl.BlockSpec((None, PAST_LEN, D_KV), lambda b, c: (b, 0, 0)),
            pl.BlockSpec((None, PAST_LEN, D_KV), lambda b, c: (b, 0, 0)),
        ],
        out_specs=pl.BlockSpec((None, SEQ, D_ATT), lambda b, c: (c, 0, b)),
        out_shape=out_shape,
        compiler_params=_params(2),
        name="attn_sample",
    )(q3, k3, v3, cache_k_l, cache_v_l)
    return att_p.reshape(N_PROMPT, D_ATT), att_s.reshape(N_SAMPLE, D_ATT)


CV_CHUNK = 64
CV_PAD = (C_CONV // 2) * SUB


def _conv_kernel(u_ref, w_ref, b_ref, y_ref, xpad, *, rows):
    n_copy = rows // 256
    xpad[0:CV_PAD, :] = jnp.zeros((CV_PAD, LANES), F32)
    xpad[CV_PAD + rows:CV_PAD + rows + CV_PAD, :] = jnp.zeros((CV_PAD, LANES), F32)

    def copy_chunk(c, carry):
        r = pl.multiple_of(c * 256, 256)
        xpad[pl.ds(CV_PAD + r, 256), :] = u_ref[pl.ds(r, 256), :]
        return carry

    lax.fori_loop(0, n_copy, copy_chunk, 0)

    def chunk(c, carry):
        r = pl.multiple_of(c * CV_CHUNK, CV_CHUNK)
        acc = jnp.zeros((CV_CHUNK, LANES), F32) + b_ref[...]
        for k in range(C_CONV):
            acc = acc + w_ref[k:k + 1, :] * xpad[pl.ds(r + SUB * k, CV_CHUNK), :]
        y_ref[pl.ds(r, CV_CHUNK), :] = acc
        return carry

    lax.fori_loop(0, rows // CV_CHUNK, chunk, 0)


def _conv_call(glu, w, b, *, rows, groups, row_block0):
    n_cb = D_C // LANES
    in_specs = [
        pl.BlockSpec((rows, LANES), lambda g, j: (row_block0 + g, j)),
        pl.BlockSpec((C_CONV, LANES), lambda g, j: (0, j)),
        pl.BlockSpec((1, LANES), lambda g, j: (0, j)),
    ]
    return pl.pallas_call(
        functools.partial(_conv_kernel, rows=rows),
        grid=(groups, n_cb),
        in_specs=in_specs,
        out_specs=pl.BlockSpec((rows, LANES), lambda g, j: (g, j)),
        out_shape=jax.ShapeDtypeStruct((groups * rows, D_C), F32),
        scratch_shapes=[pltpu.VMEM((rows + 2 * CV_PAD, LANES), F32)],
        compiler_params=_params(2),
        name=f"conv_{rows}",
    )(glu, w, b)


def _odd_out_kernel(x_ref, hcp_ref, hcs_ref, attp_ref, atts_ref, lng_ref, lnb_ref, wc_ref, wa_ref, gate_ref,
                    g2_ref, sh_ref, sc_ref, wr_ref, br_ref, x1_ref, h2_ref, route_ref):
    hc = _pick_stream(hcp_ref, hcs_ref)
    mu = jnp.mean(hc, axis=-1, keepdims=True)
    xc = hc - mu
    var = jnp.mean(xc * xc, axis=-1, keepdims=True)
    ln = xc * lax.rsqrt(var + EPS) * lng_ref[...] + lnb_ref[...]
    yc = (ln * jax.nn.sigmoid(ln)).astype(BF16)
    y = jnp.dot(yc, wc_ref[...], preferred_element_type=F32)
    y = y + jnp.dot(_pick_stream(attp_ref, atts_ref), wa_ref[...], preferred_element_type=F32)
    x1 = _gated_add(x_ref[...], gate_ref[...], y)
    x1_ref[...] = x1
    h2 = _modulate(_rms(x1, g2_ref[...]), sh_ref[...], sc_ref[...])
    h2_ref[...] = h2
    logits = jnp.dot(h2, wr_ref[...], preferred_element_type=F32, precision=lax.Precision.HIGHEST) + br_ref[...]
    lane = lax.broadcasted_iota(jnp.int32, logits.shape, 1).astype(F32)
    lg = jnp.where(lane < N_EXPERTS, logits, NEG_BIG)
    m1 = jnp.max(lg, axis=-1, keepdims=True)
    i1 = jnp.min(jnp.where(lg == m1, lane, float(LANES)), axis=-1, keepdims=True)
    lg2 = jnp.where(lane == i1, NEG_BIG, lg)
    m2 = jnp.max(lg2, axis=-1, keepdims=True)
    i2 = jnp.min(jnp.where(lg2 == m2, lane, float(LANES)), axis=-1, keepdims=True)
    e = jnp.exp(m2 - m1)
    den = 1.0 + e
    route = jnp.where(lane == 0.0, i1, jnp.where(lane == 1.0, i2,
                      jnp.where(lane == 2.0, 1.0 / den, jnp.where(lane == 3.0, e / den, 0.0))))
    route_ref[...] = route


def _odd_out(x, hc, att, lng, lnb, wc, wa, gate, g2, shift2, scale2, wr_pad, br_pad):
    half = D // 2
    return pl.pallas_call(
        _odd_out_kernel,
        grid=(N_TOK // TM,),
        in_specs=[_row_spec(D)] + _stream_specs(half) + _stream_specs(half) +
                 [_full_spec((1, half)), _full_spec((1, half)),
                  _full_spec((half, D)), _full_spec((half, D)), _mod_spec(),
                  _full_spec((1, D)), _mod_spec(), _mod_spec(), _full_spec((D, LANES)), _full_spec((1, LANES))],
        out_specs=[_row_spec(D), _row_spec(D), _row_spec(LANES)],
        out_shape=[jax.ShapeDtypeStruct((N_TOK, D), F32), jax.ShapeDtypeStruct((N_TOK, D), F32),
                   jax.ShapeDtypeStruct((N_TOK, LANES), F32)],
        compiler_params=_params(1),
        name="odd_out",
    )(x, *hc, *att, lng, lnb, wc, wa, gate, g2, shift2, scale2, wr_pad, br_pad)


def _row_copy(src_ref, dst_ref, src_row, dst_row, sem):
    return pltpu.make_async_copy(src_ref.at[pl.ds(src_row, 1)], dst_ref.at[pl.ds(dst_row, 1)], sem)


def _gather_kernel(idx_ref, src_ref, dst_ref, sem):
    base = pl.program_id(0) * GATHER_RB

    def issue(r, carry):
        _row_copy(src_ref, dst_ref, idx_ref[0, r], base + r, sem).start()
        return carry

    lax.fori_loop(0, GATHER_RB, issue, 0)

    def drain(r, carry):
        _row_copy(src_ref, dst_ref, idx_ref[0, r], base + r, sem).wait()
        return carry

    lax.fori_loop(0, GATHER_RB, drain, 0)


def _gather_rows(src, idx):
    n = idx.shape[0]
    steps = n // GATHER_RB
    return pl.pallas_call(
        _gather_kernel,
        grid=(steps,),
        in_specs=[
            pl.BlockSpec((None, 1, GATHER_RB), lambda i: (i, 0, 0), memory_space=pltpu.SMEM),
            pl.BlockSpec(memory_space=pl.ANY),
        ],
        out_specs=pl.BlockSpec(memory_space=pl.ANY),
        out_shape=jax.ShapeDtypeStruct((n, src.shape[1]), src.dtype),
        scratch_shapes=[pltpu.SemaphoreType.DMA(())],
        compiler_params=_params(1),
        name="gather_rows",
    )(idx.reshape(steps, 1, GATHER_RB), src)


def _moe_kernel(te_ref, nu_ref, xs_ref, gate_ref, w1_ref, w3_ref, w2_ref, o_ref):
    del te_ref
    i = pl.program_id(0)

    @pl.when(i < nu_ref[0])
    def _():
        h = xs_ref[...].astype(BF16)
        a = jnp.dot(h, w1_ref[...], preferred_element_type=F32)
        b = jnp.dot(h, w3_ref[...], preferred_element_type=F32)
        u = (a * jax.nn.sigmoid(a) * b).astype(BF16)
        y = jnp.dot(u, w2_ref[...], preferred_element_type=F32)
        o_ref[...] = y * gate_ref[...]

    @pl.when(i >= nu_ref[0])
    def _():
        o_ref[...] = jnp.zeros(o_ref.shape, F32)


def _moe_experts(tile_expert, n_used, xs, gate_sorted, w1, w3, w2):
    grid_spec = pltpu.PrefetchScalarGridSpec(
        num_scalar_prefetch=2,
        grid=(MOE_TILES,),
        in_specs=[
            pl.BlockSpec((MOE_TM, D), lambda i, te, nu: (i, 0)),
            pl.BlockSpec((MOE_TM, 1), lambda i, te, nu: (i, 0)),
            pl.BlockSpec((None, D, D_FF_E), lambda i, te, nu: (te[i], 0, 0)),
            pl.BlockSpec((None, D, D_FF_E), lambda i, te, nu: (te[i], 0, 0)),
            pl.BlockSpec((None, D_FF_E, D), lambda i, te, nu: (te[i], 0, 0)),
        ],
        out_specs=pl.BlockSpec((MOE_TM, D), lambda i, te, nu: (i, 0)),
    )
    return pl.pallas_call(
        _moe_kernel,
        grid_spec=grid_spec,
        out_shape=jax.ShapeDtypeStruct((MOE_ROWS, D), F32),
        compiler_params=_params(1),
        name="moe_experts",
    )(tile_expert, n_used, xs, gate_sorted, w1, w3, w2)


def _route_plan(route):
    ids = route[:, 0:2].astype(jnp.int32).reshape(-1)
    gates = route[:, 2:4].reshape(-1)
    onehot = (ids[:, None] == jnp.arange(N_EXPERTS, dtype=jnp.int32)[None, :]).astype(jnp.int32)
    csum = jnp.cumsum(onehot, axis=0)
    rank = jnp.sum(csum * onehot, axis=1) - 1
    counts = csum[-1]
    tiles = (counts + MOE_TM - 1) // MOE_TM
    tile_end = jnp.cumsum(tiles)
    group_start = (tile_end - tiles) * MOE_TM
    pos = jnp.sum(onehot * group_start[None, :], axis=1) + rank
    src = jnp.zeros((MOE_ROWS,), jnp.int32).at[pos].set(jnp.arange(2 * N_TOK, dtype=jnp.int32) // 2)
    gate_sorted = jnp.zeros((MOE_ROWS,), F32).at[pos].set(gates)
    tile_ids = jnp.arange(MOE_TILES, dtype=jnp.int32)
    tile_expert = jnp.minimum(jnp.sum((tile_ids[:, None] >= tile_end[None, :]).astype(jnp.int32), axis=1),
                              N_EXPERTS - 1)
    n_used = tile_end[-1:].astype(jnp.int32)
    return pos, src, gate_sorted.reshape(MOE_ROWS, 1), tile_expert, n_used


def _combine_kernel(x_ref, y_ref, gate_ref, o_ref):
    y = y_ref[:, :D] + y_ref[:, D:]
    o_ref[...] = _gated_add(x_ref[...], gate_ref[...], y)


def _combine(x, y2, gate):
    return pl.pallas_call(
        _combine_kernel,
        grid=(N_TOK // TM,),
        in_specs=[_row_spec(D), _row_spec(2 * D), _mod_spec()],
        out_specs=_row_spec(D),
        out_shape=jax.ShapeDtypeStruct((N_TOK, D), F32),
        compiler_params=_params(1),
        name="moe_combine",
    )(x, y2, gate)


def _final_kernel(x_ref, g_ref, o_ref):
    o_ref[...] = _rms(x_ref[...], g_ref[...])


def _final_norm(x, g):
    return pl.pallas_call(
        _final_kernel,
        grid=(N_TOK // TM,),
        in_specs=[_row_spec(D), _full_spec((1, D))],
        out_specs=_row_spec(D),
        out_shape=jax.ShapeDtypeStruct((N_TOK, D), F32),
        compiler_params=_params(1),
        name="final_norm",
    )(x, g)


def _block_diag_gates(w_r, w_i, b_r, b_i):
    def bd(w):
        w4 = w.reshape(4, 2, A_BS, A_BS)
        z = jnp.zeros((4, A_BS, A_BS), w.dtype)
        top = jnp.concatenate([w4[:, 0], z], axis=2)
        bot = jnp.concatenate([z, w4[:, 1]], axis=2)
        return jnp.concatenate([top, bot], axis=1)

    wg = jnp.concatenate([bd(w_r[0]), bd(w_i[0]), bd(w_r[1]), bd(w_i[1])], axis=2)
    bg = jnp.concatenate([b_r[0].reshape(4, 1, LANES), b_i[0].reshape(4, 1, LANES),
                          b_r[1].reshape(4, 1, LANES), b_i[1].reshape(4, 1, LANES)], axis=2)
    return wg.astype(BF16), bg


def _head_mean_matrix(width):
    idx = jnp.arange(width) // HEAD_DIM
    return (idx[:, None] == idx[None, :]).astype(F32) / HEAD_DIM


def _rope_tables():
    pos = jnp.arange(DEC_SEQ)
    row = (pos // GRID_W).astype(F32)
    col = (pos % GRID_W).astype(F32)
    n_freq = HEAD_DIM // 4
    inv = ROPE_THETA ** (-jnp.arange(n_freq, dtype=F32) / n_freq)
    ang = jnp.stack([row[:, None] * inv, col[:, None] * inv], axis=1)
    cos = jnp.cos(ang)
    sin = jnp.sin(ang)
    cos_h = jnp.stack([cos, cos], axis=2).reshape(DEC_SEQ, HEAD_DIM)
    sin_h = jnp.stack([-sin, sin], axis=2).reshape(DEC_SEQ, HEAD_DIM)
    cos_t = jnp.tile(cos_h, (1, 2))
    sin_t = jnp.tile(sin_h, (1, 2))
    cos_t = jnp.broadcast_to(cos_t[:, None, :], (DEC_SEQ, SUB, D_KV)).reshape(N_SAMPLE, D_KV)
    sin_t = jnp.broadcast_to(sin_t[:, None, :], (DEC_SEQ, SUB, D_KV)).reshape(N_SAMPLE, D_KV)
    return cos_t, sin_t


def _to_time_major(x_prompt, x_sample):
    xp = x_prompt.reshape(P_GROUPS, SUB, SEQ, D).transpose(0, 2, 1, 3).reshape(N_PROMPT, D)
    xs = x_sample.transpose(1, 0, 2).reshape(N_SAMPLE, D)
    return jnp.concatenate([xp, xs], axis=0)


def _prompt_to_batch_major(a):
    w = a.shape[-1]
    return a[:N_PROMPT].reshape(P_GROUPS, SEQ, SUB, w).transpose(0, 2, 1, 3).reshape(BATCH, SEQ, w)


def kernel(x_prompt, x_sample, c, state_rglru, cache_k, cache_v, c_ctx, w_mod, b_mod, norm1, norm2, ev_w_in, a_conv_w, a_conv_b, a_w_r, a_b_r, a_w_i, a_b_i, a_lam, b_w_pool, b_scale, ev_w_out, od_w_in, c_conv_w, c_conv_b, c_ln_g, c_ln_b, q_norm, k_norm, od_w_out, ff_w1, ff_w3, ff_w2, moe_w_router, moe_b_router, moe_w1, moe_w3, moe_w2, norm_f):
    x = _to_time_major(x_prompt, x_sample)

    cond16 = jnp.concatenate([c_ctx[None, :], c, jnp.zeros((16 - 1 - DEC_BATCH, D), F32)], axis=0)
    mods = _ada_params(cond16, w_mod, b_mod)
    mods = jnp.stack([jnp.broadcast_to(mods[:, :, 0:1], (DEPTH, 6, SUB, D)), mods[:, :, 1:1 + SUB]], axis=2)

    cos_t, sin_t = _rope_tables()
    bdq = _head_mean_matrix(D_ATT)
    bdk = _head_mean_matrix(D_KV)
    cache_k4 = cache_k.reshape(DEC_BATCH, DEPTH // 2, PAST_LEN, D_KV)
    cache_v4 = cache_v.reshape(DEC_BATCH, DEPTH // 2, PAST_LEN, D_KV)

    new_states, new_k, new_v = [], [], []
    for layer in range(DEPTH):
        li = layer // 2
        shift1, scale1, gate1, shift2, scale2, gate2 = [mods[layer, j] for j in range(6)]
        g1 = norm1[layer].reshape(1, D)
        g2 = norm2[layer].reshape(1, D)
        if layer % 2 == 0:
            z = _even_in(x, g1, shift1, scale1, ev_w_in[li].astype(BF16))
            wg, bg = _block_diag_gates(a_w_r[li], a_w_i[li], a_b_r[li], a_b_i[li])
            lam = jnp.concatenate([a_lam[li, 0].reshape(4, 1, LANES), a_lam[li, 1].reshape(4, 1, LANES)], axis=2)
            cb = a_conv_b[li].reshape(1, D_A)
            h0_p = jnp.zeros((P_GROUPS, 2, SUB, D_A), F32)
            h0_s = state_rglru[:, li].transpose(1, 0, 2)[None]
            ya_p, h_last = _rglru_call(z, a_conv_w[li], cb, wg, bg, lam, h0_p,
                                       rows=R_PROMPT, groups=P_GROUPS, row_block0=0)
            ya_s, _ = _rglru_call(z, a_conv_w[li], cb, wg, bg, lam, h0_s,
                                  rows=R_SAMPLE, groups=1, row_block0=1)
            wp = b_w_pool[li].astype(BF16)
            sp = b_scale[li].reshape(1, D_B)
            yb_p = _pool_call(z, wp, sp, rows=R_PROMPT, groups=P_GROUPS, row_block0=0)
            yb_s = _pool_call(z, wp, sp, rows=R_SAMPLE, groups=1, row_block0=1)
            w_out = ev_w_out[li].astype(BF16)
            x = _even_out(x, (ya_p, ya_s), (yb_p, yb_s), w_out[:D_A], w_out[D_A:], gate1)
            x = _ffn(x, g2, shift2, scale2, gate2,
                     ff_w1[li].astype(BF16), ff_w3[li].astype(BF16), ff_w2[li].astype(BF16))
            new_states.append(h_last.transpose(0, 2, 1, 3).reshape(BATCH, 2, D_A))
        else:
            qg = jnp.tile(q_norm[li], N_Q_HEADS).reshape(1, D_ATT)
            kg = jnp.tile(k_norm[li], N_KV_HEADS).reshape(1, D_KV)
            glu, q, k, v = _odd_in(x, g1, shift1, scale1, od_w_in[li].astype(BF16), bdq, bdk, qg, kg, cos_t, sin_t)
            att = _attention(q, k, v, cache_k4[:, li], cache_v4[:, li])
            cw = c_conv_w[li]
            cb = c_conv_b[li].reshape(1, D_C)
            hc = (_conv_call(glu, cw, cb, rows=R_PROMPT, groups=P_GROUPS, row_block0=0),
                  _conv_call(glu, cw, cb, rows=R_SAMPLE, groups=1, row_block0=1))
            w_out = od_w_out[li].astype(BF16)
            wr = jnp.zeros((D, LANES), F32).at[:, :N_EXPERTS].set(moe_w_router[li])
            br = jnp.zeros((1, LANES), F32).at[0, :N_EXPERTS].set(moe_b_router[li])
            x1, h2, route = _odd_out(x, hc, att, c_ln_g[li].reshape(1, D_C), c_ln_b[li].reshape(1, D_C),
                                     w_out[:D_C], w_out[D_C:], gate1, g2, shift2, scale2, wr, br)
            pos, src, gate_sorted, tile_expert, n_used = _route_plan(route)
            xs = _gather_rows(h2, src)
            ys = _moe_experts(tile_expert, n_used, xs, gate_sorted,
                              moe_w1[li].astype(BF16), moe_w3[li].astype(BF16), moe_w2[li].astype(BF16))
            y2 = _gather_rows(ys, pos).reshape(N_TOK, 2 * D)
            x = _combine(x1, y2, gate2)
            new_k.append(_prompt_to_batch_major(k).reshape(BATCH, SEQ, N_KV_HEADS, HEAD_DIM))
            new_v.append(_prompt_to_batch_major(v).reshape(BATCH, SEQ, N_KV_HEADS, HEAD_DIM))

    y = _final_norm(x, norm_f.reshape(1, D))
    y_prompt = _prompt_to_batch_major(y)
    y_sample = y[N_PROMPT:].reshape(DEC_SEQ, DEC_BATCH, D).transpose(1, 0, 2)
    return (y_prompt, y_sample, jnp.stack(new_states, axis=1), jnp.stack(new_k, axis=1), jnp.stack(new_v, axis=1))
```

```python
import functools

import jax
import jax.numpy as jnp
from jax import lax
from jax.experimental import pallas as pl
from jax.experimental.pallas import tpu as pltpu

F32 = jnp.float32
BF16 = jnp.bfloat16

D = 1024
BATCH = 32
SEQ = 256
DEPTH = 4
DEC_BATCH = 8
DEC_SEQ = 1024
PAST_LEN = 256
GRID_W = 64
EPS = 1e-6
D_A = 512
A_BLOCKS = 8
A_BS = 64
A_CONV = 4
A_C = 8.0
D_B = 512
POOL_WINDOWS = (2, 4, 8, 16)
B_GS = 128
D_C = 512
C_CONV = 31
HEAD_DIM = 64
N_Q_HEADS = 8
N_KV_HEADS = 2
D_ATT = 512
D_KV = 128
ROPE_THETA = 10000.0
D_FF = 2816
N_EXPERTS = 8
D_FF_E = 1408
D_IN_EVEN = 1536
D_IN_ODD = 1792

SUB = 8
LANES = 128
N_PROMPT = BATCH * SEQ
N_SAMPLE = DEC_BATCH * DEC_SEQ
N_TOK = N_PROMPT + N_SAMPLE
P_GROUPS = BATCH // SUB
R_PROMPT = SEQ * SUB
R_SAMPLE = DEC_SEQ * SUB
TM = 512
TILES_PER_STREAM = N_PROMPT // TM
MOE_TM = 512
MOE_TILES = 2 * N_TOK // MOE_TM + N_EXPERTS
MOE_ROWS = MOE_TILES * MOE_TM
SLAB = D // LANES
DMA_UNROLL = 8
NEG_BIG = -3.0e38
VMEM_LIMIT = 56 * 1024 * 1024


def _params(n_axes, vmem=VMEM_LIMIT):
    return pltpu.CompilerParams(dimension_semantics=("arbitrary",) * n_axes, vmem_limit_bytes=vmem)


def _rms(x, g):
    ms = jnp.mean(x * x, axis=-1, keepdims=True)
    return x * lax.rsqrt(ms + EPS) * g


def _modulate(xn, shift, scale):
    tm = xn.shape[0]
    h = xn.reshape(tm // SUB, SUB, D) * (1.0 + scale)[None] + shift[None]
    return h.reshape(tm, D)


def _gated_add(x, gate, y):
    tm = x.shape[0]
    return x + (y.reshape(tm // SUB, SUB, D) * gate[None]).reshape(tm, D)


def _mod_spec():
    return pl.BlockSpec((None, SUB, D), lambda i, *_: (i // TILES_PER_STREAM, 0, 0))


def _row_spec(width, tm=TM):
    return pl.BlockSpec((tm, width), lambda i, *_: (i, 0))


def _full_spec(shape):
    nd = len(shape)
    return pl.BlockSpec(shape, lambda i, *_: (0,) * nd)


def _ada_kernel(c_ref, w_ref, b_ref, o_ref):
    c = c_ref[...]
    s = (c * jax.nn.sigmoid(c)).astype(BF16)
    o_ref[...] = jnp.dot(s, w_ref[...].astype(BF16), preferred_element_type=F32) + b_ref[...]


def _ada_params(cond16, w_mod, b_mod):
    return pl.pallas_call(
        _ada_kernel,
        grid=(DEPTH, 6),
        in_specs=[
            pl.BlockSpec((16, D), lambda l, j: (0, 0)),
            pl.BlockSpec((None, D, D), lambda l, j: (l, 0, j)),
            pl.BlockSpec((None, None, 1, D), lambda l, j: (l, j, 0, 0)),
        ],
        out_specs=pl.BlockSpec((None, None, 16, D), lambda l, j: (l, j, 0, 0)),
        out_shape=jax.ShapeDtypeStruct((DEPTH, 6, 16, D), F32),
        compiler_params=_params(2),
        name="ada_params",
    )(cond16, w_mod, b_mod.reshape(DEPTH, 6, 1, D))


def _even_in_kernel(x_ref, g_ref, sh_ref, sc_ref, w_ref, z_ref):
    h = _modulate(_rms(x_ref[...], g_ref[...]), sh_ref[...], sc_ref[...])
    z_ref[...] = jnp.dot(h.astype(BF16), w_ref[...], preferred_element_type=F32)


def _even_in(x, g, shift, scale, w_bf16):
    return pl.pallas_call(
        _even_in_kernel,
        grid=(N_TOK // TM,),
        in_specs=[_row_spec(D), _full_spec((1, D)), _mod_spec(), _mod_spec(), _full_spec((D, D_IN_EVEN))],
        out_specs=_row_spec(D_IN_EVEN),
        out_shape=jax.ShapeDtypeStruct((N_TOK, D_IN_EVEN), F32),
        compiler_params=_params(1),
        name="even_in",
    )(x, g, shift, scale, w_bf16)


RG_CHUNK = 256
RG_PAD = 16


def _rglru_kernel(u_ref, ga_ref, cw_ref, cb_ref, wg_ref, bg_ref, lam_ref, h0_ref,
                  y_ref, hl_ref, xpad, a0, b0, a1, b1, *, rows):
    steps = rows // SUB
    n_chunks = rows // RG_CHUNK

    xpad[0:RG_PAD, :] = jnp.zeros((RG_PAD, LANES), F32)
    xpad[RG_PAD + rows:RG_PAD + rows + RG_PAD, :] = jnp.zeros((RG_PAD, LANES), F32)

    def copy_chunk(c, carry):
        r = pl.multiple_of(c * RG_CHUNK, RG_CHUNK)
        xpad[pl.ds(RG_PAD + r, RG_CHUNK), :] = u_ref[pl.ds(r, RG_CHUNK), :]
        return carry

    lax.fori_loop(0, n_chunks, copy_chunk, 0)

    lam = lam_ref[...]
    softplus_neg = jnp.maximum(-lam, 0.0) + jnp.log1p(jnp.exp(-jnp.abs(lam)))
    a_refs = (a0, a1)
    b_refs = (b0, b1)

    def gate_chunk(c, carry):
        r = pl.multiple_of(c * RG_CHUNK, RG_CHUNK)
        xc = jnp.zeros((RG_CHUNK, LANES), F32) + cb_ref[...]
        for k in range(A_CONV):
            xc = xc + cw_ref[k:k + 1, :] * xpad[pl.ds(r + SUB * k, RG_CHUNK), :]
        pre = jnp.dot(xc.astype(BF16), wg_ref[...], preferred_element_type=F32) + bg_ref[...]
        for d in range(2):
            rg = jax.nn.sigmoid(pre[:, d * 256:d * 256 + LANES])
            ig = jax.nn.sigmoid(pre[:, d * 256 + LANES:(d + 1) * 256])
            log_a = (-A_C) * rg * softplus_neg[:, d * LANES:(d + 1) * LANES]
            a = jnp.exp(log_a)
            one_minus_a2 = -jnp.tanh(log_a) * (a * a + 1.0)
            a_refs[d][pl.ds(r, RG_CHUNK), :] = a
            b_refs[d][pl.ds(r, RG_CHUNK), :] = jnp.sqrt(one_minus_a2) * (ig * xc)
        return carry

    lax.fori_loop(0, n_chunks, gate_chunk, 0)

    def step(t, carry):
        hf, hb = carry
        rf = pl.multiple_of(t * SUB, SUB)
        rb = pl.multiple_of((steps - 1 - t) * SUB, SUB)
        hf = a0[pl.ds(rf, SUB), :] * hf + b0[pl.ds(rf, SUB), :]
        b0[pl.ds(rf, SUB), :] = hf
        hb = a1[pl.ds(rb, SUB), :] * hb + b1[pl.ds(rb, SUB), :]
        b1[pl.ds(rb, SUB), :] = hb
        return hf, hb

    hf, hb = lax.fori_loop(0, steps, step, (h0_ref[0], h0_ref[1]), unroll=8)
    hl_ref[0] = hf
    hl_ref[1] = hb

    def out_chunk(c, carry):
        r = pl.multiple_of(c * RG_CHUNK, RG_CHUNK)
        y = (b0[pl.ds(r, RG_CHUNK), :] + b1[pl.ds(r, RG_CHUNK), :]) * jax.nn.gelu(ga_ref[pl.ds(r, RG_CHUNK), :])
        y_ref[pl.ds(r, RG_CHUNK), :] = y.astype(BF16)
        return carry

    lax.fori_loop(0, n_chunks, out_chunk, 0)


def _rglru_call(z, cw, cb, wg, bg, lam, h0, *, rows, groups, row_block0):
    n_cb = D_A // LANES
    in_specs = [
        pl.BlockSpec((rows, LANES), lambda g, j: (row_block0 + g, j)),
        pl.BlockSpec((rows, LANES), lambda g, j: (row_block0 + g, n_cb + j)),
        pl.BlockSpec((A_CONV, LANES), lambda g, j: (0, j)),
        pl.BlockSpec((1, LANES), lambda g, j: (0, j)),
        pl.BlockSpec((None, LANES, 4 * LANES), lambda g, j: (j, 0, 0)),
        pl.BlockSpec((None, 1, 4 * LANES), lambda g, j: (j, 0, 0)),
        pl.BlockSpec((None, 1, 2 * LANES), lambda g, j: (j, 0, 0)),
        pl.BlockSpec((None, 2, SUB, LANES), lambda g, j: (g, 0, 0, j)),
    ]
    return pl.pallas_call(
        functools.partial(_rglru_kernel, rows=rows),
        grid=(groups, n_cb),
        in_specs=in_specs,
        out_specs=[
            pl.BlockSpec((rows, LANES), lambda g, j: (g, j)),
            pl.BlockSpec((None, 2, SUB, LANES), lambda g, j: (g, 0, 0, j)),
        ],
        out_shape=[
            jax.ShapeDtypeStruct((groups * rows, D_A), BF16),
            jax.ShapeDtypeStruct((groups, 2, SUB, D_A), F32),
        ],
        scratch_shapes=[pltpu.VMEM((rows + 2 * RG_PAD, LANES), F32)] + [pltpu.VMEM((rows, LANES), F32)] * 4,
        compiler_params=_params(2),
        name=f"rglru_{rows}",
    )(z, z, cw, cb, wg, bg, lam, h0)


POOL_CHUNK = 256
POOL_PAD = 64


def _pool_kernel(u_ref, w_ref, s_ref, y_ref, xpad, *, rows):
    steps = rows // SUB
    n_chunks = rows // POOL_CHUNK
    j = pl.program_id(1)

    xpad[0:POOL_PAD, :] = jnp.zeros((POOL_PAD, LANES), F32)
    xpad[POOL_PAD + rows:POOL_PAD + rows + POOL_PAD, :] = jnp.zeros((POOL_PAD, LANES), F32)

    def copy_chunk(c, carry):
        r = pl.multiple_of(c * POOL_CHUNK, POOL_CHUNK)
        xpad[pl.ds(POOL_PAD + r, POOL_CHUNK), :] = u_ref[pl.ds(r, POOL_CHUNK), :]
        return carry

    lax.fori_loop(0, n_chunks, copy_chunk, 0)

    for gi, win in enumerate(POOL_WINDOWS):
        half = win // 2

        @pl.when(j == gi)
        def _(half=half, win=win):
            def chunk(c, carry):
                r = pl.multiple_of(c * POOL_CHUNK, POOL_CHUNK)
                acc = xpad[pl.ds(POOL_PAD + r - SUB * half, POOL_CHUNK), :]
                for s in range(1, win):
                    acc = acc + xpad[pl.ds(POOL_PAD + r + SUB * (s - half), POOL_CHUNK), :]
                t = (r + lax.broadcasted_iota(jnp.int32, (POOL_CHUNK, 1), 0)) // SUB
                cnt = jnp.minimum(t + half, steps) - jnp.maximum(t - half, 0)
                dlt = acc / cnt.astype(F32) - xpad[pl.ds(POOL_PAD + r, POOL_CHUNK), :]
                y = jnp.dot(dlt.astype(BF16), w_ref[...], preferred_element_type=F32) * s_ref[...]
                y_ref[pl.ds(r, POOL_CHUNK), :] = y.astype(BF16)
                return carry

            lax.fori_loop(0, n_chunks, chunk, 0)


def _pool_call(z, w_pool_bf16, s_pool, *, rows, groups, row_block0):
    n_cb = D_B // LANES
    col0 = 2 * D_A // LANES
    in_specs = [
        pl.BlockSpec((rows, LANES), lambda g, j: (row_block0 + g, col0 + j)),
        pl.BlockSpec((None, B_GS, B_GS), lambda g, j: (j, 0, 0)),
        pl.BlockSpec((1, LANES), lambda g, j: (0, j)),
    ]
    return pl.pallas_call(
        functools.partial(_pool_kernel, rows=rows),
        grid=(groups, n_cb),
        in_specs=in_specs,
        out_specs=pl.BlockSpec((rows, LANES), lambda g, j: (g, j)),
        out_shape=jax.ShapeDtypeStruct((groups * rows, D_B), BF16),
        scratch_shapes=[pltpu.VMEM((rows + 2 * POOL_PAD, LANES), F32)],
        compiler_params=_params(2),
        name=f"pool_{rows}",
    )(z, w_pool_bf16, s_pool)


def _stream_specs(width):
    prompt = pl.BlockSpec((TM, width), lambda i, *_: (jnp.minimum(i, TILES_PER_STREAM - 1), 0))
    sample = pl.BlockSpec((TM, width), lambda i, *_: (jnp.maximum(i - TILES_PER_STREAM, 0), 0))
    return [prompt, sample]


def _pick_stream(prompt_ref, sample_ref):
    return jnp.where(pl.program_id(0) >= TILES_PER_STREAM, sample_ref[...], prompt_ref[...])


def _mix_out_kernel(x_ref, yap_ref, yas_ref, ybp_ref, ybs_ref, wa_ref, wb_ref, gate_ref, o_ref):
    y = jnp.dot(_pick_stream(yap_ref, yas_ref), wa_ref[...], preferred_element_type=F32)
    y = y + jnp.dot(_pick_stream(ybp_ref, ybs_ref), wb_ref[...], preferred_element_type=F32)
    o_ref[...] = _gated_add(x_ref[...], gate_ref[...], y)


def _even_out(x, ya, yb, wa, wb, gate):
    half = D // 2
    return pl.pallas_call(
        _mix_out_kernel,
        grid=(N_TOK // TM,),
        in_specs=[_row_spec(D)] + _stream_specs(half) + _stream_specs(half) +
                 [_full_spec((half, D)), _full_spec((half, D)), _mod_spec()],
        out_specs=_row_spec(D),
        out_shape=jax.ShapeDtypeStruct((N_TOK, D), F32),
        compiler_params=_params(1),
        name="even_out",
    )(x, *ya, *yb, wa, wb, gate)


def _ffn_kernel(x_ref, g_ref, sh_ref, sc_ref, gate_ref, w1_ref, w3_ref, w2_ref, o_ref, h_scr, acc_scr):
    j = pl.program_id(1)

    @pl.when(j == 0)
    def _():
        h = _modulate(_rms(x_ref[...], g_ref[...]), sh_ref[...], sc_ref[...])
        h_scr[...] = h.astype(BF16)

    h = h_scr[...]
    a = jnp.dot(h, w1_ref[...], preferred_element_type=F32)
    b = jnp.dot(h, w3_ref[...], preferred_element_type=F32)
    u = (a * jax.nn.sigmoid(a) * b).astype(BF16)
    y = jnp.dot(u, w2_ref[...], preferred_element_type=F32)

    @pl.when(j == 0)
    def _():
        acc_scr[...] = y

    @pl.when(j > 0)
    def _():
        acc_scr[...] += y

    @pl.when(j == pl.num_programs(1) - 1)
    def _():
        o_ref[...] = _gated_add(x_ref[...], gate_ref[...], acc_scr[...])


def _ffn(x, g, shift, scale, gate, w1, w3, w2):
    n_ff = D_FF // D_FF_E
    return pl.pallas_call(
        _ffn_kernel,
        grid=(N_TOK // TM, n_ff),
        in_specs=[
            pl.BlockSpec((TM, D), lambda i, j: (i, 0)),
            pl.BlockSpec((1, D), lambda i, j: (0, 0)),
            pl.BlockSpec((None, SUB, D), lambda i, j: (i // TILES_PER_STREAM, 0, 0)),
            pl.BlockSpec((None, SUB, D), lambda i, j: (i // TILES_PER_STREAM, 0, 0)),
            pl.BlockSpec((None, SUB, D), lambda i, j: (i // TILES_PER_STREAM, 0, 0)),
            pl.BlockSpec((D, D_FF_E), lambda i, j: (0, j)),
            pl.BlockSpec((D, D_FF_E), lambda i, j: (0, j)),
            pl.BlockSpec((D_FF_E, D), lambda i, j: (j, 0)),
        ],
        out_specs=pl.BlockSpec((TM, D), lambda i, j: (i, 0)),
        out_shape=jax.ShapeDtypeStruct((N_TOK, D), F32),
        scratch_shapes=[pltpu.VMEM((TM, D), BF16), pltpu.VMEM((TM, D), F32)],
        compiler_params=_params(2),
        name="ffn",
    )(x, g, shift, scale, gate, w1, w3, w2)


def _head_rms(x, ones_bd, g):
    sq = x * x
    hi = sq.astype(BF16)
    lo = (sq - hi.astype(F32)).astype(BF16)
    ms = jnp.dot(hi, ones_bd, preferred_element_type=F32) + jnp.dot(lo, ones_bd, preferred_element_type=F32)
    return x * lax.rsqrt(ms + EPS) * g


def _rope(x, cos, sin_signed):
    w = x.shape[-1]
    lane = lax.broadcasted_iota(jnp.int32, x.shape, 1)
    first = (lane % 32) < 16
    partner = jnp.where(first, pltpu.roll(x, w - 16, 1), pltpu.roll(x, 16, 1))
    return x * cos + partner * sin_signed


def _odd_in_kernel(x_ref, g_ref, sh_ref, sc_ref, w_ref, bdq_ref, bdk_ref, qg_ref, kg_ref, cos_ref, sin_ref,
                   glu_ref, q_ref, k_ref, v_ref):
    i = pl.program_id(0)
    h = _modulate(_rms(x_ref[...], g_ref[...]), sh_ref[...], sc_ref[...])
    z = jnp.dot(h.astype(BF16), w_ref[...], preferred_element_type=F32)
    glu_ref[...] = z[:, :D_C] * jax.nn.sigmoid(z[:, D_C:2 * D_C])
    o1 = 2 * D_C
    o2 = o1 + D_ATT
    o3 = o2 + D_KV
    q = _head_rms(z[:, o1:o2], bdq_ref[...], qg_ref[...])
    k = _head_rms(z[:, o2:o3], bdk_ref[...], kg_ref[...])
    cos = cos_ref[...]
    sin = sin_ref[...]
    is_sample = i >= TILES_PER_STREAM
    q_r = _rope(q, jnp.concatenate([cos] * 4, axis=1), jnp.concatenate([sin] * 4, axis=1))
    k_r = _rope(k, cos, sin)
    q_ref[...] = jnp.where(is_sample, q_r, q)
    k_ref[...] = jnp.where(is_sample, k_r, k)
    v_ref[...] = z[:, o3:]


def _odd_in(x, g, shift, scale, w_bf16, bdq, bdk, qg, kg, cos_t, sin_t):
    rope_spec = pl.BlockSpec((TM, D_KV), lambda i: (jnp.maximum(i - TILES_PER_STREAM, 0), 0))
    return pl.pallas_call(
        _odd_in_kernel,
        grid=(N_TOK // TM,),
        in_specs=[_row_spec(D), _full_spec((1, D)), _mod_spec(), _mod_spec(), _full_spec((D, D_IN_ODD)),
                  _full_spec((D_ATT, D_ATT)), _full_spec((D_KV, D_KV)), _full_spec((1, D_ATT)),
                  _full_spec((1, D_KV)), rope_spec, rope_spec],
        out_specs=[_row_spec(D_C), _row_spec(D_ATT), _row_spec(D_KV), _row_spec(D_KV)],
        out_shape=[jax.ShapeDtypeStruct((N_TOK, D_C), F32), jax.ShapeDtypeStruct((N_TOK, D_ATT), F32),
                   jax.ShapeDtypeStruct((N_TOK, D_KV), F32), jax.ShapeDtypeStruct((N_TOK, D_KV), F32)],
        compiler_params=_params(1),
        name="odd_in",
    )(x, g, shift, scale, w_bf16, bdq, bdk, qg, kg, cos_t, sin_t)


def _attend(q, k_all, v_all):
    scale = HEAD_DIM ** -0.5
    group = N_Q_HEADS // N_KV_HEADS
    heads = []
    for h in range(N_KV_HEADS):
        kh = k_all[:, h * HEAD_DIM:(h + 1) * HEAD_DIM].astype(BF16)
        vh = v_all[:, h * HEAD_DIM:(h + 1) * HEAD_DIM].astype(BF16)
        for gq in range(group):
            hd = h * group + gq
            qh = (q[:, hd * HEAD_DIM:(hd + 1) * HEAD_DIM] * scale).astype(BF16)
            s = lax.dot_general(qh, kh, (((1,), (1,)), ((), ())), preferred_element_type=F32)
            m = jnp.max(s, axis=-1, keepdims=True)
            p = jnp.exp(s - m)
            l = jnp.sum(p, axis=-1, keepdims=True)
            heads.append(jnp.dot(p.astype(BF16), vh, preferred_element_type=F32) / l)
    return jnp.concatenate(heads, axis=1)


def _attn_prompt_kernel(q_ref, k_ref, v_ref, o_ref):
    b = pl.program_id(1)
    o_ref[:, b, :] = _attend(q_ref[:, b, :], k_ref[:, b, :], v_ref[:, b, :])


def _attn_sample_kernel(q_ref, k_ref, v_ref, ck_ref, cv_ref, o_ref):
    b = pl.program_id(1)
    k_all = jnp.concatenate([ck_ref[...], k_ref[:, b, :]], axis=0)
    v_all = jnp.concatenate([cv_ref[...], v_ref[:, b, :]], axis=0)
    o_ref[:, b, :] = _attend(q_ref[:, b, :], k_all, v_all)


def _attention(q, k, v, cache_k_l, cache_v_l):
    q3 = q.reshape(N_TOK // SUB, SUB, D_ATT)
    k3 = k.reshape(N_TOK // SUB, SUB, D_KV)
    v3 = v.reshape(N_TOK // SUB, SUB, D_KV)
    out_shape = jax.ShapeDtypeStruct((P_GROUPS * SEQ, SUB, D_ATT), F32)
    att_p = pl.pallas_call(
        _attn_prompt_kernel,
        grid=(P_GROUPS, SUB),
        in_specs=[
            pl.BlockSpec((SEQ, SUB, D_ATT), lambda g, b: (g, 0, 0)),
            pl.BlockSpec((SEQ, SUB, D_KV), lambda g, b: (g, 0, 0)),
            pl.BlockSpec((SEQ, SUB, D_KV), lambda g, b: (g, 0, 0)),
        ],
        out_specs=pl.BlockSpec((SEQ, SUB, D_ATT), lambda g, b: (g, 0, 0)),
        out_shape=out_shape,
        compiler_params=_params(2),
        name="attn_prompt",
    )(q3, k3, v3)
    q_chunks = DEC_SEQ // SEQ
    att_s = pl.pallas_call(
        _attn_sample_kernel,
        grid=(q_chunks, DEC_BATCH),
        in_specs=[
            pl.BlockSpec((SEQ, SUB, D_ATT), lambda c, b: (P_GROUPS + c, 0, 0)),
            pl.BlockSpec((DEC_SEQ, SUB, D_KV), lambda c, b: (1, 0, 0)),
            pl.BlockSpec((DEC_SEQ, SUB, D_KV), lambda c, b: (1, 0, 0)),
            pl.BlockSpec((None, PAST_LEN, D_KV), lambda c, b: (b, 0, 0)),
            pl.BlockSpec((None, PAST_LEN, D_KV), lambda c, b: (b, 0, 0)),
        ],
        out_specs=pl.BlockSpec((SEQ, SUB, D_ATT), lambda c, b: (c, 0, 0)),
        out_shape=out_shape,
        compiler_params=_params(2),
        name="attn_sample",
    )(q3, k3, v3, cache_k_l, cache_v_l)
    return att_p.reshape(N_PROMPT, D_ATT), att_s.reshape(N_SAMPLE, D_ATT)


CV_CHUNK = 64
CV_PAD = (C_CONV // 2) * SUB


def _conv_kernel(u_ref, w_ref, b_ref, y_ref, xpad, *, rows):
    n_copy = rows // 256
    xpad[0:CV_PAD, :] = jnp.zeros((CV_PAD, LANES), F32)
    xpad[CV_PAD + rows:CV_PAD + rows + CV_PAD, :] = jnp.zeros((CV_PAD, LANES), F32)

    def copy_chunk(c, carry):
        r = pl.multiple_of(c * 256, 256)
        xpad[pl.ds(CV_PAD + r, 256), :] = u_ref[pl.ds(r, 256), :]
        return carry

    lax.fori_loop(0, n_copy, copy_chunk, 0)

    def chunk(c, carry):
        r = pl.multiple_of(c * CV_CHUNK, CV_CHUNK)
        acc = jnp.zeros((CV_CHUNK, LANES), F32) + b_ref[...]
        for k in range(C_CONV):
            acc = acc + w_ref[k:k + 1, :] * xpad[pl.ds(r + SUB * k, CV_CHUNK), :]
        y_ref[pl.ds(r, CV_CHUNK), :] = acc
        return carry

    lax.fori_loop(0, rows // CV_CHUNK, chunk, 0)


def _conv_call(glu, w, b, *, rows, groups, row_block0):
    n_cb = D_C // LANES
    in_specs = [
        pl.BlockSpec((rows, LANES), lambda g, j: (row_block0 + g, j)),
        pl.BlockSpec((C_CONV, LANES), lambda g, j: (0, j)),
        pl.BlockSpec((1, LANES), lambda g, j: (0, j)),
    ]
    return pl.pallas_call(
        functools.partial(_conv_kernel, rows=rows),
        grid=(groups, n_cb),
        in_specs=in_specs,
        out_specs=pl.BlockSpec((rows, LANES), lambda g, j: (g, j)),
        out_shape=jax.ShapeDtypeStruct((groups * rows, D_C), F32),
        scratch_shapes=[pltpu.VMEM((rows + 2 * CV_PAD, LANES), F32)],
        compiler_params=_params(2),
        name=f"conv_{rows}",
    )(glu, w, b)


def _odd_out_kernel(x_ref, hcp_ref, hcs_ref, attp_ref, atts_ref, lng_ref, lnb_ref, wc_ref, wa_ref, gate_ref,
                    g2_ref, sh_ref, sc_ref, wr_ref, br_ref, x1_ref, h2_ref, route_ref):
    hc = _pick_stream(hcp_ref, hcs_ref)
    mu = jnp.mean(hc, axis=-1, keepdims=True)
    xc = hc - mu
    var = jnp.mean(xc * xc, axis=-1, keepdims=True)
    ln = xc * lax.rsqrt(var + EPS) * lng_ref[...] + lnb_ref[...]
    yc = (ln * jax.nn.sigmoid(ln)).astype(BF16)
    y = jnp.dot(yc, wc_ref[...], preferred_element_type=F32)
    y = y + jnp.dot(_pick_stream(attp_ref, atts_ref).astype(BF16), wa_ref[...], preferred_element_type=F32)
    x1 = _gated_add(x_ref[...], gate_ref[...], y)
    x1_ref[...] = x1
    h2 = _modulate(_rms(x1, g2_ref[...]), sh_ref[...], sc_ref[...])
    for j in range(SLAB):
        h2_ref[pl.ds(j, TM, stride=SLAB), :] = h2[:, j * LANES:(j + 1) * LANES]
    logits = jnp.dot(h2, wr_ref[...], preferred_element_type=F32, precision=lax.Precision.HIGHEST) + br_ref[...]
    lane = lax.broadcasted_iota(jnp.int32, logits.shape, 1).astype(F32)
    lg = jnp.where(lane < N_EXPERTS, logits, NEG_BIG)
    m1 = jnp.max(lg, axis=-1, keepdims=True)
    i1 = jnp.min(jnp.where(lg == m1, lane, float(LANES)), axis=-1, keepdims=True)
    lg2 = jnp.where(lane == i1, NEG_BIG, lg)
    m2 = jnp.max(lg2, axis=-1, keepdims=True)
    i2 = jnp.min(jnp.where(lg2 == m2, lane, float(LANES)), axis=-1, keepdims=True)
    e = jnp.exp(m2 - m1)
    den = 1.0 + e
    route = jnp.where(lane == 0.0, i1, jnp.where(lane == 1.0, i2,
                      jnp.where(lane == 2.0, 1.0 / den, jnp.where(lane == 3.0, e / den, 0.0))))
    route_ref[...] = route


def _odd_out(x, hc, att, lng, lnb, wc, wa, gate, g2, shift2, scale2, wr_pad, br_pad):
    half = D // 2
    return pl.pallas_call(
        _odd_out_kernel,
        grid=(N_TOK // TM,),
        in_specs=[_row_spec(D)] + _stream_specs(half) + _stream_specs(half) +
                 [_full_spec((1, half)), _full_spec((1, half)),
                  _full_spec((half, D)), _full_spec((half, D)), _mod_spec(),
                  _full_spec((1, D)), _mod_spec(), _mod_spec(), _full_spec((D, LANES)), _full_spec((1, LANES))],
        out_specs=[_row_spec(D), pl.BlockSpec((TM * SLAB, LANES), lambda i: (i, 0)), _row_spec(LANES)],
        out_shape=[jax.ShapeDtypeStruct((N_TOK, D), F32), jax.ShapeDtypeStruct((N_TOK * SLAB, LANES), F32),
                   jax.ShapeDtypeStruct((N_TOK, LANES), F32)],
        compiler_params=_params(1),
        name="odd_out",
    )(x, *hc, *att, lng, lnb, wc, wa, gate, g2, shift2, scale2, wr_pad, br_pad)


def _moe_kernel(te_ref, nv_ref, nu_ref, jcur_ref, jnext_ref, h2_hbm, w1_ref, w3_ref, w2_ref, y2_hbm,
                xbuf, obuf, gsem, ssem):
    del te_ref
    i = pl.program_id(0)
    n_used = nu_ref[0]
    slot = i % 2

    def rows_of(r):
        return pl.ds(pl.multiple_of(r * SLAB, SLAB), SLAB)

    def start_gather(j_ref, s):
        def one(r):
            tok = lax.shift_right_logical(j_ref[0, r], 1)
            pltpu.make_async_copy(h2_hbm.at[rows_of(tok), :], xbuf.at[s, rows_of(r), :], gsem.at[s]).start()

        def body(c, carry):
            for k in range(DMA_UNROLL):
                one(c * DMA_UNROLL + k)
            return carry

        lax.fori_loop(0, MOE_TM // DMA_UNROLL, body, 0)

    def wait_gather(s):
        pltpu.make_async_copy(h2_hbm.at[pl.ds(0, MOE_TM * SLAB), :], xbuf.at[s], gsem.at[s]).wait()

    def start_scatter(s, n):
        def one(r):
            pltpu.make_async_copy(obuf.at[s, rows_of(r), :], y2_hbm.at[rows_of(jcur_ref[0, r]), :],
                                  ssem.at[s]).start()

        def body(c, carry):
            for k in range(DMA_UNROLL):
                one(c * DMA_UNROLL + k)
            return carry

        full = lax.shift_right_logical(n, DMA_UNROLL.bit_length() - 1)
        lax.fori_loop(0, full, body, 0)

        def tail(r, carry):
            one(r)
            return carry

        lax.fori_loop(full * DMA_UNROLL, n, tail, 0)

    def wait_scatter(s, n):
        @pl.when(n > 0)
        def _():
            pltpu.make_async_copy(obuf.at[s, pl.ds(0, n * SLAB), :], y2_hbm.at[pl.ds(0, n * SLAB), :],
                                  ssem.at[s]).wait()

    @pl.when(i == 0)
    def _():
        start_gather(jcur_ref, 0)

    @pl.when(i < n_used)
    def _():
        wait_gather(slot)

        @pl.when(i + 1 < n_used)
        def _():
            start_gather(jnext_ref, 1 - slot)

        @pl.when(i >= 2)
        def _():
            wait_scatter(slot, nv_ref[jnp.maximum(i - 2, 0)])

        x = jnp.concatenate([xbuf[slot, pl.ds(j, MOE_TM, stride=SLAB), :] for j in range(SLAB)], axis=1)
        h = x.astype(BF16)
        a = jnp.dot(h, w1_ref[...], preferred_element_type=F32)
        b = jnp.dot(h, w3_ref[...], preferred_element_type=F32)
        u = (a * jax.nn.sigmoid(a) * b).astype(BF16)
        y = jnp.dot(u, w2_ref[...], preferred_element_type=F32)
        for j in range(SLAB):
            obuf[slot, pl.ds(j, MOE_TM, stride=SLAB), :] = y[:, j * LANES:(j + 1) * LANES]
        start_scatter(slot, nv_ref[i])

        @pl.when(i == n_used - 1)
        def _():
            @pl.when(i >= 1)
            def _():
                wait_scatter(1 - slot, nv_ref[jnp.maximum(i - 1, 0)])

            wait_scatter(slot, nv_ref[i])


def _moe_experts(tile_expert, tile_valid, n_used, jtab, h2, w1, w3, w2):
    jtab3 = jtab.reshape(MOE_TILES, 1, MOE_TM)
    grid_spec = pltpu.PrefetchScalarGridSpec(
        num_scalar_prefetch=3,
        grid=(MOE_TILES,),
        in_specs=[
            pl.BlockSpec((None, 1, MOE_TM), lambda i, *_: (i, 0, 0), memory_space=pltpu.SMEM),
            pl.BlockSpec((None, 1, MOE_TM), lambda i, *_: (jnp.minimum(i + 1, MOE_TILES - 1), 0, 0),
                         memory_space=pltpu.SMEM),
            pl.BlockSpec(memory_space=pl.ANY),
            pl.BlockSpec((None, D, D_FF_E), lambda i, te, *_: (te[i], 0, 0)),
            pl.BlockSpec((None, D, D_FF_E), lambda i, te, *_: (te[i], 0, 0)),
            pl.BlockSpec((None, D_FF_E, D), lambda i, te, *_: (te[i], 0, 0)),
        ],
        out_specs=pl.BlockSpec(memory_space=pl.ANY),
        scratch_shapes=[
            pltpu.VMEM((2, MOE_TM * SLAB, LANES), F32),
            pltpu.VMEM((2, MOE_TM * SLAB, LANES), F32),
            pltpu.SemaphoreType.DMA((2,)),
            pltpu.SemaphoreType.DMA((2,)),
        ],
    )
    y2 = pl.pallas_call(
        _moe_kernel,
        grid_spec=grid_spec,
        out_shape=jax.ShapeDtypeStruct((2 * N_TOK * SLAB, LANES), F32),
        compiler_params=_params(1),
        name="moe_experts",
    )(tile_expert, tile_valid, n_used, jtab3, jtab3, h2, w1, w3, w2)
    return y2


def _route_plan(route):
    ids = route[:, 0:2].astype(jnp.int32).reshape(-1)
    onehot = (ids[:, None] == jnp.arange(N_EXPERTS, dtype=jnp.int32)[None, :]).astype(jnp.int32)
    csum = jnp.cumsum(onehot, axis=0)
    rank = jnp.sum(csum * onehot, axis=1) - 1
    counts = csum[-1]
    tiles = (counts + MOE_TM - 1) // MOE_TM
    tile_end = jnp.cumsum(tiles)
    group_start = (tile_end - tiles) * MOE_TM
    pos = jnp.sum(onehot * group_start[None, :], axis=1) + rank
    jtab = jnp.zeros((MOE_ROWS,), jnp.int32).at[pos].set(jnp.arange(2 * N_TOK, dtype=jnp.int32))
    tile_ids = jnp.arange(MOE_TILES, dtype=jnp.int32)
    tile_expert = jnp.minimum(jnp.sum((tile_ids[:, None] >= tile_end[None, :]).astype(jnp.int32), axis=1),
                              N_EXPERTS - 1)
    rows_before = (tile_ids - (tile_end - tiles)[tile_expert]) * MOE_TM
    tile_valid = jnp.clip(counts[tile_expert] - rows_before, 0, MOE_TM)
    tile_valid = jnp.where(tile_ids < tile_end[-1], tile_valid, 0).astype(jnp.int32)
    n_used = tile_end[-1:].astype(jnp.int32)
    return jtab, tile_expert, tile_valid, n_used


def _combine_kernel(x_ref, y_ref, route_ref, gate_ref, o_ref):
    g0 = route_ref[:, 2:3]
    g1 = route_ref[:, 3:4]
    parts = [g0 * y_ref[pl.ds(j, TM, stride=2 * SLAB), :] + g1 * y_ref[pl.ds(SLAB + j, TM, stride=2 * SLAB), :]
             for j in range(SLAB)]
    o_ref[...] = _gated_add(x_ref[...], gate_ref[...], jnp.concatenate(parts, axis=1))


def _combine(x, y2, route, gate):
    return pl.pallas_call(
        _combine_kernel,
        grid=(N_TOK // TM,),
        in_specs=[_row_spec(D), pl.BlockSpec((TM * 2 * SLAB, LANES), lambda i: (i, 0)),
                  _row_spec(LANES), _mod_spec()],
        out_specs=_row_spec(D),
        out_shape=jax.ShapeDtypeStruct((N_TOK, D), F32),
        compiler_params=_params(1),
        name="moe_combine",
    )(x, y2, route, gate)


def _final_kernel(x_ref, g_ref, o_ref):
    o_ref[...] = _rms(x_ref[...], g_ref[...])


def _final_norm(x, g):
    return pl.pallas_call(
        _final_kernel,
        grid=(N_TOK // TM,),
        in_specs=[_row_spec(D), _full_spec((1, D))],
        out_specs=_row_spec(D),
        out_shape=jax.ShapeDtypeStruct((N_TOK, D), F32),
        compiler_params=_params(1),
        name="final_norm",
    )(x, g)


def _block_diag_gates(w_r, w_i, b_r, b_i):
    def bd(w):
        w4 = w.reshape(4, 2, A_BS, A_BS)
        z = jnp.zeros((4, A_BS, A_BS), w.dtype)
        top = jnp.concatenate([w4[:, 0], z], axis=2)
        bot = jnp.concatenate([z, w4[:, 1]], axis=2)
        return jnp.concatenate([top, bot], axis=1)

    wg = jnp.concatenate([bd(w_r[0]), bd(w_i[0]), bd(w_r[1]), bd(w_i[1])], axis=2)
    bg = jnp.concatenate([b_r[0].reshape(4, 1, LANES), b_i[0].reshape(4, 1, LANES),
                          b_r[1].reshape(4, 1, LANES), b_i[1].reshape(4, 1, LANES)], axis=2)
    return wg.astype(BF16), bg


def _head_mean_matrix(width):
    idx = jnp.arange(width) // HEAD_DIM
    return ((idx[:, None] == idx[None, :]).astype(F32) / HEAD_DIM).astype(BF16)


def _rope_tables():
    pos = jnp.arange(DEC_SEQ)
    row = (pos // GRID_W).astype(F32)
    col = (pos % GRID_W).astype(F32)
    n_freq = HEAD_DIM // 4
    inv = ROPE_THETA ** (-jnp.arange(n_freq, dtype=F32) / n_freq)
    ang = jnp.stack([row[:, None] * inv, col[:, None] * inv], axis=1)
    cos = jnp.cos(ang)
    sin = jnp.sin(ang)
    cos_h = jnp.stack([cos, cos], axis=2).reshape(DEC_SEQ, HEAD_DIM)
    sin_h = jnp.stack([-sin, sin], axis=2).reshape(DEC_SEQ, HEAD_DIM)
    cos_t = jnp.tile(cos_h, (1, 2))
    sin_t = jnp.tile(sin_h, (1, 2))
    cos_t = jnp.broadcast_to(cos_t[:, None, :], (DEC_SEQ, SUB, D_KV)).reshape(N_SAMPLE, D_KV)
    sin_t = jnp.broadcast_to(sin_t[:, None, :], (DEC_SEQ, SUB, D_KV)).reshape(N_SAMPLE, D_KV)
    return cos_t, sin_t


def _to_time_major(x_prompt, x_sample):
    xp = x_prompt.reshape(P_GROUPS, SUB, SEQ, D).transpose(0, 2, 1, 3).reshape(N_PROMPT, D)
    xs = x_sample.transpose(1, 0, 2).reshape(N_SAMPLE, D)
    return jnp.concatenate([xp, xs], axis=0)


def _prompt_to_batch_major(a):
    w = a.shape[-1]
    return a[:N_PROMPT].reshape(P_GROUPS, SEQ, SUB, w).transpose(0, 2, 1, 3).reshape(BATCH, SEQ, w)


def kernel(x_prompt, x_sample, c, state_rglru, cache_k, cache_v, c_ctx, w_mod, b_mod, norm1, norm2, ev_w_in, a_conv_w, a_conv_b, a_w_r, a_b_r, a_w_i, a_b_i, a_lam, b_w_pool, b_scale, ev_w_out, od_w_in, c_conv_w, c_conv_b, c_ln_g, c_ln_b, q_norm, k_norm, od_w_out, ff_w1, ff_w3, ff_w2, moe_w_router, moe_b_router, moe_w1, moe_w3, moe_w2, norm_f):
    x = _to_time_major(x_prompt, x_sample)

    cond16 = jnp.concatenate([c_ctx[None, :], c, jnp.zeros((16 - 1 - DEC_BATCH, D), F32)], axis=0)
    mods = _ada_params(cond16, w_mod, b_mod)
    mods = jnp.stack([jnp.broadcast_to(mods[:, :, 0:1], (DEPTH, 6, SUB, D)), mods[:, :, 1:1 + SUB]], axis=2)

    cos_t, sin_t = _rope_tables()
    bdq = _head_mean_matrix(D_ATT)
    bdk = _head_mean_matrix(D_KV)
    cache_k4 = cache_k.reshape(DEC_BATCH, DEPTH // 2, PAST_LEN, D_KV)
    cache_v4 = cache_v.reshape(DEC_BATCH, DEPTH // 2, PAST_LEN, D_KV)

    new_states, new_k, new_v = [], [], []
    for layer in range(DEPTH):
        li = layer // 2
        shift1, scale1, gate1, shift2, scale2, gate2 = [mods[layer, j] for j in range(6)]
        g1 = norm1[layer].reshape(1, D)
        g2 = norm2[layer].reshape(1, D)
        if layer % 2 == 0:
            z = _even_in(x, g1, shift1, scale1, ev_w_in[li].astype(BF16))
            wg, bg = _block_diag_gates(a_w_r[li], a_w_i[li], a_b_r[li], a_b_i[li])
            lam = jnp.concatenate([a_lam[li, 0].reshape(4, 1, LANES), a_lam[li, 1].reshape(4, 1, LANES)], axis=2)
            cb = a_conv_b[li].reshape(1, D_A)
            h0_p = jnp.zeros((P_GROUPS, 2, SUB, D_A), F32)
            h0_s = state_rglru[:, li].transpose(1, 0, 2)[None]
            ya_p, h_last = _rglru_call(z, a_conv_w[li], cb, wg, bg, lam, h0_p,
                                       rows=R_PROMPT, groups=P_GROUPS, row_block0=0)
            ya_s, _ = _rglru_call(z, a_conv_w[li], cb, wg, bg, lam, h0_s,
                                  rows=R_SAMPLE, groups=1, row_block0=1)
            wp = b_w_pool[li].astype(BF16)
            sp = b_scale[li].reshape(1, D_B)
            yb_p = _pool_call(z, wp, sp, rows=R_PROMPT, groups=P_GROUPS, row_block0=0)
            yb_s = _pool_call(z, wp, sp, rows=R_SAMPLE, groups=1, row_block0=1)
            w_out = ev_w_out[li].astype(BF16)
            x = _even_out(x, (ya_p, ya_s), (yb_p, yb_s), w_out[:D_A], w_out[D_A:], gate1)
            x = _ffn(x, g2, shift2, scale2, gate2,
                     ff_w1[li].astype(BF16), ff_w3[li].astype(BF16), ff_w2[li].astype(BF16))
            new_states.append(h_last.transpose(0, 2, 1, 3).reshape(BATCH, 2, D_A))
        else:
            qg = jnp.tile(q_norm[li], N_Q_HEADS).reshape(1, D_ATT)
            kg = jnp.tile(k_norm[li], N_KV_HEADS).reshape(1, D_KV)
            glu, q, k, v = _odd_in(x, g1, shift1, scale1, od_w_in[li].astype(BF16), bdq, bdk, qg, kg, cos_t, sin_t)
            att = _attention(q, k, v, cache_k4[:, li], cache_v4[:, li])
            cw = c_conv_w[li]
            cb = c_conv_b[li].reshape(1, D_C)
            hc = (_conv_call(glu, cw, cb, rows=R_PROMPT, groups=P_GROUPS, row_block0=0),
                  _conv_call(glu, cw, cb, rows=R_SAMPLE, groups=1, row_block0=1))
            w_out = od_w_out[li].astype(BF16)
            wr = jnp.zeros((D, LANES), F32).at[:, :N_EXPERTS].set(moe_w_router[li])
            br = jnp.zeros((1, LANES), F32).at[0, :N_EXPERTS].set(moe_b_router[li])
            x1, h2, route = _odd_out(x, hc, att, c_ln_g[li].reshape(1, D_C), c_ln_b[li].reshape(1, D_C),
                                     w_out[:D_C], w_out[D_C:], gate1, g2, shift2, scale2, wr, br)
            jtab, tile_expert, tile_valid, n_used = _route_plan(route)
            y2 = _moe_experts(tile_expert, tile_valid, n_used, jtab, h2,
                              moe_w1[li].astype(BF16), moe_w3[li].astype(BF16), moe_w2[li].astype(BF16))
            x = _combine(x1, y2, route, gate2)
            new_k.append(_prompt_to_batch_major(k).reshape(BATCH, SEQ, N_KV_HEADS, HEAD_DIM))
            new_v.append(_prompt_to_batch_major(v).reshape(BATCH, SEQ, N_KV_HEADS, HEAD_DIM))

    y = _final_norm(x, norm_f.reshape(1, D))
    y_prompt = _prompt_to_batch_major(y)
    y_sample = y[N_PROMPT:].reshape(DEC_SEQ, DEC_BATCH, D).transpose(1, 0, 2)
    return (y_prompt, y_sample, jnp.stack(new_states, axis=1), jnp.stack(new_k, axis=1), jnp.stack(new_v, axis=1))
```

```python
import functools

import jax
import jax.numpy as jnp
from jax import lax
from jax.experimental import pallas as pl
from jax.experimental.pallas import tpu as pltpu

F32 = jnp.float32
BF16 = jnp.bfloat16

D = 1024
BATCH = 32
SEQ = 256
DEPTH = 4
DEC_BATCH = 8
DEC_SEQ = 1024
PAST_LEN = 256
GRID_W = 64
EPS = 1e-6
D_A = 512
A_BLOCKS = 8
A_BS = 64
A_CONV = 4
A_C = 8.0
D_B = 512
POOL_WINDOWS = (2, 4, 8, 16)
B_GS = 128
D_C = 512
C_CONV = 31
HEAD_DIM = 64
N_Q_HEADS = 8
N_KV_HEADS = 2
D_ATT = 512
D_KV = 128
ROPE_THETA = 10000.0
D_FF = 2816
N_EXPERTS = 8
D_FF_E = 1408
D_IN_EVEN = 1536
D_IN_ODD = 1792

SUB = 8
LANES = 128
N_PROMPT = BATCH * SEQ
N_SAMPLE = DEC_BATCH * DEC_SEQ
N_TOK = N_PROMPT + N_SAMPLE
P_GROUPS = BATCH // SUB
R_PROMPT = SEQ * SUB
R_SAMPLE = DEC_SEQ * SUB
TM = 512
TILES_PER_STREAM = N_PROMPT // TM
MOE_TM = 512
MOE_TILES = 2 * N_TOK // MOE_TM + N_EXPERTS
MOE_ROWS = MOE_TILES * MOE_TM
SLAB = D // LANES
DMA_UNROLL = 8
NEG_BIG = -3.0e38
VMEM_LIMIT = 56 * 1024 * 1024


def _params(n_axes, vmem=VMEM_LIMIT):
    return pltpu.CompilerParams(dimension_semantics=("arbitrary",) * n_axes, vmem_limit_bytes=vmem)


def _rms(x, g):
    ms = jnp.mean(x * x, axis=-1, keepdims=True)
    return x * lax.rsqrt(ms + EPS) * g


def _modulate(xn, shift, scale):
    tm = xn.shape[0]
    h = xn.reshape(tm // SUB, SUB, D) * (1.0 + scale)[None] + shift[None]
    return h.reshape(tm, D)


def _gated_add(x, gate, y):
    tm = x.shape[0]
    return x + (y.reshape(tm // SUB, SUB, D) * gate[None]).reshape(tm, D)


def _mod_spec():
    return pl.BlockSpec((None, SUB, D), lambda i, *_: (i // TILES_PER_STREAM, 0, 0))


def _row_spec(width, tm=TM):
    return pl.BlockSpec((tm, width), lambda i, *_: (i, 0))


def _full_spec(shape):
    nd = len(shape)
    return pl.BlockSpec(shape, lambda i, *_: (0,) * nd)


def _ada_kernel(c_ref, w_ref, b_ref, o_ref):
    c = c_ref[...]
    s = (c * jax.nn.sigmoid(c)).astype(BF16)
    o_ref[...] = jnp.dot(s, w_ref[...].astype(BF16), preferred_element_type=F32) + b_ref[...]


def _ada_params(cond16, w_mod, b_mod):
    return pl.pallas_call(
        _ada_kernel,
        grid=(DEPTH, 6),
        in_specs=[
            pl.BlockSpec((16, D), lambda l, j: (0, 0)),
            pl.BlockSpec((None, D, D), lambda l, j: (l, 0, j)),
            pl.BlockSpec((None, None, 1, D), lambda l, j: (l, j, 0, 0)),
        ],
        out_specs=pl.BlockSpec((None, None, 16, D), lambda l, j: (l, j, 0, 0)),
        out_shape=jax.ShapeDtypeStruct((DEPTH, 6, 16, D), F32),
        compiler_params=_params(2),
        name="ada_params",
    )(cond16, w_mod, b_mod.reshape(DEPTH, 6, 1, D))


def _even_in_kernel(x_ref, g_ref, sh_ref, sc_ref, w_ref, z_ref):
    h = _modulate(_rms(x_ref[...], g_ref[...]), sh_ref[...], sc_ref[...])
    z_ref[...] = jnp.dot(h.astype(BF16), w_ref[...], preferred_element_type=F32)


def _even_in(x, g, shift, scale, w_bf16):
    return pl.pallas_call(
        _even_in_kernel,
        grid=(N_TOK // TM,),
        in_specs=[_row_spec(D), _full_spec((1, D)), _mod_spec(), _mod_spec(), _full_spec((D, D_IN_EVEN))],
        out_specs=_row_spec(D_IN_EVEN),
        out_shape=jax.ShapeDtypeStruct((N_TOK, D_IN_EVEN), F32),
        compiler_params=_params(1),
        name="even_in",
    )(x, g, shift, scale, w_bf16)


T_TILE = TM // SUB


def _first_in_kernel(xp_ref, xs_ref, g_ref, sh_ref, sc_ref, w_ref, z_ref, x_ref, cols):
    is_sample = pl.program_id(0) >= TILES_PER_STREAM
    for b in range(SUB):
        xb = jnp.where(is_sample, xs_ref[b], xp_ref[b])
        for c in range(SLAB):
            cols[c, pl.ds(b, T_TILE, stride=SUB), :] = xb[:, c * LANES:(c + 1) * LANES]
    x = jnp.concatenate([cols[c] for c in range(SLAB)], axis=1)
    x_ref[...] = x
    h = _modulate(_rms(x, g_ref[...]), sh_ref[...], sc_ref[...])
    z_ref[...] = jnp.dot(h.astype(BF16), w_ref[...], preferred_element_type=F32)


def _first_in(x_prompt, x_sample, g, shift, scale, w_bf16):
    t_tiles = SEQ // T_TILE
    last_p = TILES_PER_STREAM - 1
    return pl.pallas_call(
        _first_in_kernel,
        grid=(N_TOK // TM,),
        in_specs=[
            pl.BlockSpec((SUB, T_TILE, D), lambda i: (jnp.minimum(i, last_p) // t_tiles,
                                                      jnp.minimum(i, last_p) % t_tiles, 0)),
            pl.BlockSpec((SUB, T_TILE, D), lambda i: (0, jnp.maximum(i - TILES_PER_STREAM, 0), 0)),
            _full_spec((1, D)), _mod_spec(), _mod_spec(), _full_spec((D, D_IN_EVEN))],
        out_specs=[_row_spec(D_IN_EVEN), _row_spec(D)],
        out_shape=[jax.ShapeDtypeStruct((N_TOK, D_IN_EVEN), F32), jax.ShapeDtypeStruct((N_TOK, D), F32)],
        scratch_shapes=[pltpu.VMEM((SLAB, TM, LANES), F32)],
        compiler_params=_params(1),
        name="first_in",
    )(x_prompt, x_sample, g, shift, scale, w_bf16)


RG_CHUNK = 256
RG_PAD = 16


def _rglru_kernel(u_ref, ga_ref, cw_ref, cb_ref, wg_ref, bg_ref, lam_ref, h0_ref,
                  y_ref, hl_ref, xpad, a0, b0, a1, b1, *, rows):
    steps = rows // SUB
    n_chunks = rows // RG_CHUNK

    xpad[0:RG_PAD, :] = jnp.zeros((RG_PAD, LANES), F32)
    xpad[RG_PAD + rows:RG_PAD + rows + RG_PAD, :] = jnp.zeros((RG_PAD, LANES), F32)

    def copy_chunk(c, carry):
        r = pl.multiple_of(c * RG_CHUNK, RG_CHUNK)
        xpad[pl.ds(RG_PAD + r, RG_CHUNK), :] = u_ref[pl.ds(r, RG_CHUNK), :]
        return carry

    lax.fori_loop(0, n_chunks, copy_chunk, 0)

    lam = lam_ref[...]
    softplus_neg = jnp.maximum(-lam, 0.0) + jnp.log1p(jnp.exp(-jnp.abs(lam)))
    a_refs = (a0, a1)
    b_refs = (b0, b1)

    def gate_chunk(c, carry):
        r = pl.multiple_of(c * RG_CHUNK, RG_CHUNK)
        xc = jnp.zeros((RG_CHUNK, LANES), F32) + cb_ref[...]
        for k in range(A_CONV):
            xc = xc + cw_ref[k:k + 1, :] * xpad[pl.ds(r + SUB * k, RG_CHUNK), :]
        pre = jnp.dot(xc.astype(BF16), wg_ref[...], preferred_element_type=F32) + bg_ref[...]
        for d in range(2):
            rg = jax.nn.sigmoid(pre[:, d * 256:d * 256 + LANES])
            ig = jax.nn.sigmoid(pre[:, d * 256 + LANES:(d + 1) * 256])
            log_a = (-A_C) * rg * softplus_neg[:, d * LANES:(d + 1) * LANES]
            a = jnp.exp(log_a)
            one_minus_a2 = -jnp.tanh(log_a) * (a * a + 1.0)
            a_refs[d][pl.ds(r, RG_CHUNK), :] = a
            b_refs[d][pl.ds(r, RG_CHUNK), :] = jnp.sqrt(one_minus_a2) * (ig * xc)
        return carry

    lax.fori_loop(0, n_chunks, gate_chunk, 0)

    def step(t, carry):
        hf, hb = carry
        rf = pl.multiple_of(t * SUB, SUB)
        rb = pl.multiple_of((steps - 1 - t) * SUB, SUB)
        hf = a0[pl.ds(rf, SUB), :] * hf + b0[pl.ds(rf, SUB), :]
        b0[pl.ds(rf, SUB), :] = hf
        hb = a1[pl.ds(rb, SUB), :] * hb + b1[pl.ds(rb, SUB), :]
        b1[pl.ds(rb, SUB), :] = hb
        return hf, hb

    hf, hb = lax.fori_loop(0, steps, step, (h0_ref[0], h0_ref[1]), unroll=8)
    hl_ref[0] = hf
    hl_ref[1] = hb

    def out_chunk(c, carry):
        r = pl.multiple_of(c * RG_CHUNK, RG_CHUNK)
        y = (b0[pl.ds(r, RG_CHUNK), :] + b1[pl.ds(r, RG_CHUNK), :]) * jax.nn.gelu(ga_ref[pl.ds(r, RG_CHUNK), :])
        y_ref[pl.ds(r, RG_CHUNK), :] = y.astype(BF16)
        return carry

    lax.fori_loop(0, n_chunks, out_chunk, 0)


def _rglru_call(z, cw, cb, wg, bg, lam, h0, *, rows, groups, row_block0):
    n_cb = D_A // LANES
    in_specs = [
        pl.BlockSpec((rows, LANES), lambda g, j: (row_block0 + g, j)),
        pl.BlockSpec((rows, LANES), lambda g, j: (row_block0 + g, n_cb + j)),
        pl.BlockSpec((A_CONV, LANES), lambda g, j: (0, j)),
        pl.BlockSpec((1, LANES), lambda g, j: (0, j)),
        pl.BlockSpec((None, LANES, 4 * LANES), lambda g, j: (j, 0, 0)),
        pl.BlockSpec((None, 1, 4 * LANES), lambda g, j: (j, 0, 0)),
        pl.BlockSpec((None, 1, 2 * LANES), lambda g, j: (j, 0, 0)),
        pl.BlockSpec((None, 2, SUB, LANES), lambda g, j: (g, 0, 0, j)),
    ]
    return pl.pallas_call(
        functools.partial(_rglru_kernel, rows=rows),
        grid=(groups, n_cb),
        in_specs=in_specs,
        out_specs=[
            pl.BlockSpec((rows, LANES), lambda g, j: (g, j)),
            pl.BlockSpec((None, 2, SUB, LANES), lambda g, j: (g, 0, 0, j)),
        ],
        out_shape=[
            jax.ShapeDtypeStruct((groups * rows, D_A), BF16),
            jax.ShapeDtypeStruct((groups, 2, SUB, D_A), F32),
        ],
        scratch_shapes=[pltpu.VMEM((rows + 2 * RG_PAD, LANES), F32)] + [pltpu.VMEM((rows, LANES), F32)] * 4,
        compiler_params=_params(2),
        name=f"rglru_{rows}",
    )(z, z, cw, cb, wg, bg, lam, h0)


POOL_CHUNK = 256
POOL_PAD = 64


def _pool_kernel(u_ref, w_ref, s_ref, y_ref, xpad, *, rows):
    steps = rows // SUB
    n_chunks = rows // POOL_CHUNK
    j = pl.program_id(1)

    xpad[0:POOL_PAD, :] = jnp.zeros((POOL_PAD, LANES), F32)
    xpad[POOL_PAD + rows:POOL_PAD + rows + POOL_PAD, :] = jnp.zeros((POOL_PAD, LANES), F32)

    def copy_chunk(c, carry):
        r = pl.multiple_of(c * POOL_CHUNK, POOL_CHUNK)
        xpad[pl.ds(POOL_PAD + r, POOL_CHUNK), :] = u_ref[pl.ds(r, POOL_CHUNK), :]
        return carry

    lax.fori_loop(0, n_chunks, copy_chunk, 0)

    for gi, win in enumerate(POOL_WINDOWS):
        half = win // 2

        @pl.when(j == gi)
        def _(half=half, win=win):
            def chunk(c, carry):
                r = pl.multiple_of(c * POOL_CHUNK, POOL_CHUNK)
                acc = xpad[pl.ds(POOL_PAD + r - SUB * half, POOL_CHUNK), :]
                for s in range(1, win):
                    acc = acc + xpad[pl.ds(POOL_PAD + r + SUB * (s - half), POOL_CHUNK), :]
                t = (r + lax.broadcasted_iota(jnp.int32, (POOL_CHUNK, 1), 0)) // SUB
                cnt = jnp.minimum(t + half, steps) - jnp.maximum(t - half, 0)
                dlt = acc / cnt.astype(F32) - xpad[pl.ds(POOL_PAD + r, POOL_CHUNK), :]
                y = jnp.dot(dlt.astype(BF16), w_ref[...], preferred_element_type=F32) * s_ref[...]
                y_ref[pl.ds(r, POOL_CHUNK), :] = y.astype(BF16)
                return carry

            lax.fori_loop(0, n_chunks, chunk, 0)


def _pool_call(z, w_pool_bf16, s_pool, *, rows, groups, row_block0):
    n_cb = D_B // LANES
    col0 = 2 * D_A // LANES
    in_specs = [
        pl.BlockSpec((rows, LANES), lambda g, j: (row_block0 + g, col0 + j)),
        pl.BlockSpec((None, B_GS, B_GS), lambda g, j: (j, 0, 0)),
        pl.BlockSpec((1, LANES), lambda g, j: (0, j)),
    ]
    return pl.pallas_call(
        functools.partial(_pool_kernel, rows=rows),
        grid=(groups, n_cb),
        in_specs=in_specs,
        out_specs=pl.BlockSpec((rows, LANES), lambda g, j: (g, j)),
        out_shape=jax.ShapeDtypeStruct((groups * rows, D_B), BF16),
        scratch_shapes=[pltpu.VMEM((rows + 2 * POOL_PAD, LANES), F32)],
        compiler_params=_params(2),
        name=f"pool_{rows}",
    )(z, w_pool_bf16, s_pool)


def _stream_specs(width):
    prompt = pl.BlockSpec((TM, width), lambda i, *_: (jnp.minimum(i, TILES_PER_STREAM - 1), 0))
    sample = pl.BlockSpec((TM, width), lambda i, *_: (jnp.maximum(i - TILES_PER_STREAM, 0), 0))
    return [prompt, sample]


def _pick_stream(prompt_ref, sample_ref):
    return jnp.where(pl.program_id(0) >= TILES_PER_STREAM, sample_ref[...], prompt_ref[...])


def _mix_out_kernel(x_ref, yap_ref, yas_ref, ybp_ref, ybs_ref, wa_ref, wb_ref, gate_ref, o_ref):
    y = jnp.dot(_pick_stream(yap_ref, yas_ref), wa_ref[...], preferred_element_type=F32)
    y = y + jnp.dot(_pick_stream(ybp_ref, ybs_ref), wb_ref[...], preferred_element_type=F32)
    o_ref[...] = _gated_add(x_ref[...], gate_ref[...], y)


def _even_out(x, ya, yb, wa, wb, gate):
    half = D // 2
    return pl.pallas_call(
        _mix_out_kernel,
        grid=(N_TOK // TM,),
        in_specs=[_row_spec(D)] + _stream_specs(half) + _stream_specs(half) +
                 [_full_spec((half, D)), _full_spec((half, D)), _mod_spec()],
        out_specs=_row_spec(D),
        out_shape=jax.ShapeDtypeStruct((N_TOK, D), F32),
        compiler_params=_params(1),
        name="even_out",
    )(x, *ya, *yb, wa, wb, gate)


def _ffn_kernel(x_ref, g_ref, sh_ref, sc_ref, gate_ref, w1_ref, w3_ref, w2_ref, o_ref, h_scr, acc_scr):
    j = pl.program_id(1)

    @pl.when(j == 0)
    def _():
        h = _modulate(_rms(x_ref[...], g_ref[...]), sh_ref[...], sc_ref[...])
        h_scr[...] = h.astype(BF16)

    h = h_scr[...]
    a = jnp.dot(h, w1_ref[...], preferred_element_type=F32)
    b = jnp.dot(h, w3_ref[...], preferred_element_type=F32)
    u = (a * jax.nn.sigmoid(a) * b).astype(BF16)
    y = jnp.dot(u, w2_ref[...], preferred_element_type=F32)

    @pl.when(j == 0)
    def _():
        acc_scr[...] = y

    @pl.when(j > 0)
    def _():
        acc_scr[...] += y

    @pl.when(j == pl.num_programs(1) - 1)
    def _():
        o_ref[...] = _gated_add(x_ref[...], gate_ref[...], acc_scr[...])


def _ffn(x, g, shift, scale, gate, w1, w3, w2, li):
    n_ff = D_FF // D_FF_E
    return pl.pallas_call(
        _ffn_kernel,
        grid=(N_TOK // TM, n_ff),
        in_specs=[
            pl.BlockSpec((TM, D), lambda i, j: (i, 0)),
            pl.BlockSpec((1, D), lambda i, j: (0, 0)),
            pl.BlockSpec((None, SUB, D), lambda i, j: (i // TILES_PER_STREAM, 0, 0)),
            pl.BlockSpec((None, SUB, D), lambda i, j: (i // TILES_PER_STREAM, 0, 0)),
            pl.BlockSpec((None, SUB, D), lambda i, j: (i // TILES_PER_STREAM, 0, 0)),
            pl.BlockSpec((None, D, D_FF_E), lambda i, j: (li, 0, j)),
            pl.BlockSpec((None, D, D_FF_E), lambda i, j: (li, 0, j)),
            pl.BlockSpec((None, D_FF_E, D), lambda i, j: (li, j, 0)),
        ],
        out_specs=pl.BlockSpec((TM, D), lambda i, j: (i, 0)),
        out_shape=jax.ShapeDtypeStruct((N_TOK, D), F32),
        scratch_shapes=[pltpu.VMEM((TM, D), BF16), pltpu.VMEM((TM, D), F32)],
        compiler_params=_params(2),
        name="ffn",
    )(x, g, shift, scale, gate, w1, w3, w2)


def _head_rms(x, ones_bd, g):
    sq = x * x
    hi = sq.astype(BF16)
    lo = (sq - hi.astype(F32)).astype(BF16)
    ms = jnp.dot(hi, ones_bd, preferred_element_type=F32) + jnp.dot(lo, ones_bd, preferred_element_type=F32)
    return x * lax.rsqrt(ms + EPS) * g


def _rope(x, cos, sin_signed):
    w = x.shape[-1]
    lane = lax.broadcasted_iota(jnp.int32, x.shape, 1)
    first = (lane % 32) < 16
    partner = jnp.where(first, pltpu.roll(x, w - 16, 1), pltpu.roll(x, 16, 1))
    return x * cos + partner * sin_signed


Q_BLOCKS = D_ATT // LANES


def _odd_in_kernel(x_ref, g_ref, sh_ref, sc_ref, w_ref, bdq_ref, bdk_ref, qg_ref, kg_ref, cos_ref, sin_ref,
                   glu_ref, q0_ref, q1_ref, q2_ref, q3_ref, k_ref, v_ref):
    i = pl.program_id(0)
    h = _modulate(_rms(x_ref[...], g_ref[...]), sh_ref[...], sc_ref[...])
    z = jnp.dot(h.astype(BF16), w_ref[...], preferred_element_type=F32)
    glu_ref[...] = z[:, :D_C] * jax.nn.sigmoid(z[:, D_C:2 * D_C])
    o1 = 2 * D_C
    o2 = o1 + D_ATT
    o3 = o2 + D_KV
    q = _head_rms(z[:, o1:o2], bdq_ref[...], qg_ref[...])
    k = _head_rms(z[:, o2:o3], bdk_ref[...], kg_ref[...])
    cos = cos_ref[...]
    sin = sin_ref[...]
    is_sample = i >= TILES_PER_STREAM
    q_r = _rope(q, jnp.concatenate([cos] * 4, axis=1), jnp.concatenate([sin] * 4, axis=1))
    k_r = _rope(k, cos, sin)
    q = jnp.where(is_sample, q_r, q)
    for c, q_ref in enumerate((q0_ref, q1_ref, q2_ref, q3_ref)):
        q_ref[...] = q[:, c * LANES:(c + 1) * LANES]
    k_ref[...] = jnp.where(is_sample, k_r, k)
    v_ref[...] = z[:, o3:]


def _odd_in(x, g, shift, scale, w_bf16, bdq, bdk, qg, kg, cos_t, sin_t):
    rope_spec = pl.BlockSpec((TM, D_KV), lambda i: (jnp.maximum(i - TILES_PER_STREAM, 0), 0))
    return pl.pallas_call(
        _odd_in_kernel,
        grid=(N_TOK // TM,),
        in_specs=[_row_spec(D), _full_spec((1, D)), _mod_spec(), _mod_spec(), _full_spec((D, D_IN_ODD)),
                  _full_spec((D_ATT, D_ATT)), _full_spec((D_KV, D_KV)), _full_spec((1, D_ATT)),
                  _full_spec((1, D_KV)), rope_spec, rope_spec],
        out_specs=[_row_spec(D_C)] + [_row_spec(LANES)] * (Q_BLOCKS + 2),
        out_shape=[jax.ShapeDtypeStruct((N_TOK, D_C), F32)] +
                  [jax.ShapeDtypeStruct((N_TOK, LANES), F32)] * (Q_BLOCKS + 2),
        compiler_params=_params(1),
        name="odd_in",
    )(x, g, shift, scale, w_bf16, bdq, bdk, qg, kg, cos_t, sin_t)


def _attend(q_refs, o_refs, q_rows, k_all, v_all):
    scale = HEAD_DIM ** -0.5
    heads_per_block = LANES // HEAD_DIM
    group = N_Q_HEADS // N_KV_HEADS
    kv = []
    for h in range(N_KV_HEADS):
        kv.append((k_all[:, h * HEAD_DIM:(h + 1) * HEAD_DIM].astype(BF16),
                   v_all[:, h * HEAD_DIM:(h + 1) * HEAD_DIM].astype(BF16)))
    for c in range(Q_BLOCKS):
        qc = (q_refs[c][q_rows, :] * scale).astype(BF16)
        outs = []
        for sub in range(heads_per_block):
            kh, vh = kv[(c * heads_per_block + sub) // group]
            qh = qc[:, sub * HEAD_DIM:(sub + 1) * HEAD_DIM]
            s = lax.dot_general(qh, kh, (((1,), (1,)), ((), ())), preferred_element_type=F32)
            m = jnp.max(s, axis=-1, keepdims=True)
            p = jnp.exp(s - m)
            l = jnp.sum(p, axis=-1, keepdims=True)
            outs.append(jnp.dot(p.astype(BF16), vh, preferred_element_type=F32) / l)
        o_refs[c][q_rows, :] = jnp.concatenate(outs, axis=1)


def _attn_prompt_kernel(*refs):
    q_refs, (k_ref, v_ref), o_refs = refs[:Q_BLOCKS], refs[Q_BLOCKS:Q_BLOCKS + 2], refs[Q_BLOCKS + 2:]
    rows = pl.ds(pl.program_id(1), SEQ, stride=SUB)
    _attend(q_refs, o_refs, rows, k_ref[rows, :], v_ref[rows, :])


def _attn_sample_kernel(*refs):
    q_refs, (k_ref, v_ref, ck_ref, cv_ref), o_refs = refs[:Q_BLOCKS], refs[Q_BLOCKS:Q_BLOCKS + 4], refs[Q_BLOCKS + 4:]
    b = pl.program_id(1)
    kv_rows = pl.ds(b, DEC_SEQ, stride=SUB)
    k_all = jnp.concatenate([ck_ref[...], k_ref[kv_rows, :]], axis=0)
    v_all = jnp.concatenate([cv_ref[...], v_ref[kv_rows, :]], axis=0)
    _attend(q_refs, o_refs, pl.ds(b, SEQ, stride=SUB), k_all, v_all)


def _attention(qs, k, v, cache_k_l, cache_v_l):
    chunk = pl.BlockSpec((R_PROMPT, LANES), lambda g, b: (g, 0))
    out_shape = [jax.ShapeDtypeStruct((N_PROMPT, LANES), F32)] * Q_BLOCKS
    att_p = pl.pallas_call(
        _attn_prompt_kernel,
        grid=(P_GROUPS, SUB),
        in_specs=[chunk] * (Q_BLOCKS + 2),
        out_specs=[chunk] * Q_BLOCKS,
        out_shape=out_shape,
        compiler_params=_params(2),
        name="attn_prompt",
    )(*qs, k, v)
    q_chunks = DEC_SEQ // SEQ
    q_chunk = pl.BlockSpec((R_PROMPT, LANES), lambda c, b: (P_GROUPS + c, 0))
    kv_all = pl.BlockSpec((R_SAMPLE, LANES), lambda c, b: (1, 0))
    cache = pl.BlockSpec((None, PAST_LEN, D_KV), lambda c, b: (b, 0, 0))
    att_s = pl.pallas_call(
        _attn_sample_kernel,
        grid=(q_chunks, DEC_BATCH),
        in_specs=[q_chunk] * Q_BLOCKS + [kv_all, kv_all, cache, cache],
        out_specs=[pl.BlockSpec((R_PROMPT, LANES), lambda c, b: (c, 0))] * Q_BLOCKS,
        out_shape=out_shape,
        compiler_params=_params(2),
        name="attn_sample",
    )(*qs, k, v, cache_k_l, cache_v_l)
    return att_p, att_s


CV_CHUNK = 64
CV_PAD = (C_CONV // 2) * SUB


def _conv_kernel(u_ref, w_ref, b_ref, y_ref, xpad, *, rows):
    n_copy = rows // 256
    xpad[0:CV_PAD, :] = jnp.zeros((CV_PAD, LANES), F32)
    xpad[CV_PAD + rows:CV_PAD + rows + CV_PAD, :] = jnp.zeros((CV_PAD, LANES), F32)

    def copy_chunk(c, carry):
        r = pl.multiple_of(c * 256, 256)
        xpad[pl.ds(CV_PAD + r, 256), :] = u_ref[pl.ds(r, 256), :]
        return carry

    lax.fori_loop(0, n_copy, copy_chunk, 0)

    def chunk(c, carry):
        r = pl.multiple_of(c * CV_CHUNK, CV_CHUNK)
        acc = jnp.zeros((CV_CHUNK, LANES), F32) + b_ref[...]
        for k in range(C_CONV):
            acc = acc + w_ref[k:k + 1, :] * xpad[pl.ds(r + SUB * k, CV_CHUNK), :]
        y_ref[pl.ds(r, CV_CHUNK), :] = acc
        return carry

    lax.fori_loop(0, rows // CV_CHUNK, chunk, 0)


def _conv_call(glu, w, b, *, rows, groups, row_block0):
    n_cb = D_C // LANES
    in_specs = [
        pl.BlockSpec((rows, LANES), lambda g, j: (row_block0 + g, j)),
        pl.BlockSpec((C_CONV, LANES), lambda g, j: (0, j)),
        pl.BlockSpec((1, LANES), lambda g, j: (0, j)),
    ]
    return pl.pallas_call(
        functools.partial(_conv_kernel, rows=rows),
        grid=(groups, n_cb),
        in_specs=in_specs,
        out_specs=pl.BlockSpec((rows, LANES), lambda g, j: (g, j)),
        out_shape=jax.ShapeDtypeStruct((groups * rows, D_C), F32),
        scratch_shapes=[pltpu.VMEM((rows + 2 * CV_PAD, LANES), F32)],
        compiler_params=_params(2),
        name=f"conv_{rows}",
    )(glu, w, b)


def _odd_out_kernel(x_ref, hcp_ref, hcs_ref, *refs):
    att_refs, refs = refs[:2 * Q_BLOCKS], refs[2 * Q_BLOCKS:]
    (lng_ref, lnb_ref, wc_ref, wa_ref, gate_ref, g2_ref, sh_ref, sc_ref, wrh_ref, wrl_ref, br_ref,
     x1_ref, h2_ref, route_ref) = refs
    att = jnp.concatenate([_pick_stream(att_refs[c], att_refs[Q_BLOCKS + c]) for c in range(Q_BLOCKS)], axis=1)
    hc = _pick_stream(hcp_ref, hcs_ref)
    mu = jnp.mean(hc, axis=-1, keepdims=True)
    xc = hc - mu
    var = jnp.mean(xc * xc, axis=-1, keepdims=True)
    ln = xc * lax.rsqrt(var + EPS) * lng_ref[...] + lnb_ref[...]
    yc = (ln * jax.nn.sigmoid(ln)).astype(BF16)
    y = jnp.dot(yc, wc_ref[...], preferred_element_type=F32)
    y = y + jnp.dot(att.astype(BF16), wa_ref[...], preferred_element_type=F32)
    x1 = _gated_add(x_ref[...], gate_ref[...], y)
    x1_ref[...] = x1
    h2 = _modulate(_rms(x1, g2_ref[...]), sh_ref[...], sc_ref[...])
    for j in range(SLAB):
        h2_ref[pl.ds(j, TM, stride=SLAB), :] = h2[:, j * LANES:(j + 1) * LANES]
    h2_hi = h2.astype(BF16)
    h2_lo = (h2 - h2_hi.astype(F32)).astype(BF16)
    logits = (jnp.dot(h2_hi, wrh_ref[...], preferred_element_type=F32)
              + jnp.dot(h2_lo, wrh_ref[...], preferred_element_type=F32)
              + jnp.dot(h2_hi, wrl_ref[...], preferred_element_type=F32)) + br_ref[...]
    lane = lax.broadcasted_iota(jnp.int32, logits.shape, 1).astype(F32)
    lg = jnp.where(lane < N_EXPERTS, logits, NEG_BIG)
    m1 = jnp.max(lg, axis=-1, keepdims=True)
    i1 = jnp.min(jnp.where(lg == m1, lane, float(LANES)), axis=-1, keepdims=True)
    lg2 = jnp.where(lane == i1, NEG_BIG, lg)
    m2 = jnp.max(lg2, axis=-1, keepdims=True)
    i2 = jnp.min(jnp.where(lg2 == m2, lane, float(LANES)), axis=-1, keepdims=True)
    e = jnp.exp(m2 - m1)
    den = 1.0 + e
    route = jnp.where(lane == 0.0, i1, jnp.where(lane == 1.0, i2,
                      jnp.where(lane == 2.0, 1.0 / den, jnp.where(lane == 3.0, e / den, 0.0))))
    route_ref[...] = route


def _odd_out(x, hc, att, lng, lnb, wc, wa, gate, g2, shift2, scale2, wr_hi, wr_lo, br_pad):
    half = D // 2
    att_specs = [_stream_specs(LANES)[0]] * Q_BLOCKS + [_stream_specs(LANES)[1]] * Q_BLOCKS
    return pl.pallas_call(
        _odd_out_kernel,
        grid=(N_TOK // TM,),
        in_specs=[_row_spec(D)] + _stream_specs(half) + att_specs +
                 [_full_spec((1, half)), _full_spec((1, half)),
                  _full_spec((half, D)), _full_spec((half, D)), _mod_spec(),
                  _full_spec((1, D)), _mod_spec(), _mod_spec(),
                  _full_spec((D, LANES)), _full_spec((D, LANES)), _full_spec((1, LANES))],
        out_specs=[_row_spec(D), pl.BlockSpec((TM * SLAB, LANES), lambda i: (i, 0)), _row_spec(LANES)],
        out_shape=[jax.ShapeDtypeStruct((N_TOK, D), F32), jax.ShapeDtypeStruct((N_TOK * SLAB, LANES), F32),
                   jax.ShapeDtypeStruct((N_TOK, LANES), F32)],
        compiler_params=_params(1),
        name="odd_out",
    )(x, *hc, *att[0], *att[1], lng, lnb, wc, wa, gate, g2, shift2, scale2, wr_hi, wr_lo, br_pad)


MOE_STEPS = MOE_TILES + 1
MOE_BURSTS = ((0, 176), (176, 344), (344, MOE_TM))


def _moe_kernel(te_ref, nv_ref, nu_ref, jprev_ref, jcur_ref, jnext_ref, h2_hbm, w1_ref, w3_ref, w2_ref, y2_hbm,
                xbuf, obuf, h_scr, a_scr, u_scr, gsem, ssem):
    del te_ref
    i = pl.program_id(0)
    n_used = nu_ref[0]
    slot = i % 2
    other = 1 - slot
    compute = i < n_used
    gather_next = i + 1 < n_used
    nv_prev = jnp.where(jnp.logical_and(i >= 1, i <= n_used), nv_ref[jnp.clip(i - 1, 0, MOE_TILES - 1)], 0)

    def rows_of(r):
        return pl.ds(pl.multiple_of(r * SLAB, SLAB), SLAB)

    def gather_one(j_ref, s, r, priority):
        tok = lax.shift_right_logical(j_ref[0, r], 1)
        pltpu.make_async_copy(h2_hbm.at[rows_of(tok), :], xbuf.at[s, rows_of(r), :],
                              gsem.at[s]).start(priority=priority)

    def scatter_one(s, r, priority):
        pltpu.make_async_copy(obuf.at[s, rows_of(r), :], y2_hbm.at[rows_of(jprev_ref[0, r]), :],
                              ssem.at[s]).start(priority=priority)

    def issue(lo, hi, one):
        n = jnp.maximum(hi - lo, 0)
        full = lax.shift_right_logical(n, DMA_UNROLL.bit_length() - 1)

        def body(c, carry):
            for k in range(DMA_UNROLL):
                one(lo + c * DMA_UNROLL + k, k % 2)
            return carry

        lax.fori_loop(0, full, body, 0)

        def tail(r, carry):
            one(r, 0)
            return carry

        lax.fori_loop(lo + full * DMA_UNROLL, lo + n, tail, 0)

    def wait_gather(s):
        pltpu.make_async_copy(h2_hbm.at[pl.ds(0, MOE_TM * SLAB), :], xbuf.at[s], gsem.at[s]).wait()

    def wait_scatter(s, n):
        @pl.when(n > 0)
        def _():
            pltpu.make_async_copy(obuf.at[s, pl.ds(0, n * SLAB), :], y2_hbm.at[pl.ds(0, n * SLAB), :],
                                  ssem.at[s]).wait()

    def burst(part):
        lo, hi = MOE_BURSTS[part]

        @pl.when(gather_next)
        def _():
            issue(lo, hi, lambda r, p: gather_one(jnext_ref, other, r, p))

        issue(lo, jnp.minimum(hi, nv_prev), lambda r, p: scatter_one(other, r, p))

    @pl.when(i == 0)
    def _():
        issue(0, MOE_TM, lambda r, p: gather_one(jcur_ref, 0, r, p))

    @pl.when(compute)
    def _():
        wait_gather(slot)

    burst(0)

    @pl.when(compute)
    def _():
        x = jnp.concatenate([xbuf[slot, pl.ds(j, MOE_TM, stride=SLAB), :] for j in range(SLAB)], axis=1)
        h = x.astype(BF16)
        h_scr[...] = h
        a_scr[...] = jnp.dot(h, w1_ref[...], preferred_element_type=F32)

    burst(1)

    @pl.when(compute)
    def _():
        a = a_scr[...]
        b = jnp.dot(h_scr[...], w3_ref[...], preferred_element_type=F32)
        u_scr[...] = (a * jax.nn.sigmoid(a) * b).astype(BF16)

    burst(2)

    @pl.when(compute)
    def _():
        y = jnp.dot(u_scr[...], w2_ref[...], preferred_element_type=F32)

        @pl.when(i >= 2)
        def _():
            wait_scatter(slot, nv_ref[jnp.maximum(i - 2, 0)])

        for j in range(SLAB):
            obuf[slot, pl.ds(j, MOE_TM, stride=SLAB), :] = y[:, j * LANES:(j + 1) * LANES]

    @pl.when(i == n_used)
    def _():
        @pl.when(i >= 2)
        def _():
            wait_scatter(slot, nv_ref[jnp.maximum(i - 2, 0)])

        wait_scatter(other, nv_prev)


def _moe_experts(tile_expert, tile_valid, n_used, jtab, h2, w1, w3, w2, li):
    jtab3 = jtab.reshape(MOE_TILES, 1, MOE_TM)
    last = MOE_TILES - 1

    def table(shift):
        return pl.BlockSpec((None, 1, MOE_TM), lambda i, *_: (jnp.clip(i + shift, 0, last), 0, 0),
                            memory_space=pltpu.SMEM)

    def weight(rows, cols):
        return pl.BlockSpec((None, None, rows, cols), lambda i, te, *_: (li, te[jnp.minimum(i, last)], 0, 0))

    grid_spec = pltpu.PrefetchScalarGridSpec(
        num_scalar_prefetch=3,
        grid=(MOE_STEPS,),
        in_specs=[table(-1), table(0), table(1), pl.BlockSpec(memory_space=pl.ANY),
                  weight(D, D_FF_E), weight(D, D_FF_E), weight(D_FF_E, D)],
        out_specs=pl.BlockSpec(memory_space=pl.ANY),
        scratch_shapes=[
            pltpu.VMEM((2, MOE_TM * SLAB, LANES), F32),
            pltpu.VMEM((2, MOE_TM * SLAB, LANES), F32),
            pltpu.VMEM((MOE_TM, D), BF16),
            pltpu.VMEM((MOE_TM, D_FF_E), F32),
            pltpu.VMEM((MOE_TM, D_FF_E), BF16),
            pltpu.SemaphoreType.DMA((2,)),
            pltpu.SemaphoreType.DMA((2,)),
        ],
    )
    return pl.pallas_call(
        _moe_kernel,
        grid_spec=grid_spec,
        out_shape=jax.ShapeDtypeStruct((2 * N_TOK * SLAB, LANES), F32),
        compiler_params=_params(1),
        name="moe_experts",
    )(tile_expert, tile_valid, n_used, jtab3, jtab3, jtab3, h2, w1, w3, w2)


def _route_plan(route):
    ids = route[:, 0:2].astype(jnp.int32).reshape(-1)
    onehot = (ids[:, None] == jnp.arange(N_EXPERTS, dtype=jnp.int32)[None, :]).astype(jnp.int32)
    csum = jnp.cumsum(onehot, axis=0)
    rank = jnp.sum(csum * onehot, axis=1) - 1
    counts = csum[-1]
    tiles = (counts + MOE_TM - 1) // MOE_TM
    tile_end = jnp.cumsum(tiles)
    group_start = (tile_end - tiles) * MOE_TM
    pos = jnp.sum(onehot * group_start[None, :], axis=1) + rank
    jtab = jnp.zeros((MOE_ROWS,), jnp.int32).at[pos].set(jnp.arange(2 * N_TOK, dtype=jnp.int32))
    tile_ids = jnp.arange(MOE_TILES, dtype=jnp.int32)
    tile_expert = jnp.minimum(jnp.sum((tile_ids[:, None] >= tile_end[None, :]).astype(jnp.int32), axis=1),
                              N_EXPERTS - 1)
    rows_before = (tile_ids - (tile_end - tiles)[tile_expert]) * MOE_TM
    tile_valid = jnp.clip(counts[tile_expert] - rows_before, 0, MOE_TM)
    tile_valid = jnp.where(tile_ids < tile_end[-1], tile_valid, 0).astype(jnp.int32)
    n_used = tile_end[-1:].astype(jnp.int32)
    return jtab, tile_expert, tile_valid, n_used


def _combine_kernel(x_ref, y_ref, route_ref, gate_ref, o_ref):
    g0 = route_ref[:, 2:3]
    g1 = route_ref[:, 3:4]
    parts = [g0 * y_ref[pl.ds(j, TM, stride=2 * SLAB), :] + g1 * y_ref[pl.ds(SLAB + j, TM, stride=2 * SLAB), :]
             for j in range(SLAB)]
    o_ref[...] = _gated_add(x_ref[...], gate_ref[...], jnp.concatenate(parts, axis=1))


def _combine(x, y2, route, gate):
    return pl.pallas_call(
        _combine_kernel,
        grid=(N_TOK // TM,),
        in_specs=[_row_spec(D), pl.BlockSpec((TM * 2 * SLAB, LANES), lambda i: (i, 0)),
                  _row_spec(LANES), _mod_spec()],
        out_specs=_row_spec(D),
        out_shape=jax.ShapeDtypeStruct((N_TOK, D), F32),
        compiler_params=_params(1),
        name="moe_combine",
    )(x, y2, route, gate)


def _final_kernel(x_ref, g_ref, o_ref, cols):
    y = _rms(x_ref[...], g_ref[...])
    for c in range(SLAB):
        cols[c] = y[:, c * LANES:(c + 1) * LANES]
    for b in range(SUB):
        for c in range(SLAB):
            o_ref[b, :, c * LANES:(c + 1) * LANES] = cols[c, pl.ds(b, T_TILE, stride=SUB), :]


def _final_norm(x, g, *, batch, steps, tile0):
    t_tiles = steps // T_TILE
    return pl.pallas_call(
        _final_kernel,
        grid=(batch * steps // TM,),
        in_specs=[pl.BlockSpec((TM, D), lambda i: (tile0 + i, 0)), _full_spec((1, D))],
        out_specs=pl.BlockSpec((SUB, T_TILE, D), lambda i: (i // t_tiles, i % t_tiles, 0)),
        out_shape=jax.ShapeDtypeStruct((batch, steps, D), F32),
        scratch_shapes=[pltpu.VMEM((SLAB, TM, LANES), F32)],
        compiler_params=_params(1),
        name=f"final_norm_{steps}",
    )(x, g)


def _block_diag_gates(w_r, w_i, b_r, b_i):
    def bd(w):
        w4 = w.reshape(4, 2, A_BS, A_BS)
        z = jnp.zeros((4, A_BS, A_BS), w.dtype)
        top = jnp.concatenate([w4[:, 0], z], axis=2)
        bot = jnp.concatenate([z, w4[:, 1]], axis=2)
        return jnp.concatenate([top, bot], axis=1)

    wg = jnp.concatenate([bd(w_r[0]), bd(w_i[0]), bd(w_r[1]), bd(w_i[1])], axis=2)
    bg = jnp.concatenate([b_r[0].reshape(4, 1, LANES), b_i[0].reshape(4, 1, LANES),
                          b_r[1].reshape(4, 1, LANES), b_i[1].reshape(4, 1, LANES)], axis=2)
    return wg.astype(BF16), bg


def _head_mean_matrix(width):
    idx = jnp.arange(width) // HEAD_DIM
    return ((idx[:, None] == idx[None, :]).astype(F32) / HEAD_DIM).astype(BF16)


def _rope_tables():
    pos = jnp.arange(DEC_SEQ)
    row = (pos // GRID_W).astype(F32)
    col = (pos % GRID_W).astype(F32)
    n_freq = HEAD_DIM // 4
    inv = ROPE_THETA ** (-jnp.arange(n_freq, dtype=F32) / n_freq)
    ang = jnp.stack([row[:, None] * inv, col[:, None] * inv], axis=1)
    cos = jnp.cos(ang)
    sin = jnp.sin(ang)
    cos_h = jnp.stack([cos, cos], axis=2).reshape(DEC_SEQ, HEAD_DIM)
    sin_h = jnp.stack([-sin, sin], axis=2).reshape(DEC_SEQ, HEAD_DIM)
    cos_t = jnp.tile(cos_h, (1, 2))
    sin_t = jnp.tile(sin_h, (1, 2))
    cos_t = jnp.broadcast_to(cos_t[:, None, :], (DEC_SEQ, SUB, D_KV)).reshape(N_SAMPLE, D_KV)
    sin_t = jnp.broadcast_to(sin_t[:, None, :], (DEC_SEQ, SUB, D_KV)).reshape(N_SAMPLE, D_KV)
    return cos_t, sin_t


def _prompt_to_batch_major(a):
    w = a.shape[-1]
    return a[:N_PROMPT].reshape(P_GROUPS, SEQ, SUB, w).transpose(0, 2, 1, 3).reshape(BATCH, SEQ, w)


def kernel(x_prompt, x_sample, c, state_rglru, cache_k, cache_v, c_ctx, w_mod, b_mod, norm1, norm2, ev_w_in, a_conv_w, a_conv_b, a_w_r, a_b_r, a_w_i, a_b_i, a_lam, b_w_pool, b_scale, ev_w_out, od_w_in, c_conv_w, c_conv_b, c_ln_g, c_ln_b, q_norm, k_norm, od_w_out, ff_w1, ff_w3, ff_w2, moe_w_router, moe_b_router, moe_w1, moe_w3, moe_w2, norm_f):
    cond16 = jnp.concatenate([c_ctx[None, :], c, jnp.zeros((16 - 1 - DEC_BATCH, D), F32)], axis=0)
    mods = _ada_params(cond16, w_mod, b_mod)
    mods = jnp.stack([jnp.broadcast_to(mods[:, :, 0:1], (DEPTH, 6, SUB, D)), mods[:, :, 1:1 + SUB]], axis=2)

    cos_t, sin_t = _rope_tables()
    bdq = _head_mean_matrix(D_ATT)
    bdk = _head_mean_matrix(D_KV)
    cache_k4 = cache_k.reshape(DEC_BATCH, DEPTH // 2, PAST_LEN, D_KV)
    cache_v4 = cache_v.reshape(DEC_BATCH, DEPTH // 2, PAST_LEN, D_KV)
    ff_w1b, ff_w3b, ff_w2b = ff_w1.astype(BF16), ff_w3.astype(BF16), ff_w2.astype(BF16)
    moe_w1b, moe_w3b, moe_w2b = moe_w1.astype(BF16), moe_w3.astype(BF16), moe_w2.astype(BF16)

    x = None
    new_states, new_k, new_v = [], [], []
    for layer in range(DEPTH):
        li = layer // 2
        shift1, scale1, gate1, shift2, scale2, gate2 = [mods[layer, j] for j in range(6)]
        g1 = norm1[layer].reshape(1, D)
        g2 = norm2[layer].reshape(1, D)
        if layer % 2 == 0:
            if layer == 0:
                z, x = _first_in(x_prompt, x_sample, g1, shift1, scale1, ev_w_in[li].astype(BF16))
            else:
                z = _even_in(x, g1, shift1, scale1, ev_w_in[li].astype(BF16))
            wg, bg = _block_diag_gates(a_w_r[li], a_w_i[li], a_b_r[li], a_b_i[li])
            lam = jnp.concatenate([a_lam[li, 0].reshape(4, 1, LANES), a_lam[li, 1].reshape(4, 1, LANES)], axis=2)
            cb = a_conv_b[li].reshape(1, D_A)
            h0_p = jnp.zeros((P_GROUPS, 2, SUB, D_A), F32)
            h0_s = state_rglru[:, li].transpose(1, 0, 2)[None]
            ya_p, h_last = _rglru_call(z, a_conv_w[li], cb, wg, bg, lam, h0_p,
                                       rows=R_PROMPT, groups=P_GROUPS, row_block0=0)
            ya_s, _ = _rglru_call(z, a_conv_w[li], cb, wg, bg, lam, h0_s,
                                  rows=R_SAMPLE, groups=1, row_block0=1)
            wp = b_w_pool[li].astype(BF16)
            sp = b_scale[li].reshape(1, D_B)
            yb_p = _pool_call(z, wp, sp, rows=R_PROMPT, groups=P_GROUPS, row_block0=0)
            yb_s = _pool_call(z, wp, sp, rows=R_SAMPLE, groups=1, row_block0=1)
            w_out = ev_w_out[li].astype(BF16)
            x = _even_out(x, (ya_p, ya_s), (yb_p, yb_s), w_out[:D_A], w_out[D_A:], gate1)
            x = _ffn(x, g2, shift2, scale2, gate2, ff_w1b, ff_w3b, ff_w2b, li)
            new_states.append(h_last.transpose(0, 2, 1, 3).reshape(BATCH, 2, D_A))
        else:
            qg = jnp.tile(q_norm[li], N_Q_HEADS).reshape(1, D_ATT)
            kg = jnp.tile(k_norm[li], N_KV_HEADS).reshape(1, D_KV)
            glu, *qs, k, v = _odd_in(x, g1, shift1, scale1, od_w_in[li].astype(BF16), bdq, bdk, qg, kg,
                                     cos_t, sin_t)
            att = _attention(qs, k, v, cache_k4[:, li], cache_v4[:, li])
            cw = c_conv_w[li]
            cb = c_conv_b[li].reshape(1, D_C)
            hc = (_conv_call(glu, cw, cb, rows=R_PROMPT, groups=P_GROUPS, row_block0=0),
                  _conv_call(glu, cw, cb, rows=R_SAMPLE, groups=1, row_block0=1))
            w_out = od_w_out[li].astype(BF16)
            wr = jnp.zeros((D, LANES), F32).at[:, :N_EXPERTS].set(moe_w_router[li])
            br = jnp.zeros((1, LANES), F32).at[0, :N_EXPERTS].set(moe_b_router[li])
            wr_hi = wr.astype(BF16)
            wr_lo = (wr - wr_hi.astype(F32)).astype(BF16)
            x1, h2, route = _odd_out(x, hc, att, c_ln_g[li].reshape(1, D_C), c_ln_b[li].reshape(1, D_C),
                                     w_out[:D_C], w_out[D_C:], gate1, g2, shift2, scale2, wr_hi, wr_lo, br)
            jtab, tile_expert, tile_valid, n_used = _route_plan(route)
            y2 = _moe_experts(tile_expert, tile_valid, n_used, jtab, h2, moe_w1b, moe_w3b, moe_w2b, li)
            x = _combine(x1, y2, route, gate2)
            new_k.append(_prompt_to_batch_major(k).reshape(BATCH, SEQ, N_KV_HEADS, HEAD_DIM))
            new_v.append(_prompt_to_batch_major(v).reshape(BATCH, SEQ, N_KV_HEADS, HEAD_DIM))

    gf = norm_f.reshape(1, D)
    y_prompt = _final_norm(x, gf, batch=BATCH, steps=SEQ, tile0=0)
    y_sample = _final_norm(x, gf, batch=DEC_BATCH, steps=DEC_SEQ, tile0=TILES_PER_STREAM)
    return (y_prompt, y_sample, jnp.stack(new_states, axis=1), jnp.stack(new_k, axis=1), jnp.stack(new_v, axis=1))
```

```python
import functools

import jax
import jax.numpy as jnp
from jax import lax
from jax.experimental import pallas as pl
from jax.experimental.pallas import tpu as pltpu

F32 = jnp.float32
BF16 = jnp.bfloat16

D = 1024
BATCH = 32
SEQ = 256
DEPTH = 4
DEC_BATCH = 8
DEC_SEQ = 1024
PAST_LEN = 256
GRID_W = 64
EPS = 1e-6
D_A = 512
A_BLOCKS = 8
A_BS = 64
A_CONV = 4
A_C = 8.0
D_B = 512
POOL_WINDOWS = (2, 4, 8, 16)
B_GS = 128
D_C = 512
C_CONV = 31
HEAD_DIM = 64
N_Q_HEADS = 8
N_KV_HEADS = 2
D_ATT = 512
D_KV = 128
ROPE_THETA = 10000.0
D_FF = 2816
N_EXPERTS = 8
D_FF_E = 1408
D_IN_EVEN = 1536
D_IN_ODD = 1792

SUB = 8
LANES = 128
N_PROMPT = BATCH * SEQ
N_SAMPLE = DEC_BATCH * DEC_SEQ
N_TOK = N_PROMPT + N_SAMPLE
P_GROUPS = BATCH // SUB
R_PROMPT = SEQ * SUB
R_SAMPLE = DEC_SEQ * SUB
TM = 512
TILES_PER_STREAM = N_PROMPT // TM
MOE_TM = 512
MOE_TILES = 2 * N_TOK // MOE_TM + N_EXPERTS
MOE_ROWS = MOE_TILES * MOE_TM
SLAB = D // LANES
DMA_UNROLL = 8
NEG_BIG = -3.0e38
VMEM_LIMIT = 56 * 1024 * 1024


def _params(n_axes, vmem=VMEM_LIMIT):
    return pltpu.CompilerParams(dimension_semantics=("arbitrary",) * n_axes, vmem_limit_bytes=vmem)


def _rms(x, g):
    ms = jnp.mean(x * x, axis=-1, keepdims=True)
    return x * lax.rsqrt(ms + EPS) * g


def _modulate(xn, shift, scale):
    tm = xn.shape[0]
    h = xn.reshape(tm // SUB, SUB, D) * (1.0 + scale)[None] + shift[None]
    return h.reshape(tm, D)


def _gated_add(x, gate, y):
    tm = x.shape[0]
    return x + (y.reshape(tm // SUB, SUB, D) * gate[None]).reshape(tm, D)


def _mod_spec():
    return pl.BlockSpec((None, SUB, D), lambda i, *_: (i // TILES_PER_STREAM, 0, 0))


def _row_spec(width, tm=TM):
    return pl.BlockSpec((tm, width), lambda i, *_: (i, 0))


def _full_spec(shape):
    nd = len(shape)
    return pl.BlockSpec(shape, lambda i, *_: (0,) * nd)


def _ada_kernel(c_ref, w_ref, b_ref, o_ref):
    c = c_ref[...]
    s = (c * jax.nn.sigmoid(c)).astype(BF16)
    o_ref[...] = jnp.dot(s, w_ref[...].astype(BF16), preferred_element_type=F32) + b_ref[...]


def _ada_params(cond16, w_mod, b_mod):
    return pl.pallas_call(
        _ada_kernel,
        grid=(DEPTH, 6),
        in_specs=[
            pl.BlockSpec((16, D), lambda l, j: (0, 0)),
            pl.BlockSpec((None, D, D), lambda l, j: (l, 0, j)),
            pl.BlockSpec((None, None, 1, D), lambda l, j: (l, j, 0, 0)),
        ],
        out_specs=pl.BlockSpec((None, None, 16, D), lambda l, j: (l, j, 0, 0)),
        out_shape=jax.ShapeDtypeStruct((DEPTH, 6, 16, D), F32),
        compiler_params=_params(2),
        name="ada_params",
    )(cond16, w_mod, b_mod.reshape(DEPTH, 6, 1, D))


T_TILE = TM // SUB


def _first_in_kernel(xp_ref, xs_ref, g_ref, sh_ref, sc_ref, w_ref, z_ref, x_ref, cols):
    is_sample = pl.program_id(0) >= TILES_PER_STREAM
    for b in range(SUB):
        xb = jnp.where(is_sample, xs_ref[b], xp_ref[b])
        for c in range(SLAB):
            cols[c, pl.ds(b, T_TILE, stride=SUB), :] = xb[:, c * LANES:(c + 1) * LANES]
    x = jnp.concatenate([cols[c] for c in range(SLAB)], axis=1)
    x_ref[...] = x
    h = _modulate(_rms(x, g_ref[...]), sh_ref[...], sc_ref[...])
    z_ref[...] = jnp.dot(h.astype(BF16), w_ref[...], preferred_element_type=F32)


def _first_in(x_prompt, x_sample, g, shift, scale, w_bf16):
    t_tiles = SEQ // T_TILE
    last_p = TILES_PER_STREAM - 1
    return pl.pallas_call(
        _first_in_kernel,
        grid=(N_TOK // TM,),
        in_specs=[
            pl.BlockSpec((SUB, T_TILE, D), lambda i: (jnp.minimum(i, last_p) // t_tiles,
                                                      jnp.minimum(i, last_p) % t_tiles, 0)),
            pl.BlockSpec((SUB, T_TILE, D), lambda i: (0, jnp.maximum(i - TILES_PER_STREAM, 0), 0)),
            _full_spec((1, D)), _mod_spec(), _mod_spec(), _full_spec((D, D_IN_EVEN))],
        out_specs=[_row_spec(D_IN_EVEN), _row_spec(D)],
        out_shape=[jax.ShapeDtypeStruct((N_TOK, D_IN_EVEN), F32), jax.ShapeDtypeStruct((N_TOK, D), F32)],
        scratch_shapes=[pltpu.VMEM((SLAB, TM, LANES), F32)],
        compiler_params=_params(1),
        name="first_in",
    )(x_prompt, x_sample, g, shift, scale, w_bf16)


RG_CHUNK = 256
RG_PAD = 16


def _rglru_kernel(u_ref, ga_ref, cw_ref, cb_ref, wg_ref, bg_ref, lam_ref, h0_ref,
                  y_ref, hl_ref, xpad, a0, b0, a1, b1, *, rows):
    steps = rows // SUB
    n_chunks = rows // RG_CHUNK

    xpad[0:RG_PAD, :] = jnp.zeros((RG_PAD, LANES), F32)
    xpad[RG_PAD + rows:RG_PAD + rows + RG_PAD, :] = jnp.zeros((RG_PAD, LANES), F32)

    def copy_chunk(c, carry):
        r = pl.multiple_of(c * RG_CHUNK, RG_CHUNK)
        xpad[pl.ds(RG_PAD + r, RG_CHUNK), :] = u_ref[pl.ds(r, RG_CHUNK), :]
        return carry

    lax.fori_loop(0, n_chunks, copy_chunk, 0)

    lam = lam_ref[...]
    softplus_neg = jnp.maximum(-lam, 0.0) + jnp.log1p(jnp.exp(-jnp.abs(lam)))
    a_refs = (a0, a1)
    b_refs = (b0, b1)

    def gate_chunk(c, carry):
        r = pl.multiple_of(c * RG_CHUNK, RG_CHUNK)
        xc = jnp.zeros((RG_CHUNK, LANES), F32) + cb_ref[...]
        for k in range(A_CONV):
            xc = xc + cw_ref[k:k + 1, :] * xpad[pl.ds(r + SUB * k, RG_CHUNK), :]
        pre = jnp.dot(xc.astype(BF16), wg_ref[...], preferred_element_type=F32) + bg_ref[...]
        for d in range(2):
            rg = jax.nn.sigmoid(pre[:, d * 256:d * 256 + LANES])
            ig = jax.nn.sigmoid(pre[:, d * 256 + LANES:(d + 1) * 256])
            log_a = (-A_C) * rg * softplus_neg[:, d * LANES:(d + 1) * LANES]
            a = jnp.exp(log_a)
            one_minus_a2 = -jnp.tanh(log_a) * (a * a + 1.0)
            a_refs[d][pl.ds(r, RG_CHUNK), :] = a
            b_refs[d][pl.ds(r, RG_CHUNK), :] = jnp.sqrt(one_minus_a2) * (ig * xc)
        return carry

    lax.fori_loop(0, n_chunks, gate_chunk, 0)

    def step(t, carry):
        hf, hb = carry
        rf = pl.multiple_of(t * SUB, SUB)
        rb = pl.multiple_of((steps - 1 - t) * SUB, SUB)
        hf = a0[pl.ds(rf, SUB), :] * hf + b0[pl.ds(rf, SUB), :]
        b0[pl.ds(rf, SUB), :] = hf
        hb = a1[pl.ds(rb, SUB), :] * hb + b1[pl.ds(rb, SUB), :]
        b1[pl.ds(rb, SUB), :] = hb
        return hf, hb

    hf, hb = lax.fori_loop(0, steps, step, (h0_ref[0], h0_ref[1]), unroll=8)
    hl_ref[0] = hf
    hl_ref[1] = hb

    def out_chunk(c, carry):
        r = pl.multiple_of(c * RG_CHUNK, RG_CHUNK)
        y = (b0[pl.ds(r, RG_CHUNK), :] + b1[pl.ds(r, RG_CHUNK), :]) * jax.nn.gelu(ga_ref[pl.ds(r, RG_CHUNK), :])
        y_ref[pl.ds(r, RG_CHUNK), :] = y.astype(BF16)
        return carry

    lax.fori_loop(0, n_chunks, out_chunk, 0)


def _rglru_call(z, cw, cb, wg, bg, lam, h0, *, rows, groups, row_block0):
    n_cb = D_A // LANES
    in_specs = [
        pl.BlockSpec((rows, LANES), lambda g, j: (row_block0 + g, j)),
        pl.BlockSpec((rows, LANES), lambda g, j: (row_block0 + g, n_cb + j)),
        pl.BlockSpec((A_CONV, LANES), lambda g, j: (0, j)),
        pl.BlockSpec((1, LANES), lambda g, j: (0, j)),
        pl.BlockSpec((None, LANES, 4 * LANES), lambda g, j: (j, 0, 0)),
        pl.BlockSpec((None, 1, 4 * LANES), lambda g, j: (j, 0, 0)),
        pl.BlockSpec((None, 1, 2 * LANES), lambda g, j: (j, 0, 0)),
        pl.BlockSpec((None, 2, SUB, LANES), lambda g, j: (g, 0, 0, j)),
    ]
    return pl.pallas_call(
        functools.partial(_rglru_kernel, rows=rows),
        grid=(groups, n_cb),
        in_specs=in_specs,
        out_specs=[
            pl.BlockSpec((rows, LANES), lambda g, j: (g, j)),
            pl.BlockSpec((None, 2, SUB, LANES), lambda g, j: (g, 0, 0, j)),
        ],
        out_shape=[
            jax.ShapeDtypeStruct((groups * rows, D_A), BF16),
            jax.ShapeDtypeStruct((groups, 2, SUB, D_A), F32),
        ],
        scratch_shapes=[pltpu.VMEM((rows + 2 * RG_PAD, LANES), F32)] + [pltpu.VMEM((rows, LANES), F32)] * 4,
        compiler_params=_params(2),
        name=f"rglru_{rows}",
    )(z, z, cw, cb, wg, bg, lam, h0)


POOL_CHUNK = 256
POOL_PAD = 64


def _pool_kernel(u_ref, w_ref, s_ref, y_ref, xpad, *, rows):
    steps = rows // SUB
    n_chunks = rows // POOL_CHUNK
    j = pl.program_id(1)

    xpad[0:POOL_PAD, :] = jnp.zeros((POOL_PAD, LANES), F32)
    xpad[POOL_PAD + rows:POOL_PAD + rows + POOL_PAD, :] = jnp.zeros((POOL_PAD, LANES), F32)

    def copy_chunk(c, carry):
        r = pl.multiple_of(c * POOL_CHUNK, POOL_CHUNK)
        xpad[pl.ds(POOL_PAD + r, POOL_CHUNK), :] = u_ref[pl.ds(r, POOL_CHUNK), :]
        return carry

    lax.fori_loop(0, n_chunks, copy_chunk, 0)

    for gi, win in enumerate(POOL_WINDOWS):
        half = win // 2

        @pl.when(j == gi)
        def _(half=half, win=win):
            def chunk(c, carry):
                r = pl.multiple_of(c * POOL_CHUNK, POOL_CHUNK)
                acc = xpad[pl.ds(POOL_PAD + r - SUB * half, POOL_CHUNK), :]
                for s in range(1, win):
                    acc = acc + xpad[pl.ds(POOL_PAD + r + SUB * (s - half), POOL_CHUNK), :]
                t = (r + lax.broadcasted_iota(jnp.int32, (POOL_CHUNK, 1), 0)) // SUB
                cnt = jnp.minimum(t + half, steps) - jnp.maximum(t - half, 0)
                dlt = acc / cnt.astype(F32) - xpad[pl.ds(POOL_PAD + r, POOL_CHUNK), :]
                y = jnp.dot(dlt.astype(BF16), w_ref[...], preferred_element_type=F32) * s_ref[...]
                y_ref[pl.ds(r, POOL_CHUNK), :] = y.astype(BF16)
                return carry

            lax.fori_loop(0, n_chunks, chunk, 0)


def _pool_call(z, w_pool_bf16, s_pool, *, rows, groups, row_block0):
    n_cb = D_B // LANES
    col0 = 2 * D_A // LANES
    in_specs = [
        pl.BlockSpec((rows, LANES), lambda g, j: (row_block0 + g, col0 + j)),
        pl.BlockSpec((None, B_GS, B_GS), lambda g, j: (j, 0, 0)),
        pl.BlockSpec((1, LANES), lambda g, j: (0, j)),
    ]
    return pl.pallas_call(
        functools.partial(_pool_kernel, rows=rows),
        grid=(groups, n_cb),
        in_specs=in_specs,
        out_specs=pl.BlockSpec((rows, LANES), lambda g, j: (g, j)),
        out_shape=jax.ShapeDtypeStruct((groups * rows, D_B), BF16),
        scratch_shapes=[pltpu.VMEM((rows + 2 * POOL_PAD, LANES), F32)],
        compiler_params=_params(2),
        name=f"pool_{rows}",
    )(z, w_pool_bf16, s_pool)


def _stream_specs(width):
    prompt = pl.BlockSpec((TM, width), lambda i, *_: (jnp.minimum(i, TILES_PER_STREAM - 1), 0))
    sample = pl.BlockSpec((TM, width), lambda i, *_: (jnp.maximum(i - TILES_PER_STREAM, 0), 0))
    return [prompt, sample]


def _pick_stream(prompt_ref, sample_ref):
    return jnp.where(pl.program_id(0) >= TILES_PER_STREAM, sample_ref[...], prompt_ref[...])


def _even_tail_kernel(x_ref, yap_ref, yas_ref, ybp_ref, ybs_ref, wa_ref, wb_ref, gate1_ref,
                      g2_ref, sh_ref, sc_ref, gate2_ref, w1_ref, w3_ref, w2_ref, o_ref):
    y = jnp.dot(_pick_stream(yap_ref, yas_ref), wa_ref[...], preferred_element_type=F32)
    y = y + jnp.dot(_pick_stream(ybp_ref, ybs_ref), wb_ref[...], preferred_element_type=F32)
    x1 = _gated_add(x_ref[...], gate1_ref[...], y)
    h = _modulate(_rms(x1, g2_ref[...]), sh_ref[...], sc_ref[...]).astype(BF16)
    a = jnp.dot(h, w1_ref[...], preferred_element_type=F32)
    b = jnp.dot(h, w3_ref[...], preferred_element_type=F32)
    u = (a * jax.nn.sigmoid(a) * b).astype(BF16)
    f = jnp.dot(u, w2_ref[...], preferred_element_type=F32)
    o_ref[...] = _gated_add(x1, gate2_ref[...], f)


def _even_tail(x, ya, yb, wa, wb, gate1, g2, shift2, scale2, gate2, w1, w3, w2, li):
    half = D // 2
    resident = pl.Buffered(1)
    return pl.pallas_call(
        _even_tail_kernel,
        grid=(N_TOK // TM,),
        in_specs=[_row_spec(D)] + _stream_specs(half) + _stream_specs(half) +
                 [pl.BlockSpec((half, D), lambda i: (0, 0), pipeline_mode=resident),
                  pl.BlockSpec((half, D), lambda i: (0, 0), pipeline_mode=resident),
                  _mod_spec(), _full_spec((1, D)), _mod_spec(), _mod_spec(), _mod_spec(),
                  pl.BlockSpec((None, D, D_FF), lambda i: (li, 0, 0), pipeline_mode=resident),
                  pl.BlockSpec((None, D, D_FF), lambda i: (li, 0, 0), pipeline_mode=resident),
                  pl.BlockSpec((None, D_FF, D), lambda i: (li, 0, 0), pipeline_mode=resident)],
        out_specs=_row_spec(D),
        out_shape=jax.ShapeDtypeStruct((N_TOK, D), F32),
        compiler_params=_params(1),
        name="even_tail",
    )(x, *ya, *yb, wa, wb, gate1, g2, shift2, scale2, gate2, w1, w3, w2)


def _head_rms(x, ones_bd, g):
    sq = x * x
    hi = sq.astype(BF16)
    lo = (sq - hi.astype(F32)).astype(BF16)
    ms = jnp.dot(hi, ones_bd, preferred_element_type=F32) + jnp.dot(lo, ones_bd, preferred_element_type=F32)
    return x * lax.rsqrt(ms + EPS) * g


def _rope(x, cos, sin_signed):
    w = x.shape[-1]
    lane = lax.broadcasted_iota(jnp.int32, x.shape, 1)
    first = (lane % 32) < 16
    partner = jnp.where(first, pltpu.roll(x, w - 16, 1), pltpu.roll(x, 16, 1))
    return x * cos + partner * sin_signed


Q_BLOCKS = D_ATT // LANES


def _odd_in_kernel(x_ref, g_ref, sh_ref, sc_ref, w_ref, bdq_ref, bdk_ref, qg_ref, kg_ref, cos_ref, sin_ref,
                   glu_ref, q0_ref, q1_ref, q2_ref, q3_ref, k_ref, v_ref):
    i = pl.program_id(0)
    h = _modulate(_rms(x_ref[...], g_ref[...]), sh_ref[...], sc_ref[...])
    z = jnp.dot(h.astype(BF16), w_ref[...], preferred_element_type=F32)
    glu_ref[...] = z[:, :D_C] * jax.nn.sigmoid(z[:, D_C:2 * D_C])
    o1 = 2 * D_C
    o2 = o1 + D_ATT
    o3 = o2 + D_KV
    q = _head_rms(z[:, o1:o2], bdq_ref[...], qg_ref[...])
    k = _head_rms(z[:, o2:o3], bdk_ref[...], kg_ref[...])
    cos = cos_ref[...]
    sin = sin_ref[...]
    is_sample = i >= TILES_PER_STREAM
    q_r = _rope(q, jnp.concatenate([cos] * 4, axis=1), jnp.concatenate([sin] * 4, axis=1))
    k_r = _rope(k, cos, sin)
    q = jnp.where(is_sample, q_r, q)
    for c, q_ref in enumerate((q0_ref, q1_ref, q2_ref, q3_ref)):
        q_ref[...] = q[:, c * LANES:(c + 1) * LANES]
    k_ref[...] = jnp.where(is_sample, k_r, k)
    v_ref[...] = z[:, o3:]


def _odd_in(x, g, shift, scale, w_bf16, bdq, bdk, qg, kg, cos_t, sin_t):
    rope_spec = pl.BlockSpec((TM, D_KV), lambda i: (jnp.maximum(i - TILES_PER_STREAM, 0), 0))
    return pl.pallas_call(
        _odd_in_kernel,
        grid=(N_TOK // TM,),
        in_specs=[_row_spec(D), _full_spec((1, D)), _mod_spec(), _mod_spec(), _full_spec((D, D_IN_ODD)),
                  _full_spec((D_ATT, D_ATT)), _full_spec((D_KV, D_KV)), _full_spec((1, D_ATT)),
                  _full_spec((1, D_KV)), rope_spec, rope_spec],
        out_specs=[_row_spec(D_C)] + [_row_spec(LANES)] * (Q_BLOCKS + 2),
        out_shape=[jax.ShapeDtypeStruct((N_TOK, D_C), F32)] +
                  [jax.ShapeDtypeStruct((N_TOK, LANES), F32)] * (Q_BLOCKS + 2),
        compiler_params=_params(1),
        name="odd_in",
    )(x, g, shift, scale, w_bf16, bdq, bdk, qg, kg, cos_t, sin_t)


def _attend(q_refs, o_refs, q_rows, k_all, v_all):
    scale = HEAD_DIM ** -0.5
    heads_per_block = LANES // HEAD_DIM
    group = N_Q_HEADS // N_KV_HEADS
    kv = []
    for h in range(N_KV_HEADS):
        kv.append((k_all[:, h * HEAD_DIM:(h + 1) * HEAD_DIM].astype(BF16),
                   v_all[:, h * HEAD_DIM:(h + 1) * HEAD_DIM].astype(BF16)))
    for c in range(Q_BLOCKS):
        qc = (q_refs[c][q_rows, :] * scale).astype(BF16)
        outs = []
        for sub in range(heads_per_block):
            kh, vh = kv[(c * heads_per_block + sub) // group]
            qh = qc[:, sub * HEAD_DIM:(sub + 1) * HEAD_DIM]
            s = lax.dot_general(qh, kh, (((1,), (1,)), ((), ())), preferred_element_type=F32)
            m = jnp.max(s, axis=-1, keepdims=True)
            p = jnp.exp(s - m)
            l = jnp.sum(p, axis=-1, keepdims=True)
            outs.append(jnp.dot(p.astype(BF16), vh, preferred_element_type=F32) / l)
        o_refs[c][q_rows, :] = jnp.concatenate(outs, axis=1)


def _attn_prompt_kernel(*refs):
    q_refs, (k_ref, v_ref), o_refs = refs[:Q_BLOCKS], refs[Q_BLOCKS:Q_BLOCKS + 2], refs[Q_BLOCKS + 2:]
    rows = pl.ds(pl.program_id(1), SEQ, stride=SUB)
    _attend(q_refs, o_refs, rows, k_ref[rows, :], v_ref[rows, :])


def _attn_sample_kernel(*refs):
    q_refs, (k_ref, v_ref, ck_ref, cv_ref), o_refs = refs[:Q_BLOCKS], refs[Q_BLOCKS:Q_BLOCKS + 4], refs[Q_BLOCKS + 4:]
    b = pl.program_id(1)
    kv_rows = pl.ds(b, DEC_SEQ, stride=SUB)
    k_all = jnp.concatenate([ck_ref[...], k_ref[kv_rows, :]], axis=0)
    v_all = jnp.concatenate([cv_ref[...], v_ref[kv_rows, :]], axis=0)
    _attend(q_refs, o_refs, pl.ds(b, SEQ, stride=SUB), k_all, v_all)


def _attention(qs, k, v, cache_k_l, cache_v_l):
    chunk = pl.BlockSpec((R_PROMPT, LANES), lambda g, b: (g, 0))
    out_shape = [jax.ShapeDtypeStruct((N_PROMPT, LANES), F32)] * Q_BLOCKS
    att_p = pl.pallas_call(
        _attn_prompt_kernel,
        grid=(P_GROUPS, SUB),
        in_specs=[chunk] * (Q_BLOCKS + 2),
        out_specs=[chunk] * Q_BLOCKS,
        out_shape=out_shape,
        compiler_params=_params(2),
        name="attn_prompt",
    )(*qs, k, v)
    q_chunks = DEC_SEQ // SEQ
    q_chunk = pl.BlockSpec((R_PROMPT, LANES), lambda c, b: (P_GROUPS + c, 0))
    kv_all = pl.BlockSpec((R_SAMPLE, LANES), lambda c, b: (1, 0))
    cache = pl.BlockSpec((None, PAST_LEN, D_KV), lambda c, b: (b, 0, 0))
    att_s = pl.pallas_call(
        _attn_sample_kernel,
        grid=(q_chunks, DEC_BATCH),
        in_specs=[q_chunk] * Q_BLOCKS + [kv_all, kv_all, cache, cache],
        out_specs=[pl.BlockSpec((R_PROMPT, LANES), lambda c, b: (c, 0))] * Q_BLOCKS,
        out_shape=out_shape,
        compiler_params=_params(2),
        name="attn_sample",
    )(*qs, k, v, cache_k_l, cache_v_l)
    return att_p, att_s


CV_CHUNK = 64
CV_PAD = (C_CONV // 2) * SUB


def _conv_kernel(u_ref, w_ref, b_ref, y_ref, xpad, *, rows):
    n_copy = rows // 256
    xpad[0:CV_PAD, :] = jnp.zeros((CV_PAD, LANES), F32)
    xpad[CV_PAD + rows:CV_PAD + rows + CV_PAD, :] = jnp.zeros((CV_PAD, LANES), F32)

    def copy_chunk(c, carry):
        r = pl.multiple_of(c * 256, 256)
        xpad[pl.ds(CV_PAD + r, 256), :] = u_ref[pl.ds(r, 256), :]
        return carry

    lax.fori_loop(0, n_copy, copy_chunk, 0)

    def chunk(c, carry):
        r = pl.multiple_of(c * CV_CHUNK, CV_CHUNK)
        acc = jnp.zeros((CV_CHUNK, LANES), F32) + b_ref[...]
        for k in range(C_CONV):
            acc = acc + w_ref[k:k + 1, :] * xpad[pl.ds(r + SUB * k, CV_CHUNK), :]
        y_ref[pl.ds(r, CV_CHUNK), :] = acc
        return carry

    lax.fori_loop(0, rows // CV_CHUNK, chunk, 0)


def _conv_call(glu, w, b, *, rows, groups, row_block0):
    n_cb = D_C // LANES
    in_specs = [
        pl.BlockSpec((rows, LANES), lambda g, j: (row_block0 + g, j)),
        pl.BlockSpec((C_CONV, LANES), lambda g, j: (0, j)),
        pl.BlockSpec((1, LANES), lambda g, j: (0, j)),
    ]
    return pl.pallas_call(
        functools.partial(_conv_kernel, rows=rows),
        grid=(groups, n_cb),
        in_specs=in_specs,
        out_specs=pl.BlockSpec((rows, LANES), lambda g, j: (g, j)),
        out_shape=jax.ShapeDtypeStruct((groups * rows, D_C), F32),
        scratch_shapes=[pltpu.VMEM((rows + 2 * CV_PAD, LANES), F32)],
        compiler_params=_params(2),
        name=f"conv_{rows}",
    )(glu, w, b)


def _odd_out_kernel(x_ref, hcp_ref, hcs_ref, *refs):
    att_refs, refs = refs[:2 * Q_BLOCKS], refs[2 * Q_BLOCKS:]
    (lng_ref, lnb_ref, wc_ref, wa_ref, gate_ref, g2_ref, sh_ref, sc_ref, wrh_ref, wrl_ref, br_ref,
     x1_ref, h2_ref, route_ref) = refs
    att = jnp.concatenate([_pick_stream(att_refs[c], att_refs[Q_BLOCKS + c]) for c in range(Q_BLOCKS)], axis=1)
    hc = _pick_stream(hcp_ref, hcs_ref)
    mu = jnp.mean(hc, axis=-1, keepdims=True)
    xc = hc - mu
    var = jnp.mean(xc * xc, axis=-1, keepdims=True)
    ln = xc * lax.rsqrt(var + EPS) * lng_ref[...] + lnb_ref[...]
    yc = (ln * jax.nn.sigmoid(ln)).astype(BF16)
    y = jnp.dot(yc, wc_ref[...], preferred_element_type=F32)
    y = y + jnp.dot(att.astype(BF16), wa_ref[...], preferred_element_type=F32)
    x1 = _gated_add(x_ref[...], gate_ref[...], y)
    x1_ref[...] = x1
    h2 = _modulate(_rms(x1, g2_ref[...]), sh_ref[...], sc_ref[...])
    for j in range(SLAB):
        h2_ref[pl.ds(j, TM, stride=SLAB), :] = h2[:, j * LANES:(j + 1) * LANES]
    h2_hi = h2.astype(BF16)
    h2_lo = (h2 - h2_hi.astype(F32)).astype(BF16)
    logits = (jnp.dot(h2_hi, wrh_ref[...], preferred_element_type=F32)
              + jnp.dot(h2_lo, wrh_ref[...], preferred_element_type=F32)
              + jnp.dot(h2_hi, wrl_ref[...], preferred_element_type=F32)) + br_ref[...]
    lane = lax.broadcasted_iota(jnp.int32, logits.shape, 1).astype(F32)
    lg = jnp.where(lane < N_EXPERTS, logits, NEG_BIG)
    m1 = jnp.max(lg, axis=-1, keepdims=True)
    i1 = jnp.min(jnp.where(lg == m1, lane, float(LANES)), axis=-1, keepdims=True)
    lg2 = jnp.where(lane == i1, NEG_BIG, lg)
    m2 = jnp.max(lg2, axis=-1, keepdims=True)
    i2 = jnp.min(jnp.where(lg2 == m2, lane, float(LANES)), axis=-1, keepdims=True)
    e = jnp.exp(m2 - m1)
    den = 1.0 + e
    route = jnp.where(lane == 0.0, i1, jnp.where(lane == 1.0, i2,
                      jnp.where(lane == 2.0, 1.0 / den, jnp.where(lane == 3.0, e / den, 0.0))))
    route_ref[...] = route


def _odd_out(x, hc, att, lng, lnb, wc, wa, gate, g2, shift2, scale2, wr_hi, wr_lo, br_pad):
    half = D // 2
    att_specs = [_stream_specs(LANES)[0]] * Q_BLOCKS + [_stream_specs(LANES)[1]] * Q_BLOCKS
    return pl.pallas_call(
        _odd_out_kernel,
        grid=(N_TOK // TM,),
        in_specs=[_row_spec(D)] + _stream_specs(half) + att_specs +
                 [_full_spec((1, half)), _full_spec((1, half)),
                  _full_spec((half, D)), _full_spec((half, D)), _mod_spec(),
                  _full_spec((1, D)), _mod_spec(), _mod_spec(),
                  _full_spec((D, LANES)), _full_spec((D, LANES)), _full_spec((1, LANES))],
        out_specs=[_row_spec(D), pl.BlockSpec((TM * SLAB, LANES), lambda i: (i, 0)), _row_spec(LANES)],
        out_shape=[jax.ShapeDtypeStruct((N_TOK, D), F32), jax.ShapeDtypeStruct((N_TOK * SLAB, LANES), F32),
                   jax.ShapeDtypeStruct((N_TOK, LANES), F32)],
        compiler_params=_params(1),
        name="odd_out",
    )(x, *hc, *att[0], *att[1], lng, lnb, wc, wa, gate, g2, shift2, scale2, wr_hi, wr_lo, br_pad)


def _moe_kernel(te_ref, nv_ref, nu_ref, jcur_ref, jnext_ref, h2_hbm, w1_ref, w3_ref, w2_ref, y2_hbm,
                xbuf, obuf, gsem, ssem):
    del te_ref
    i = pl.program_id(0)
    n_used = nu_ref[0]
    slot = i % 2
    other = 1 - slot

    def rows_of(r):
        return pl.ds(pl.multiple_of(r * SLAB, SLAB), SLAB)

    def gather_one(j_ref, s, r, priority):
        tok = lax.shift_right_logical(j_ref[0, r], 1)
        pltpu.make_async_copy(h2_hbm.at[rows_of(tok), :], xbuf.at[s, rows_of(r), :],
                              gsem.at[s]).start(priority=priority)

    def scatter_one(s, r, priority):
        pltpu.make_async_copy(obuf.at[s, rows_of(r), :], y2_hbm.at[rows_of(jcur_ref[0, r]), :],
                              ssem.at[s]).start(priority=priority)

    def issue(lo, hi, one):
        n = jnp.maximum(hi - lo, 0)
        full = lax.shift_right_logical(n, DMA_UNROLL.bit_length() - 1)

        def body(c, carry):
            for k in range(DMA_UNROLL):
                one(lo + c * DMA_UNROLL + k, k % 2)
            return carry

        lax.fori_loop(0, full, body, 0)

        def tail(r, carry):
            one(r, 0)
            return carry

        lax.fori_loop(lo + full * DMA_UNROLL, lo + n, tail, 0)

    def wait_gather(s):
        pltpu.make_async_copy(h2_hbm.at[pl.ds(0, MOE_TM * SLAB), :], xbuf.at[s], gsem.at[s]).wait()

    def wait_scatter(s, n):
        @pl.when(n > 0)
        def _():
            pltpu.make_async_copy(obuf.at[s, pl.ds(0, n * SLAB), :], y2_hbm.at[pl.ds(0, n * SLAB), :],
                                  ssem.at[s]).wait()

    @pl.when(i == 0)
    def _():
        issue(0, MOE_TM, lambda r, p: gather_one(jcur_ref, 0, r, p))

    @pl.when(i < n_used)
    def _():
        wait_gather(slot)

        @pl.when(i + 1 < n_used)
        def _():
            issue(0, MOE_TM, lambda r, p: gather_one(jnext_ref, other, r, p))

        @pl.when(i >= 2)
        def _():
            wait_scatter(slot, nv_ref[jnp.maximum(i - 2, 0)])

        x = jnp.concatenate([xbuf[slot, pl.ds(j, MOE_TM, stride=SLAB), :] for j in range(SLAB)], axis=1)
        h = x.astype(BF16)
        a = jnp.dot(h, w1_ref[...], preferred_element_type=F32)
        b = jnp.dot(h, w3_ref[...], preferred_element_type=F32)
        u = (a * jax.nn.sigmoid(a) * b).astype(BF16)
        y = jnp.dot(u, w2_ref[...], preferred_element_type=F32)
        for j in range(SLAB):
            obuf[slot, pl.ds(j, MOE_TM, stride=SLAB), :] = y[:, j * LANES:(j + 1) * LANES]
        issue(0, nv_ref[i], lambda r, p: scatter_one(slot, r, p))

        @pl.when(i == n_used - 1)
        def _():
            @pl.when(i >= 1)
            def _():
                wait_scatter(other, nv_ref[jnp.maximum(i - 1, 0)])

            wait_scatter(slot, nv_ref[i])


def _moe_experts(tile_expert, tile_valid, n_used, jtab, h2, w1, w3, w2, li):
    jtab3 = jtab.reshape(MOE_TILES, 1, MOE_TM)
    last = MOE_TILES - 1

    def table(shift):
        return pl.BlockSpec((None, 1, MOE_TM), lambda i, *_: (jnp.clip(i + shift, 0, last), 0, 0),
                            memory_space=pltpu.SMEM)

    def weight(rows, cols):
        return pl.BlockSpec((None, None, rows, cols), lambda i, te, *_: (li, te[i], 0, 0))

    grid_spec = pltpu.PrefetchScalarGridSpec(
        num_scalar_prefetch=3,
        grid=(MOE_TILES,),
        in_specs=[table(0), table(1), pl.BlockSpec(memory_space=pl.ANY),
                  weight(D, D_FF_E), weight(D, D_FF_E), weight(D_FF_E, D)],
        out_specs=pl.BlockSpec(memory_space=pl.ANY),
        scratch_shapes=[
            pltpu.VMEM((2, MOE_TM * SLAB, LANES), F32),
            pltpu.VMEM((2, MOE_TM * SLAB, LANES), F32),
            pltpu.SemaphoreType.DMA((2,)),
            pltpu.SemaphoreType.DMA((2,)),
        ],
    )
    return pl.pallas_call(
        _moe_kernel,
        grid_spec=grid_spec,
        out_shape=jax.ShapeDtypeStruct((2 * N_TOK * SLAB, LANES), F32),
        compiler_params=_params(1),
        name="moe_experts",
    )(tile_expert, tile_valid, n_used, jtab3, jtab3, h2, w1, w3, w2)


def _route_plan(route):
    ids = route[:, 0:2].astype(jnp.int32).reshape(-1)
    onehot = (ids[:, None] == jnp.arange(N_EXPERTS, dtype=jnp.int32)[None, :]).astype(jnp.int32)
    csum = jnp.cumsum(onehot, axis=0)
    rank = jnp.sum(csum * onehot, axis=1) - 1
    counts = csum[-1]
    tiles = (counts + MOE_TM - 1) // MOE_TM
    tile_end = jnp.cumsum(tiles)
    group_start = (tile_end - tiles) * MOE_TM
    pos = jnp.sum(onehot * group_start[None, :], axis=1) + rank
    jtab = jnp.zeros((MOE_ROWS,), jnp.int32).at[pos].set(jnp.arange(2 * N_TOK, dtype=jnp.int32))
    tile_ids = jnp.arange(MOE_TILES, dtype=jnp.int32)
    tile_expert = jnp.minimum(jnp.sum((tile_ids[:, None] >= tile_end[None, :]).astype(jnp.int32), axis=1),
                              N_EXPERTS - 1)
    rows_before = (tile_ids - (tile_end - tiles)[tile_expert]) * MOE_TM
    tile_valid = jnp.clip(counts[tile_expert] - rows_before, 0, MOE_TM)
    tile_valid = jnp.where(tile_ids < tile_end[-1], tile_valid, 0).astype(jnp.int32)
    n_used = tile_end[-1:].astype(jnp.int32)
    return jtab, tile_expert, tile_valid, n_used


def _moe_combined(x_ref, y_ref, route_ref, gate_ref):
    g0 = route_ref[:, 2:3]
    g1 = route_ref[:, 3:4]
    parts = [g0 * y_ref[pl.ds(j, TM, stride=2 * SLAB), :] + g1 * y_ref[pl.ds(SLAB + j, TM, stride=2 * SLAB), :]
             for j in range(SLAB)]
    return _gated_add(x_ref[...], gate_ref[...], jnp.concatenate(parts, axis=1))


def _moe_pending_specs(tile0=0, stream=None):
    gate = _mod_spec() if stream is None else pl.BlockSpec((None, SUB, D), lambda i: (stream, 0, 0))
    return [pl.BlockSpec((TM, D), lambda i: (tile0 + i, 0)),
            pl.BlockSpec((TM * 2 * SLAB, LANES), lambda i: (tile0 + i, 0)),
            pl.BlockSpec((TM, LANES), lambda i: (tile0 + i, 0)), gate]


def _even_in_moe_kernel(x1_ref, y_ref, route_ref, gate_ref, g_ref, sh_ref, sc_ref, w_ref, z_ref, x_ref):
    x = _moe_combined(x1_ref, y_ref, route_ref, gate_ref)
    x_ref[...] = x
    h = _modulate(_rms(x, g_ref[...]), sh_ref[...], sc_ref[...])
    z_ref[...] = jnp.dot(h.astype(BF16), w_ref[...], preferred_element_type=F32)


def _even_in_moe(pending, g, shift, scale, w_bf16):
    return pl.pallas_call(
        _even_in_moe_kernel,
        grid=(N_TOK // TM,),
        in_specs=_moe_pending_specs() + [_full_spec((1, D)), _mod_spec(), _mod_spec(), _full_spec((D, D_IN_EVEN))],
        out_specs=[_row_spec(D_IN_EVEN), _row_spec(D)],
        out_shape=[jax.ShapeDtypeStruct((N_TOK, D_IN_EVEN), F32), jax.ShapeDtypeStruct((N_TOK, D), F32)],
        compiler_params=_params(1),
        name="even_in_moe",
    )(*pending, g, shift, scale, w_bf16)


def _final_kernel(x1_ref, y_ref, route_ref, gate_ref, g_ref, o_ref, cols):
    y = _rms(_moe_combined(x1_ref, y_ref, route_ref, gate_ref), g_ref[...])
    for c in range(SLAB):
        cols[c] = y[:, c * LANES:(c + 1) * LANES]
    for b in range(SUB):
        for c in range(SLAB):
            o_ref[b, :, c * LANES:(c + 1) * LANES] = cols[c, pl.ds(b, T_TILE, stride=SUB), :]


def _final_norm(pending, g, *, batch, steps, stream):
    t_tiles = steps // T_TILE
    return pl.pallas_call(
        _final_kernel,
        grid=(batch * steps // TM,),
        in_specs=_moe_pending_specs(stream * TILES_PER_STREAM, stream) + [_full_spec((1, D))],
        out_specs=pl.BlockSpec((SUB, T_TILE, D), lambda i: (i // t_tiles, i % t_tiles, 0)),
        out_shape=jax.ShapeDtypeStruct((batch, steps, D), F32),
        scratch_shapes=[pltpu.VMEM((SLAB, TM, LANES), F32)],
        compiler_params=_params(1),
        name=f"final_norm_{steps}",
    )(*pending, g)


def _block_diag_gates(w_r, w_i, b_r, b_i):
    def bd(w):
        w4 = w.reshape(4, 2, A_BS, A_BS)
        z = jnp.zeros((4, A_BS, A_BS), w.dtype)
        top = jnp.concatenate([w4[:, 0], z], axis=2)
        bot = jnp.concatenate([z, w4[:, 1]], axis=2)
        return jnp.concatenate([top, bot], axis=1)

    wg = jnp.concatenate([bd(w_r[0]), bd(w_i[0]), bd(w_r[1]), bd(w_i[1])], axis=2)
    bg = jnp.concatenate([b_r[0].reshape(4, 1, LANES), b_i[0].reshape(4, 1, LANES),
                          b_r[1].reshape(4, 1, LANES), b_i[1].reshape(4, 1, LANES)], axis=2)
    return wg.astype(BF16), bg


def _head_mean_matrix(width):
    idx = jnp.arange(width) // HEAD_DIM
    return ((idx[:, None] == idx[None, :]).astype(F32) / HEAD_DIM).astype(BF16)


def _rope_tables():
    pos = jnp.arange(DEC_SEQ)
    row = (pos // GRID_W).astype(F32)
    col = (pos % GRID_W).astype(F32)
    n_freq = HEAD_DIM // 4
    inv = ROPE_THETA ** (-jnp.arange(n_freq, dtype=F32) / n_freq)
    ang = jnp.stack([row[:, None] * inv, col[:, None] * inv], axis=1)
    cos = jnp.cos(ang)
    sin = jnp.sin(ang)
    cos_h = jnp.stack([cos, cos], axis=2).reshape(DEC_SEQ, HEAD_DIM)
    sin_h = jnp.stack([-sin, sin], axis=2).reshape(DEC_SEQ, HEAD_DIM)
    cos_t = jnp.tile(cos_h, (1, 2))
    sin_t = jnp.tile(sin_h, (1, 2))
    cos_t = jnp.broadcast_to(cos_t[:, None, :], (DEC_SEQ, SUB, D_KV)).reshape(N_SAMPLE, D_KV)
    sin_t = jnp.broadcast_to(sin_t[:, None, :], (DEC_SEQ, SUB, D_KV)).reshape(N_SAMPLE, D_KV)
    return cos_t, sin_t


def _prompt_to_batch_major(a):
    w = a.shape[-1]
    return a[:N_PROMPT].reshape(P_GROUPS, SEQ, SUB, w).transpose(0, 2, 1, 3).reshape(BATCH, SEQ, w)


def kernel(x_prompt, x_sample, c, state_rglru, cache_k, cache_v, c_ctx, w_mod, b_mod, norm1, norm2, ev_w_in, a_conv_w, a_conv_b, a_w_r, a_b_r, a_w_i, a_b_i, a_lam, b_w_pool, b_scale, ev_w_out, od_w_in, c_conv_w, c_conv_b, c_ln_g, c_ln_b, q_norm, k_norm, od_w_out, ff_w1, ff_w3, ff_w2, moe_w_router, moe_b_router, moe_w1, moe_w3, moe_w2, norm_f):
    cond16 = jnp.concatenate([c_ctx[None, :], c, jnp.zeros((16 - 1 - DEC_BATCH, D), F32)], axis=0)
    mods = _ada_params(cond16, w_mod, b_mod)
    mods = jnp.stack([jnp.broadcast_to(mods[:, :, 0:1], (DEPTH, 6, SUB, D)), mods[:, :, 1:1 + SUB]], axis=2)

    cos_t, sin_t = _rope_tables()
    bdq = _head_mean_matrix(D_ATT)
    bdk = _head_mean_matrix(D_KV)
    cache_k4 = cache_k.reshape(DEC_BATCH, DEPTH // 2, PAST_LEN, D_KV)
    cache_v4 = cache_v.reshape(DEC_BATCH, DEPTH // 2, PAST_LEN, D_KV)
    ff_w1b, ff_w3b, ff_w2b = ff_w1.astype(BF16), ff_w3.astype(BF16), ff_w2.astype(BF16)
    moe_w1b, moe_w3b, moe_w2b = moe_w1.astype(BF16), moe_w3.astype(BF16), moe_w2.astype(BF16)

    x = pending = None
    new_states, new_k, new_v = [], [], []
    for layer in range(DEPTH):
        li = layer // 2
        shift1, scale1, gate1, shift2, scale2, gate2 = [mods[layer, j] for j in range(6)]
        g1 = norm1[layer].reshape(1, D)
        g2 = norm2[layer].reshape(1, D)
        if layer % 2 == 0:
            if layer == 0:
                z, x = _first_in(x_prompt, x_sample, g1, shift1, scale1, ev_w_in[li].astype(BF16))
            else:
                z, x = _even_in_moe(pending, g1, shift1, scale1, ev_w_in[li].astype(BF16))
            wg, bg = _block_diag_gates(a_w_r[li], a_w_i[li], a_b_r[li], a_b_i[li])
            lam = jnp.concatenate([a_lam[li, 0].reshape(4, 1, LANES), a_lam[li, 1].reshape(4, 1, LANES)], axis=2)
            cb = a_conv_b[li].reshape(1, D_A)
            h0_p = jnp.zeros((P_GROUPS, 2, SUB, D_A), F32)
            h0_s = state_rglru[:, li].transpose(1, 0, 2)[None]
            ya_p, h_last = _rglru_call(z, a_conv_w[li], cb, wg, bg, lam, h0_p,
                                       rows=R_PROMPT, groups=P_GROUPS, row_block0=0)
            ya_s, _ = _rglru_call(z, a_conv_w[li], cb, wg, bg, lam, h0_s,
                                  rows=R_SAMPLE, groups=1, row_block0=1)
            wp = b_w_pool[li].astype(BF16)
            sp = b_scale[li].reshape(1, D_B)
            yb_p = _pool_call(z, wp, sp, rows=R_PROMPT, groups=P_GROUPS, row_block0=0)
            yb_s = _pool_call(z, wp, sp, rows=R_SAMPLE, groups=1, row_block0=1)
            w_out = ev_w_out[li].astype(BF16)
            x = _even_tail(x, (ya_p, ya_s), (yb_p, yb_s), w_out[:D_A], w_out[D_A:], gate1,
                           g2, shift2, scale2, gate2, ff_w1b, ff_w3b, ff_w2b, li)
            new_states.append(h_last.transpose(0, 2, 1, 3).reshape(BATCH, 2, D_A))
        else:
            qg = jnp.tile(q_norm[li], N_Q_HEADS).reshape(1, D_ATT)
            kg = jnp.tile(k_norm[li], N_KV_HEADS).reshape(1, D_KV)
            glu, *qs, k, v = _odd_in(x, g1, shift1, scale1, od_w_in[li].astype(BF16), bdq, bdk, qg, kg,
                                     cos_t, sin_t)
            att = _attention(qs, k, v, cache_k4[:, li], cache_v4[:, li])
            cw = c_conv_w[li]
            cb = c_conv_b[li].reshape(1, D_C)
            hc = (_conv_call(glu, cw, cb, rows=R_PROMPT, groups=P_GROUPS, row_block0=0),
                  _conv_call(glu, cw, cb, rows=R_SAMPLE, groups=1, row_block0=1))
            w_out = od_w_out[li].astype(BF16)
            wr = jnp.zeros((D, LANES), F32).at[:, :N_EXPERTS].set(moe_w_router[li])
            br = jnp.zeros((1, LANES), F32).at[0, :N_EXPERTS].set(moe_b_router[li])
            wr_hi = wr.astype(BF16)
            wr_lo = (wr - wr_hi.astype(F32)).astype(BF16)
            x1, h2, route = _odd_out(x, hc, att, c_ln_g[li].reshape(1, D_C), c_ln_b[li].reshape(1, D_C),
                                     w_out[:D_C], w_out[D_C:], gate1, g2, shift2, scale2, wr_hi, wr_lo, br)
            jtab, tile_expert, tile_valid, n_used = _route_plan(route)
            y2 = _moe_experts(tile_expert, tile_valid, n_used, jtab, h2, moe_w1b, moe_w3b, moe_w2b, li)
            pending = (x1, y2, route, gate2)
            new_k.append(_prompt_to_batch_major(k).reshape(BATCH, SEQ, N_KV_HEADS, HEAD_DIM))
            new_v.append(_prompt_to_batch_major(v).reshape(BATCH, SEQ, N_KV_HEADS, HEAD_DIM))

    gf = norm_f.reshape(1, D)
    y_prompt = _final_norm(pending, gf, batch=BATCH, steps=SEQ, stream=0)
    y_sample = _final_norm(pending, gf, batch=DEC_BATCH, steps=DEC_SEQ, stream=1)
    return (y_prompt, y_sample, jnp.stack(new_states, axis=1), jnp.stack(new_k, axis=1), jnp.stack(new_v, axis=1))
```

```python
import functools

import jax
import jax.numpy as jnp
from jax import lax
from jax.experimental import pallas as pl
from jax.experimental.pallas import tpu as pltpu

F32 = jnp.float32
BF16 = jnp.bfloat16

D = 1024
BATCH = 32
SEQ = 256
DEPTH = 4
DEC_BATCH = 8
DEC_SEQ = 1024
PAST_LEN = 256
GRID_W = 64
EPS = 1e-6
D_A = 512
A_BLOCKS = 8
A_BS = 64
A_CONV = 4
A_C = 8.0
D_B = 512
POOL_WINDOWS = (2, 4, 8, 16)
B_GS = 128
D_C = 512
C_CONV = 31
HEAD_DIM = 64
N_Q_HEADS = 8
N_KV_HEADS = 2
D_ATT = 512
D_KV = 128
ROPE_THETA = 10000.0
D_FF = 2816
N_EXPERTS = 8
D_FF_E = 1408
D_IN_EVEN = 1536
D_IN_ODD = 1792

SUB = 8
LANES = 128
N_PROMPT = BATCH * SEQ
N_SAMPLE = DEC_BATCH * DEC_SEQ
N_TOK = N_PROMPT + N_SAMPLE
P_GROUPS = BATCH // SUB
R_PROMPT = SEQ * SUB
R_SAMPLE = DEC_SEQ * SUB
TM = 512
TILES_PER_STREAM = N_PROMPT // TM
MOE_TM = 512
MOE_TILES = 2 * N_TOK // MOE_TM + N_EXPERTS
SLAB = D // LANES
NEG_BIG = -3.0e38
VMEM_LIMIT = 56 * 1024 * 1024


def _params(n_axes, vmem=VMEM_LIMIT):
    return pltpu.CompilerParams(dimension_semantics=("arbitrary",) * n_axes, vmem_limit_bytes=vmem)


def _rms(x, g):
    ms = jnp.mean(x * x, axis=-1, keepdims=True)
    return x * lax.rsqrt(ms + EPS) * g


def _modulate(xn, shift, scale):
    tm = xn.shape[0]
    h = xn.reshape(tm // SUB, SUB, D) * (1.0 + scale)[None] + shift[None]
    return h.reshape(tm, D)


def _gated_add(x, gate, y):
    tm = x.shape[0]
    return x + (y.reshape(tm // SUB, SUB, D) * gate[None]).reshape(tm, D)


def _mod_spec():
    return pl.BlockSpec((None, SUB, D), lambda i, *_: (i // TILES_PER_STREAM, 0, 0))


def _row_spec(width, tm=TM):
    return pl.BlockSpec((tm, width), lambda i, *_: (i, 0))


def _full_spec(shape):
    nd = len(shape)
    return pl.BlockSpec(shape, lambda i, *_: (0,) * nd)


def _ada_kernel(c_ref, w_ref, b_ref, o_ref):
    c = c_ref[...]
    s = (c * jax.nn.sigmoid(c)).astype(BF16)
    o_ref[...] = jnp.dot(s, w_ref[...].astype(BF16), preferred_element_type=F32) + b_ref[...]


def _ada_params(cond16, w_mod, b_mod):
    return pl.pallas_call(
        _ada_kernel,
        grid=(DEPTH, 6),
        in_specs=[
            pl.BlockSpec((16, D), lambda l, j: (0, 0)),
            pl.BlockSpec((None, D, D), lambda l, j: (l, 0, j)),
            pl.BlockSpec((None, None, 1, D), lambda l, j: (l, j, 0, 0)),
        ],
        out_specs=pl.BlockSpec((None, None, 16, D), lambda l, j: (l, j, 0, 0)),
        out_shape=jax.ShapeDtypeStruct((DEPTH, 6, 16, D), F32),
        compiler_params=_params(2),
        name="ada_params",
    )(cond16, w_mod, b_mod.reshape(DEPTH, 6, 1, D))


T_TILE = TM // SUB


def _first_in_kernel(xp_ref, xs_ref, g_ref, sh_ref, sc_ref, w_ref, z_ref, x_ref, cols):
    is_sample = pl.program_id(0) >= TILES_PER_STREAM
    for b in range(SUB):
        xb = jnp.where(is_sample, xs_ref[b], xp_ref[b])
        for c in range(SLAB):
            cols[c, pl.ds(b, T_TILE, stride=SUB), :] = xb[:, c * LANES:(c + 1) * LANES]
    x = jnp.concatenate([cols[c] for c in range(SLAB)], axis=1)
    x_ref[...] = x
    h = _modulate(_rms(x, g_ref[...]), sh_ref[...], sc_ref[...])
    z_ref[...] = jnp.dot(h.astype(BF16), w_ref[...], preferred_element_type=F32)


def _first_in(x_prompt, x_sample, g, shift, scale, w_bf16):
    t_tiles = SEQ // T_TILE
    last_p = TILES_PER_STREAM - 1
    return pl.pallas_call(
        _first_in_kernel,
        grid=(N_TOK // TM,),
        in_specs=[
            pl.BlockSpec((SUB, T_TILE, D), lambda i: (jnp.minimum(i, last_p) // t_tiles,
                                                      jnp.minimum(i, last_p) % t_tiles, 0)),
            pl.BlockSpec((SUB, T_TILE, D), lambda i: (0, jnp.maximum(i - TILES_PER_STREAM, 0), 0)),
            _full_spec((1, D)), _mod_spec(), _mod_spec(), _full_spec((D, D_IN_EVEN))],
        out_specs=[_row_spec(D_IN_EVEN), _row_spec(D)],
        out_shape=[jax.ShapeDtypeStruct((N_TOK, D_IN_EVEN), F32), jax.ShapeDtypeStruct((N_TOK, D), F32)],
        scratch_shapes=[pltpu.VMEM((SLAB, TM, LANES), F32)],
        compiler_params=_params(1),
        name="first_in",
    )(x_prompt, x_sample, g, shift, scale, w_bf16)


RG_CHUNK = 256
RG_PAD = 16


def _rglru_kernel(u_ref, ga_ref, cw_ref, cb_ref, wg_ref, bg_ref, lam_ref, h0_ref,
                  y_ref, hl_ref, xpad, a0, b0, a1, b1, *, rows):
    steps = rows // SUB
    n_chunks = rows // RG_CHUNK

    xpad[0:RG_PAD, :] = jnp.zeros((RG_PAD, LANES), F32)
    xpad[RG_PAD + rows:RG_PAD + rows + RG_PAD, :] = jnp.zeros((RG_PAD, LANES), F32)

    def copy_chunk(c, carry):
        r = pl.multiple_of(c * RG_CHUNK, RG_CHUNK)
        xpad[pl.ds(RG_PAD + r, RG_CHUNK), :] = u_ref[pl.ds(r, RG_CHUNK), :]
        return carry

    lax.fori_loop(0, n_chunks, copy_chunk, 0)

    lam = lam_ref[...]
    softplus_neg = jnp.maximum(-lam, 0.0) + jnp.log1p(jnp.exp(-jnp.abs(lam)))
    a_refs = (a0, a1)
    b_refs = (b0, b1)

    def gate_chunk(c, carry):
        r = pl.multiple_of(c * RG_CHUNK, RG_CHUNK)
        xc = jnp.zeros((RG_CHUNK, LANES), F32) + cb_ref[...]
        for k in range(A_CONV):
            xc = xc + cw_ref[k:k + 1, :] * xpad[pl.ds(r + SUB * k, RG_CHUNK), :]
        pre = jnp.dot(xc.astype(BF16), wg_ref[...], preferred_element_type=F32) + bg_ref[...]
        for d in range(2):
            rg = jax.nn.sigmoid(pre[:, d * 256:d * 256 + LANES])
            ig = jax.nn.sigmoid(pre[:, d * 256 + LANES:(d + 1) * 256])
            log_a = (-A_C) * rg * softplus_neg[:, d * LANES:(d + 1) * LANES]
            a = jnp.exp(log_a)
            one_minus_a2 = -jnp.tanh(log_a) * (a * a + 1.0)
            a_refs[d][pl.ds(r, RG_CHUNK), :] = a
            b_refs[d][pl.ds(r, RG_CHUNK), :] = jnp.sqrt(one_minus_a2) * (ig * xc)
        return carry

    lax.fori_loop(0, n_chunks, gate_chunk, 0)

    def step(t, carry):
        hf, hb = carry
        rf = pl.multiple_of(t * SUB, SUB)
        rb = pl.multiple_of((steps - 1 - t) * SUB, SUB)
        hf = a0[pl.ds(rf, SUB), :] * hf + b0[pl.ds(rf, SUB), :]
        b0[pl.ds(rf, SUB), :] = hf
        hb = a1[pl.ds(rb, SUB), :] * hb + b1[pl.ds(rb, SUB), :]
        b1[pl.ds(rb, SUB), :] = hb
        return hf, hb

    hf, hb = lax.fori_loop(0, steps, step, (h0_ref[0], h0_ref[1]), unroll=8)
    hl_ref[0] = hf
    hl_ref[1] = hb

    def out_chunk(c, carry):
        r = pl.multiple_of(c * RG_CHUNK, RG_CHUNK)
        y = (b0[pl.ds(r, RG_CHUNK), :] + b1[pl.ds(r, RG_CHUNK), :]) * jax.nn.gelu(ga_ref[pl.ds(r, RG_CHUNK), :])
        y_ref[pl.ds(r, RG_CHUNK), :] = y.astype(BF16)
        return carry

    lax.fori_loop(0, n_chunks, out_chunk, 0)


def _rglru_call(z, cw, cb, wg, bg, lam, h0, *, rows, groups, row_block0):
    n_cb = D_A // LANES
    in_specs = [
        pl.BlockSpec((rows, LANES), lambda g, j: (row_block0 + g, j)),
        pl.BlockSpec((rows, LANES), lambda g, j: (row_block0 + g, n_cb + j)),
        pl.BlockSpec((A_CONV, LANES), lambda g, j: (0, j)),
        pl.BlockSpec((1, LANES), lambda g, j: (0, j)),
        pl.BlockSpec((None, LANES, 4 * LANES), lambda g, j: (j, 0, 0)),
        pl.BlockSpec((None, 1, 4 * LANES), lambda g, j: (j, 0, 0)),
        pl.BlockSpec((None, 1, 2 * LANES), lambda g, j: (j, 0, 0)),
        pl.BlockSpec((None, 2, SUB, LANES), lambda g, j: (g, 0, 0, j)),
    ]
    return pl.pallas_call(
        functools.partial(_rglru_kernel, rows=rows),
        grid=(groups, n_cb),
        in_specs=in_specs,
        out_specs=[
            pl.BlockSpec((rows, LANES), lambda g, j: (g, j)),
            pl.BlockSpec((None, 2, SUB, LANES), lambda g, j: (g, 0, 0, j)),
        ],
        out_shape=[
            jax.ShapeDtypeStruct((groups * rows, D_A), BF16),
            jax.ShapeDtypeStruct((groups, 2, SUB, D_A), F32),
        ],
        scratch_shapes=[pltpu.VMEM((rows + 2 * RG_PAD, LANES), F32)] + [pltpu.VMEM((rows, LANES), F32)] * 4,
        compiler_params=_params(2),
        name=f"rglru_{rows}",
    )(z, z, cw, cb, wg, bg, lam, h0)


POOL_CHUNK = 256
POOL_PAD = 64


def _pool_kernel(u_ref, w_ref, s_ref, y_ref, xpad, *, rows):
    steps = rows // SUB
    n_chunks = rows // POOL_CHUNK
    j = pl.program_id(1)

    xpad[0:POOL_PAD, :] = jnp.zeros((POOL_PAD, LANES), F32)
    xpad[POOL_PAD + rows:POOL_PAD + rows + POOL_PAD, :] = jnp.zeros((POOL_PAD, LANES), F32)

    def copy_chunk(c, carry):
        r = pl.multiple_of(c * POOL_CHUNK, POOL_CHUNK)
        xpad[pl.ds(POOL_PAD + r, POOL_CHUNK), :] = u_ref[pl.ds(r, POOL_CHUNK), :]
        return carry

    lax.fori_loop(0, n_chunks, copy_chunk, 0)

    for gi, win in enumerate(POOL_WINDOWS):
        half = win // 2

        @pl.when(j == gi)
        def _(half=half, win=win):
            def chunk(c, carry):
                r = pl.multiple_of(c * POOL_CHUNK, POOL_CHUNK)
                acc = xpad[pl.ds(POOL_PAD + r - SUB * half, POOL_CHUNK), :]
                for s in range(1, win):
                    acc = acc + xpad[pl.ds(POOL_PAD + r + SUB * (s - half), POOL_CHUNK), :]
                t = (r + lax.broadcasted_iota(jnp.int32, (POOL_CHUNK, 1), 0)) // SUB
                cnt = jnp.minimum(t + half, steps) - jnp.maximum(t - half, 0)
                dlt = acc / cnt.astype(F32) - xpad[pl.ds(POOL_PAD + r, POOL_CHUNK), :]
                y = jnp.dot(dlt.astype(BF16), w_ref[...], preferred_element_type=F32) * s_ref[...]
                y_ref[pl.ds(r, POOL_CHUNK), :] = y.astype(BF16)
                return carry

            lax.fori_loop(0, n_chunks, chunk, 0)


def _pool_call(z, w_pool_bf16, s_pool, *, rows, groups, row_block0):
    n_cb = D_B // LANES
    col0 = 2 * D_A // LANES
    in_specs = [
        pl.BlockSpec((rows, LANES), lambda g, j: (row_block0 + g, col0 + j)),
        pl.BlockSpec((None, B_GS, B_GS), lambda g, j: (j, 0, 0)),
        pl.BlockSpec((1, LANES), lambda g, j: (0, j)),
    ]
    return pl.pallas_call(
        functools.partial(_pool_kernel, rows=rows),
        grid=(groups, n_cb),
        in_specs=in_specs,
        out_specs=pl.BlockSpec((rows, LANES), lambda g, j: (g, j)),
        out_shape=jax.ShapeDtypeStruct((groups * rows, D_B), BF16),
        scratch_shapes=[pltpu.VMEM((rows + 2 * POOL_PAD, LANES), F32)],
        compiler_params=_params(2),
        name=f"pool_{rows}",
    )(z, w_pool_bf16, s_pool)


def _stream_specs(width):
    prompt = pl.BlockSpec((TM, width), lambda i, *_: (jnp.minimum(i, TILES_PER_STREAM - 1), 0))
    sample = pl.BlockSpec((TM, width), lambda i, *_: (jnp.maximum(i - TILES_PER_STREAM, 0), 0))
    return [prompt, sample]


def _pick_stream(prompt_ref, sample_ref):
    return jnp.where(pl.program_id(0) >= TILES_PER_STREAM, sample_ref[...], prompt_ref[...])


def _even_tail_kernel(x_ref, yap_ref, yas_ref, ybp_ref, ybs_ref, wa_ref, wb_ref, gate1_ref,
                      g2_ref, sh_ref, sc_ref, gate2_ref, w1_ref, w3_ref, w2_ref, o_ref):
    y = jnp.dot(_pick_stream(yap_ref, yas_ref), wa_ref[...], preferred_element_type=F32)
    y = y + jnp.dot(_pick_stream(ybp_ref, ybs_ref), wb_ref[...], preferred_element_type=F32)
    x1 = _gated_add(x_ref[...], gate1_ref[...], y)
    h = _modulate(_rms(x1, g2_ref[...]), sh_ref[...], sc_ref[...]).astype(BF16)
    a = jnp.dot(h, w1_ref[...], preferred_element_type=F32)
    b = jnp.dot(h, w3_ref[...], preferred_element_type=F32)
    u = (a * jax.nn.sigmoid(a) * b).astype(BF16)
    f = jnp.dot(u, w2_ref[...], preferred_element_type=F32)
    o_ref[...] = _gated_add(x1, gate2_ref[...], f)


def _even_tail(x, ya, yb, wa, wb, gate1, g2, shift2, scale2, gate2, w1, w3, w2, li):
    half = D // 2
    resident = pl.Buffered(1)
    return pl.pallas_call(
        _even_tail_kernel,
        grid=(N_TOK // TM,),
        in_specs=[_row_spec(D)] + _stream_specs(half) + _stream_specs(half) +
                 [pl.BlockSpec((half, D), lambda i: (0, 0), pipeline_mode=resident),
                  pl.BlockSpec((half, D), lambda i: (0, 0), pipeline_mode=resident),
                  _mod_spec(), _full_spec((1, D)), _mod_spec(), _mod_spec(), _mod_spec(),
                  pl.BlockSpec((None, D, D_FF), lambda i: (li, 0, 0), pipeline_mode=resident),
                  pl.BlockSpec((None, D, D_FF), lambda i: (li, 0, 0), pipeline_mode=resident),
                  pl.BlockSpec((None, D_FF, D), lambda i: (li, 0, 0), pipeline_mode=resident)],
        out_specs=_row_spec(D),
        out_shape=jax.ShapeDtypeStruct((N_TOK, D), F32),
        compiler_params=_params(1),
        name="even_tail",
    )(x, *ya, *yb, wa, wb, gate1, g2, shift2, scale2, gate2, w1, w3, w2)


def _head_rms(x, ones_bd, g):
    sq = x * x
    hi = sq.astype(BF16)
    lo = (sq - hi.astype(F32)).astype(BF16)
    ms = jnp.dot(hi, ones_bd, preferred_element_type=F32) + jnp.dot(lo, ones_bd, preferred_element_type=F32)
    return x * lax.rsqrt(ms + EPS) * g


def _rope(x, cos, sin_signed):
    w = x.shape[-1]
    lane = lax.broadcasted_iota(jnp.int32, x.shape, 1)
    first = (lane % 32) < 16
    partner = jnp.where(first, pltpu.roll(x, w - 16, 1), pltpu.roll(x, 16, 1))
    return x * cos + partner * sin_signed


Q_BLOCKS = D_ATT // LANES


def _odd_in_kernel(x_ref, g_ref, sh_ref, sc_ref, w_ref, bdq_ref, bdk_ref, qg_ref, kg_ref, cos_ref, sin_ref,
                   glu_ref, q0_ref, q1_ref, q2_ref, q3_ref, k_ref, v_ref):
    i = pl.program_id(0)
    h = _modulate(_rms(x_ref[...], g_ref[...]), sh_ref[...], sc_ref[...])
    z = jnp.dot(h.astype(BF16), w_ref[...], preferred_element_type=F32)
    glu_ref[...] = z[:, :D_C] * jax.nn.sigmoid(z[:, D_C:2 * D_C])
    o1 = 2 * D_C
    o2 = o1 + D_ATT
    o3 = o2 + D_KV
    q = _head_rms(z[:, o1:o2], bdq_ref[...], qg_ref[...])
    k = _head_rms(z[:, o2:o3], bdk_ref[...], kg_ref[...])
    cos = cos_ref[...]
    sin = sin_ref[...]
    is_sample = i >= TILES_PER_STREAM
    q_r = _rope(q, jnp.concatenate([cos] * 4, axis=1), jnp.concatenate([sin] * 4, axis=1))
    k_r = _rope(k, cos, sin)
    q = jnp.where(is_sample, q_r, q)
    for c, q_ref in enumerate((q0_ref, q1_ref, q2_ref, q3_ref)):
        q_ref[...] = q[:, c * LANES:(c + 1) * LANES]
    k_ref[...] = jnp.where(is_sample, k_r, k)
    v_ref[...] = z[:, o3:]


def _odd_in(x, g, shift, scale, w_bf16, bdq, bdk, qg, kg, cos_t, sin_t):
    rope_spec = pl.BlockSpec((TM, D_KV), lambda i: (jnp.maximum(i - TILES_PER_STREAM, 0), 0))
    return pl.pallas_call(
        _odd_in_kernel,
        grid=(N_TOK // TM,),
        in_specs=[_row_spec(D), _full_spec((1, D)), _mod_spec(), _mod_spec(), _full_spec((D, D_IN_ODD)),
                  _full_spec((D_ATT, D_ATT)), _full_spec((D_KV, D_KV)), _full_spec((1, D_ATT)),
                  _full_spec((1, D_KV)), rope_spec, rope_spec],
        out_specs=[_row_spec(D_C)] + [_row_spec(LANES)] * (Q_BLOCKS + 2),
        out_shape=[jax.ShapeDtypeStruct((N_TOK, D_C), F32)] +
                  [jax.ShapeDtypeStruct((N_TOK, LANES), F32)] * (Q_BLOCKS + 2),
        compiler_params=_params(1),
        name="odd_in",
    )(x, g, shift, scale, w_bf16, bdq, bdk, qg, kg, cos_t, sin_t)


def _attend(q_refs, o_refs, q_rows, k_all, v_all):
    scale = HEAD_DIM ** -0.5
    heads_per_block = LANES // HEAD_DIM
    group = N_Q_HEADS // N_KV_HEADS
    kv = []
    for h in range(N_KV_HEADS):
        kv.append((k_all[:, h * HEAD_DIM:(h + 1) * HEAD_DIM].astype(BF16),
                   v_all[:, h * HEAD_DIM:(h + 1) * HEAD_DIM].astype(BF16)))
    for c in range(Q_BLOCKS):
        qc = (q_refs[c][q_rows, :] * scale).astype(BF16)
        outs = []
        for sub in range(heads_per_block):
            kh, vh = kv[(c * heads_per_block + sub) // group]
            qh = qc[:, sub * HEAD_DIM:(sub + 1) * HEAD_DIM]
            s = lax.dot_general(qh, kh, (((1,), (1,)), ((), ())), preferred_element_type=F32)
            m = jnp.max(s, axis=-1, keepdims=True)
            p = jnp.exp(s - m)
            l = jnp.sum(p, axis=-1, keepdims=True)
            outs.append(jnp.dot(p.astype(BF16), vh, preferred_element_type=F32) / l)
        o_refs[c][q_rows, :] = jnp.concatenate(outs, axis=1)


def _attn_prompt_kernel(*refs):
    q_refs, (k_ref, v_ref), o_refs = refs[:Q_BLOCKS], refs[Q_BLOCKS:Q_BLOCKS + 2], refs[Q_BLOCKS + 2:]
    rows = pl.ds(pl.program_id(1), SEQ, stride=SUB)
    _attend(q_refs, o_refs, rows, k_ref[rows, :], v_ref[rows, :])


def _attn_sample_kernel(*refs):
    q_refs, (k_ref, v_ref, ck_ref, cv_ref), o_refs = refs[:Q_BLOCKS], refs[Q_BLOCKS:Q_BLOCKS + 4], refs[Q_BLOCKS + 4:]
    b = pl.program_id(1)
    kv_rows = pl.ds(b, DEC_SEQ, stride=SUB)
    k_all = jnp.concatenate([ck_ref[...], k_ref[kv_rows, :]], axis=0)
    v_all = jnp.concatenate([cv_ref[...], v_ref[kv_rows, :]], axis=0)
    _attend(q_refs, o_refs, pl.ds(b, SEQ, stride=SUB), k_all, v_all)


def _attention(qs, k, v, cache_k_l, cache_v_l):
    chunk = pl.BlockSpec((R_PROMPT, LANES), lambda g, b: (g, 0))
    out_shape = [jax.ShapeDtypeStruct((N_PROMPT, LANES), F32)] * Q_BLOCKS
    att_p = pl.pallas_call(
        _attn_prompt_kernel,
        grid=(P_GROUPS, SUB),
        in_specs=[chunk] * (Q_BLOCKS + 2),
        out_specs=[chunk] * Q_BLOCKS,
        out_shape=out_shape,
        compiler_params=_params(2),
        name="attn_prompt",
    )(*qs, k, v)
    q_chunks = DEC_SEQ // SEQ
    q_chunk = pl.BlockSpec((R_PROMPT, LANES), lambda c, b: (P_GROUPS + c, 0))
    kv_all = pl.BlockSpec((R_SAMPLE, LANES), lambda c, b: (1, 0))
    cache = pl.BlockSpec((None, PAST_LEN, D_KV), lambda c, b: (b, 0, 0))
    att_s = pl.pallas_call(
        _attn_sample_kernel,
        grid=(q_chunks, DEC_BATCH),
        in_specs=[q_chunk] * Q_BLOCKS + [kv_all, kv_all, cache, cache],
        out_specs=[pl.BlockSpec((R_PROMPT, LANES), lambda c, b: (c, 0))] * Q_BLOCKS,
        out_shape=out_shape,
        compiler_params=_params(2),
        name="attn_sample",
    )(*qs, k, v, cache_k_l, cache_v_l)
    return att_p, att_s


CV_CHUNK = 64
CV_PAD = (C_CONV // 2) * SUB


def _conv_kernel(u_ref, w_ref, b_ref, y_ref, xpad, *, rows):
    n_copy = rows // 256
    xpad[0:CV_PAD, :] = jnp.zeros((CV_PAD, LANES), F32)
    xpad[CV_PAD + rows:CV_PAD + rows + CV_PAD, :] = jnp.zeros((CV_PAD, LANES), F32)

    def copy_chunk(c, carry):
        r = pl.multiple_of(c * 256, 256)
        xpad[pl.ds(CV_PAD + r, 256), :] = u_ref[pl.ds(r, 256), :]
        return carry

    lax.fori_loop(0, n_copy, copy_chunk, 0)

    def chunk(c, carry):
        r = pl.multiple_of(c * CV_CHUNK, CV_CHUNK)
        acc = jnp.zeros((CV_CHUNK, LANES), F32) + b_ref[...]
        for k in range(C_CONV):
            acc = acc + w_ref[k:k + 1, :] * xpad[pl.ds(r + SUB * k, CV_CHUNK), :]
        y_ref[pl.ds(r, CV_CHUNK), :] = acc
        return carry

    lax.fori_loop(0, rows // CV_CHUNK, chunk, 0)


def _conv_call(glu, w, b, *, rows, groups, row_block0):
    n_cb = D_C // LANES
    in_specs = [
        pl.BlockSpec((rows, LANES), lambda g, j: (row_block0 + g, j)),
        pl.BlockSpec((C_CONV, LANES), lambda g, j: (0, j)),
        pl.BlockSpec((1, LANES), lambda g, j: (0, j)),
    ]
    return pl.pallas_call(
        functools.partial(_conv_kernel, rows=rows),
        grid=(groups, n_cb),
        in_specs=in_specs,
        out_specs=pl.BlockSpec((rows, LANES), lambda g, j: (g, j)),
        out_shape=jax.ShapeDtypeStruct((groups * rows, D_C), F32),
        scratch_shapes=[pltpu.VMEM((rows + 2 * CV_PAD, LANES), F32)],
        compiler_params=_params(2),
        name=f"conv_{rows}",
    )(glu, w, b)


def _odd_out_kernel(x_ref, hcp_ref, hcs_ref, *refs):
    att_refs, refs = refs[:2 * Q_BLOCKS], refs[2 * Q_BLOCKS:]
    (lng_ref, lnb_ref, wc_ref, wa_ref, gate_ref, g2_ref, sh_ref, sc_ref, wrh_ref, wrl_ref, br_ref, tril_ref,
     x1_ref, xloc_ref, route_ref, cnt_ref) = refs
    att = jnp.concatenate([_pick_stream(att_refs[c], att_refs[Q_BLOCKS + c]) for c in range(Q_BLOCKS)], axis=1)
    hc = _pick_stream(hcp_ref, hcs_ref)
    mu = jnp.mean(hc, axis=-1, keepdims=True)
    xc = hc - mu
    var = jnp.mean(xc * xc, axis=-1, keepdims=True)
    ln = xc * lax.rsqrt(var + EPS) * lng_ref[...] + lnb_ref[...]
    yc = (ln * jax.nn.sigmoid(ln)).astype(BF16)
    y = jnp.dot(yc, wc_ref[...], preferred_element_type=F32)
    y = y + jnp.dot(att.astype(BF16), wa_ref[...], preferred_element_type=F32)
    x1 = _gated_add(x_ref[...], gate_ref[...], y)
    x1_ref[...] = x1
    h2 = _modulate(_rms(x1, g2_ref[...]), sh_ref[...], sc_ref[...])
    h2_hi = h2.astype(BF16)
    h2_lo = (h2 - h2_hi.astype(F32)).astype(BF16)
    logits = (jnp.dot(h2_hi, wrh_ref[...], preferred_element_type=F32)
              + jnp.dot(h2_lo, wrh_ref[...], preferred_element_type=F32)
              + jnp.dot(h2_hi, wrl_ref[...], preferred_element_type=F32)) + br_ref[...]
    lane = lax.broadcasted_iota(jnp.int32, logits.shape, 1).astype(F32)
    lg = jnp.where(lane < N_EXPERTS, logits, NEG_BIG)
    m1 = jnp.max(lg, axis=-1, keepdims=True)
    i1 = jnp.min(jnp.where(lg == m1, lane, float(LANES)), axis=-1, keepdims=True)
    lg2 = jnp.where(lane == i1, NEG_BIG, lg)
    m2 = jnp.max(lg2, axis=-1, keepdims=True)
    i2 = jnp.min(jnp.where(lg2 == m2, lane, float(LANES)), axis=-1, keepdims=True)
    e = jnp.exp(m2 - m1)
    den = 1.0 + e

    e1 = jnp.where(lane == i1, 1.0, 0.0)
    e2 = jnp.where(lane == i2, 1.0, 0.0)
    before1 = jnp.dot(tril_ref[...], e1.astype(BF16), preferred_element_type=F32)
    before2 = jnp.dot(tril_ref[...], e2.astype(BF16), preferred_element_type=F32)
    cnt1 = jnp.sum(e1, axis=0, keepdims=True)
    cnt2 = jnp.sum(e2, axis=0, keepdims=True)
    start = (jnp.sum(jnp.where(i1 < lane, 1.0, 0.0), axis=0, keepdims=True)
             + jnp.sum(jnp.where(i2 < lane, 1.0, 0.0), axis=0, keepdims=True))
    pos0 = jnp.sum(e1 * (start + before1), axis=-1, keepdims=True)
    pos1 = jnp.sum(e2 * (start + cnt1 + before2), axis=-1, keepdims=True)
    route = jnp.where(lane == 0.0, i1, jnp.where(lane == 1.0, i2,
                      jnp.where(lane == 2.0, 1.0 / den, jnp.where(lane == 3.0, e / den,
                                jnp.where(lane == 4.0, pos0, jnp.where(lane == 5.0, pos1, 0.0))))))
    route_ref[...] = route
    cnt_ref[...] = jnp.broadcast_to(cnt1 + cnt2, (SUB, LANES))

    q = lax.broadcasted_iota(jnp.int32, (TM, 2 * TM), 1).astype(F32)
    sel = jnp.where(jnp.logical_or(q == pos0, q == pos1), 1.0, 0.0).astype(BF16)
    xloc = lax.dot_general(sel, h2_hi, (((0,), (0,)), ((), ())), preferred_element_type=F32)
    for j in range(SLAB):
        xloc_ref[pl.ds(j, 2 * TM, stride=SLAB), :] = xloc[:, j * LANES:(j + 1) * LANES]


def _odd_out(x, hc, att, lng, lnb, wc, wa, gate, g2, shift2, scale2, wr_hi, wr_lo, br_pad):
    half = D // 2
    att_specs = [_stream_specs(LANES)[0]] * Q_BLOCKS + [_stream_specs(LANES)[1]] * Q_BLOCKS
    return pl.pallas_call(
        _odd_out_kernel,
        grid=(N_TOK // TM,),
        in_specs=[_row_spec(D)] + _stream_specs(half) + att_specs +
                 [_full_spec((1, half)), _full_spec((1, half)),
                  _full_spec((half, D)), _full_spec((half, D)), _mod_spec(),
                  _full_spec((1, D)), _mod_spec(), _mod_spec(),
                  _full_spec((D, LANES)), _full_spec((D, LANES)), _full_spec((1, LANES)), _full_spec((TM, TM))],
        out_specs=[_row_spec(D), pl.BlockSpec((2 * TM * SLAB, LANES), lambda i: (i, 0)), _row_spec(LANES),
                   pl.BlockSpec((SUB, LANES), lambda i: (i, 0))],
        out_shape=[jax.ShapeDtypeStruct((N_TOK, D), F32), jax.ShapeDtypeStruct((2 * N_TOK * SLAB, LANES), F32),
                   jax.ShapeDtypeStruct((N_TOK, LANES), F32),
                   jax.ShapeDtypeStruct((N_TOK // TM * SUB, LANES), F32)],
        compiler_params=_params(1),
        name="odd_out",
    )(x, *hc, *att[0], *att[1], lng, lnb, wc, wa, gate, g2, shift2, scale2, wr_hi, wr_lo, br_pad,
      jnp.tril(jnp.ones((TM, TM), BF16), -1))


N_CHUNKS = N_TOK // TM


def _moe_kernel(te_ref, nv_ref, nu_ref, src_ref, dst_ref, len_ref, xloc_hbm, w1_ref, w3_ref, w2_ref, yloc_hbm,
                xbuf, obuf, gsem, ssem):
    del te_ref
    i = pl.program_id(0)
    n_used = nu_ref[0]
    slot = i % 2
    other = 1 - slot

    def rows(start, n):
        return pl.ds(pl.multiple_of(start * SLAB, SLAB), n * SLAB)

    def for_pieces(tile, fn):
        def body(c, carry):
            k = tile * N_CHUNKS + c
            n = len_ref[k]

            @pl.when(n > 0)
            def _():
                fn(src_ref[k], dst_ref[k], n)

            return carry

        lax.fori_loop(0, N_CHUNKS, body, 0)

    def start_gather(tile, s):
        for_pieces(tile, lambda src, dst, n: pltpu.make_async_copy(
            xloc_hbm.at[rows(src, n), :], xbuf.at[s, rows(dst, n), :], gsem.at[s]).start())

    def start_scatter(tile, s):
        for_pieces(tile, lambda src, dst, n: pltpu.make_async_copy(
            obuf.at[s, rows(dst, n), :], yloc_hbm.at[rows(src, n), :], ssem.at[s]).start())

    def wait_gather(s, n):
        pltpu.make_async_copy(xloc_hbm.at[rows(0, n), :], xbuf.at[s, rows(0, n), :], gsem.at[s]).wait()

    def wait_scatter(s, n):
        pltpu.make_async_copy(obuf.at[s, rows(0, n), :], yloc_hbm.at[rows(0, n), :], ssem.at[s]).wait()

    @pl.when(i == 0)
    def _():
        xbuf[...] = jnp.zeros(xbuf.shape, F32)
        start_gather(0, 0)

    @pl.when(i < n_used)
    def _():
        wait_gather(slot, nv_ref[i])

        @pl.when(i + 1 < n_used)
        def _():
            start_gather(i + 1, other)

        @pl.when(i >= 2)
        def _():
            wait_scatter(slot, nv_ref[jnp.maximum(i - 2, 0)])

        x = jnp.concatenate([xbuf[slot, pl.ds(j, MOE_TM, stride=SLAB), :] for j in range(SLAB)], axis=1)
        h = x.astype(BF16)
        a = jnp.dot(h, w1_ref[...], preferred_element_type=F32)
        b = jnp.dot(h, w3_ref[...], preferred_element_type=F32)
        u = (a * jax.nn.sigmoid(a) * b).astype(BF16)
        y = jnp.dot(u, w2_ref[...], preferred_element_type=F32)
        for j in range(SLAB):
            obuf[slot, pl.ds(j, MOE_TM, stride=SLAB), :] = y[:, j * LANES:(j + 1) * LANES]
        start_scatter(i, slot)

        @pl.when(i == n_used - 1)
        def _():
            @pl.when(i >= 1)
            def _():
                wait_scatter(other, nv_ref[jnp.maximum(i - 1, 0)])

            wait_scatter(slot, nv_ref[i])


def _moe_experts(plan, xloc, w1, w3, w2, li):
    def weight(rows, cols):
        return pl.BlockSpec((None, None, rows, cols), lambda i, te, *_: (li, te[i], 0, 0))

    grid_spec = pltpu.PrefetchScalarGridSpec(
        num_scalar_prefetch=len(plan),
        grid=(MOE_TILES,),
        in_specs=[pl.BlockSpec(memory_space=pl.ANY),
                  weight(D, D_FF_E), weight(D, D_FF_E), weight(D_FF_E, D)],
        out_specs=pl.BlockSpec(memory_space=pl.ANY),
        scratch_shapes=[
            pltpu.VMEM((2, MOE_TM * SLAB, LANES), F32),
            pltpu.VMEM((2, MOE_TM * SLAB, LANES), F32),
            pltpu.SemaphoreType.DMA((2,)),
            pltpu.SemaphoreType.DMA((2,)),
        ],
    )
    return pl.pallas_call(
        _moe_kernel,
        grid_spec=grid_spec,
        out_shape=jax.ShapeDtypeStruct((2 * N_TOK * SLAB, LANES), F32),
        compiler_params=_params(1),
        name="moe_experts",
    )(*plan, xloc, w1, w3, w2)


def _route_plan(counts):
    cnt = counts.reshape(N_CHUNKS, SUB, LANES)[:, 0, :N_EXPERTS].astype(jnp.int32)
    run_local = jnp.cumsum(cnt, axis=1) - cnt
    run_group = jnp.cumsum(cnt, axis=0) - cnt
    total = jnp.sum(cnt, axis=0)
    tiles = (total + MOE_TM - 1) // MOE_TM
    tile_end = jnp.cumsum(tiles)
    tile_ids = jnp.arange(MOE_TILES, dtype=jnp.int32)
    tile_expert = jnp.minimum(jnp.sum((tile_ids[:, None] >= tile_end[None, :]).astype(jnp.int32), axis=1),
                              N_EXPERTS - 1)
    first = (tile_ids - (tile_end - tiles)[tile_expert]) * MOE_TM
    used = tile_ids < tile_end[-1]
    tile_valid = jnp.where(used, jnp.clip(total[tile_expert] - first, 0, MOE_TM), 0).astype(jnp.int32)
    a = run_group[:, tile_expert].T
    n = cnt[:, tile_expert].T
    lo = jnp.maximum(a, first[:, None])
    hi = jnp.minimum(a + n, first[:, None] + MOE_TM)
    length = jnp.where(used[:, None], jnp.maximum(hi - lo, 0), 0)
    chunk_row0 = jnp.arange(N_CHUNKS, dtype=jnp.int32)[None, :] * (2 * TM)
    src = jnp.where(length > 0, chunk_row0 + run_local[:, tile_expert].T + (lo - a), 0)
    dst = jnp.where(length > 0, lo - first[:, None], 0)
    n_used = tile_end[-1:].astype(jnp.int32)
    flat = lambda t: t.astype(jnp.int32).reshape(-1)
    return tile_expert, tile_valid, n_used, flat(src), flat(dst), flat(length)


def _moe_combined(x_ref, y_ref, route_ref, gate_ref):
    y = jnp.concatenate([y_ref[pl.ds(j, 2 * TM, stride=SLAB), :] for j in range(SLAB)], axis=1).astype(BF16)
    q = lax.broadcasted_iota(jnp.int32, (TM, 2 * TM), 1).astype(F32)
    sel0 = jnp.where(q == route_ref[:, 4:5], 1.0, 0.0).astype(BF16)
    sel1 = jnp.where(q == route_ref[:, 5:6], 1.0, 0.0).astype(BF16)
    mix = (route_ref[:, 2:3] * jnp.dot(sel0, y, preferred_element_type=F32)
           + route_ref[:, 3:4] * jnp.dot(sel1, y, preferred_element_type=F32))
    return _gated_add(x_ref[...], gate_ref[...], mix)


def _moe_pending_specs(tile0=0, stream=None):
    gate = _mod_spec() if stream is None else pl.BlockSpec((None, SUB, D), lambda i: (stream, 0, 0))
    return [pl.BlockSpec((TM, D), lambda i: (tile0 + i, 0)),
            pl.BlockSpec((TM * 2 * SLAB, LANES), lambda i: (tile0 + i, 0)),
            pl.BlockSpec((TM, LANES), lambda i: (tile0 + i, 0)), gate]


def _even_in_moe_kernel(x1_ref, y_ref, route_ref, gate_ref, g_ref, sh_ref, sc_ref, w_ref, z_ref, x_ref):
    x = _moe_combined(x1_ref, y_ref, route_ref, gate_ref)
    x_ref[...] = x
    h = _modulate(_rms(x, g_ref[...]), sh_ref[...], sc_ref[...])
    z_ref[...] = jnp.dot(h.astype(BF16), w_ref[...], preferred_element_type=F32)


def _even_in_moe(pending, g, shift, scale, w_bf16):
    return pl.pallas_call(
        _even_in_moe_kernel,
        grid=(N_TOK // TM,),
        in_specs=_moe_pending_specs() + [_full_spec((1, D)), _mod_spec(), _mod_spec(), _full_spec((D, D_IN_EVEN))],
        out_specs=[_row_spec(D_IN_EVEN), _row_spec(D)],
        out_shape=[jax.ShapeDtypeStruct((N_TOK, D_IN_EVEN), F32), jax.ShapeDtypeStruct((N_TOK, D), F32)],
        compiler_params=_params(1),
        name="even_in_moe",
    )(*pending, g, shift, scale, w_bf16)


def _final_kernel(x1_ref, y_ref, route_ref, gate_ref, g_ref, o_ref, cols):
    y = _rms(_moe_combined(x1_ref, y_ref, route_ref, gate_ref), g_ref[...])
    for c in range(SLAB):
        cols[c] = y[:, c * LANES:(c + 1) * LANES]
    for b in range(SUB):
        for c in range(SLAB):
            o_ref[b, :, c * LANES:(c + 1) * LANES] = cols[c, pl.ds(b, T_TILE, stride=SUB), :]


def _final_norm(pending, g, *, batch, steps, stream):
    t_tiles = steps // T_TILE
    return pl.pallas_call(
        _final_kernel,
        grid=(batch * steps // TM,),
        in_specs=_moe_pending_specs(stream * TILES_PER_STREAM, stream) + [_full_spec((1, D))],
        out_specs=pl.BlockSpec((SUB, T_TILE, D), lambda i: (i // t_tiles, i % t_tiles, 0)),
        out_shape=jax.ShapeDtypeStruct((batch, steps, D), F32),
        scratch_shapes=[pltpu.VMEM((SLAB, TM, LANES), F32)],
        compiler_params=_params(1),
        name=f"final_norm_{steps}",
    )(*pending, g)


def _block_diag_gates(w_r, w_i, b_r, b_i):
    def bd(w):
        w4 = w.reshape(4, 2, A_BS, A_BS)
        z = jnp.zeros((4, A_BS, A_BS), w.dtype)
        top = jnp.concatenate([w4[:, 0], z], axis=2)
        bot = jnp.concatenate([z, w4[:, 1]], axis=2)
        return jnp.concatenate([top, bot], axis=1)

    wg = jnp.concatenate([bd(w_r[0]), bd(w_i[0]), bd(w_r[1]), bd(w_i[1])], axis=2)
    bg = jnp.concatenate([b_r[0].reshape(4, 1, LANES), b_i[0].reshape(4, 1, LANES),
                          b_r[1].reshape(4, 1, LANES), b_i[1].reshape(4, 1, LANES)], axis=2)
    return wg.astype(BF16), bg


def _head_mean_matrix(width):
    idx = jnp.arange(width) // HEAD_DIM
    return ((idx[:, None] == idx[None, :]).astype(F32) / HEAD_DIM).astype(BF16)


def _rope_tables():
    pos = jnp.arange(DEC_SEQ)
    row = (pos // GRID_W).astype(F32)
    col = (pos % GRID_W).astype(F32)
    n_freq = HEAD_DIM // 4
    inv = ROPE_THETA ** (-jnp.arange(n_freq, dtype=F32) / n_freq)
    ang = jnp.stack([row[:, None] * inv, col[:, None] * inv], axis=1)
    cos = jnp.cos(ang)
    sin = jnp.sin(ang)
    cos_h = jnp.stack([cos, cos], axis=2).reshape(DEC_SEQ, HEAD_DIM)
    sin_h = jnp.stack([-sin, sin], axis=2).reshape(DEC_SEQ, HEAD_DIM)
    cos_t = jnp.tile(cos_h, (1, 2))
    sin_t = jnp.tile(sin_h, (1, 2))
    cos_t = jnp.broadcast_to(cos_t[:, None, :], (DEC_SEQ, SUB, D_KV)).reshape(N_SAMPLE, D_KV)
    sin_t = jnp.broadcast_to(sin_t[:, None, :], (DEC_SEQ, SUB, D_KV)).reshape(N_SAMPLE, D_KV)
    return cos_t, sin_t


def _prompt_to_batch_major(a):
    w = a.shape[-1]
    return a[:N_PROMPT].reshape(P_GROUPS, SEQ, SUB, w).transpose(0, 2, 1, 3).reshape(BATCH, SEQ, w)


def kernel(x_prompt, x_sample, c, state_rglru, cache_k, cache_v, c_ctx, w_mod, b_mod, norm1, norm2, ev_w_in, a_conv_w, a_conv_b, a_w_r, a_b_r, a_w_i, a_b_i, a_lam, b_w_pool, b_scale, ev_w_out, od_w_in, c_conv_w, c_conv_b, c_ln_g, c_ln_b, q_norm, k_norm, od_w_out, ff_w1, ff_w3, ff_w2, moe_w_router, moe_b_router, moe_w1, moe_w3, moe_w2, norm_f):
    cond16 = jnp.concatenate([c_ctx[None, :], c, jnp.zeros((16 - 1 - DEC_BATCH, D), F32)], axis=0)
    mods = _ada_params(cond16, w_mod, b_mod)
    mods = jnp.stack([jnp.broadcast_to(mods[:, :, 0:1], (DEPTH, 6, SUB, D)), mods[:, :, 1:1 + SUB]], axis=2)

    cos_t, sin_t = _rope_tables()
    bdq = _head_mean_matrix(D_ATT)
    bdk = _head_mean_matrix(D_KV)
    cache_k4 = cache_k.reshape(DEC_BATCH, DEPTH // 2, PAST_LEN, D_KV)
    cache_v4 = cache_v.reshape(DEC_BATCH, DEPTH // 2, PAST_LEN, D_KV)
    ff_w1b, ff_w3b, ff_w2b = ff_w1.astype(BF16), ff_w3.astype(BF16), ff_w2.astype(BF16)
    moe_w1b, moe_w3b, moe_w2b = moe_w1.astype(BF16), moe_w3.astype(BF16), moe_w2.astype(BF16)

    x = pending = None
    new_states, new_k, new_v = [], [], []
    for layer in range(DEPTH):
        li = layer // 2
        shift1, scale1, gate1, shift2, scale2, gate2 = [mods[layer, j] for j in range(6)]
        g1 = norm1[layer].reshape(1, D)
        g2 = norm2[layer].reshape(1, D)
        if layer % 2 == 0:
            if layer == 0:
                z, x = _first_in(x_prompt, x_sample, g1, shift1, scale1, ev_w_in[li].astype(BF16))
            else:
                z, x = _even_in_moe(pending, g1, shift1, scale1, ev_w_in[li].astype(BF16))
            wg, bg = _block_diag_gates(a_w_r[li], a_w_i[li], a_b_r[li], a_b_i[li])
            lam = jnp.concatenate([a_lam[li, 0].reshape(4, 1, LANES), a_lam[li, 1].reshape(4, 1, LANES)], axis=2)
            cb = a_conv_b[li].reshape(1, D_A)
            h0_p = jnp.zeros((P_GROUPS, 2, SUB, D_A), F32)
            h0_s = state_rglru[:, li].transpose(1, 0, 2)[None]
            ya_p, h_last = _rglru_call(z, a_conv_w[li], cb, wg, bg, lam, h0_p,
                                       rows=R_PROMPT, groups=P_GROUPS, row_block0=0)
            ya_s, _ = _rglru_call(z, a_conv_w[li], cb, wg, bg, lam, h0_s,
                                  rows=R_SAMPLE, groups=1, row_block0=1)
            wp = b_w_pool[li].astype(BF16)
            sp = b_scale[li].reshape(1, D_B)
            yb_p = _pool_call(z, wp, sp, rows=R_PROMPT, groups=P_GROUPS, row_block0=0)
            yb_s = _pool_call(z, wp, sp, rows=R_SAMPLE, groups=1, row_block0=1)
            w_out = ev_w_out[li].astype(BF16)
            x = _even_tail(x, (ya_p, ya_s), (yb_p, yb_s), w_out[:D_A], w_out[D_A:], gate1,
                           g2, shift2, scale2, gate2, ff_w1b, ff_w3b, ff_w2b, li)
            new_states.append(h_last.transpose(0, 2, 1, 3).reshape(BATCH, 2, D_A))
        else:
            qg = jnp.tile(q_norm[li], N_Q_HEADS).reshape(1, D_ATT)
            kg = jnp.tile(k_norm[li], N_KV_HEADS).reshape(1, D_KV)
            glu, *qs, k, v = _odd_in(x, g1, shift1, scale1, od_w_in[li].astype(BF16), bdq, bdk, qg, kg,
                                     cos_t, sin_t)
            att = _attention(qs, k, v, cache_k4[:, li], cache_v4[:, li])
            cw = c_conv_w[li]
            cb = c_conv_b[li].reshape(1, D_C)
            hc = (_conv_call(glu, cw, cb, rows=R_PROMPT, groups=P_GROUPS, row_block0=0),
                  _conv_call(glu, cw, cb, rows=R_SAMPLE, groups=1, row_block0=1))
            w_out = od_w_out[li].astype(BF16)
            wr = jnp.zeros((D, LANES), F32).at[:, :N_EXPERTS].set(moe_w_router[li])
            br = jnp.zeros((1, LANES), F32).at[0, :N_EXPERTS].set(moe_b_router[li])
            wr_hi = wr.astype(BF16)
            wr_lo = (wr - wr_hi.astype(F32)).astype(BF16)
            x1, xloc, route, counts = _odd_out(x, hc, att, c_ln_g[li].reshape(1, D_C), c_ln_b[li].reshape(1, D_C),
                                     w_out[:D_C], w_out[D_C:], gate1, g2, shift2, scale2, wr_hi, wr_lo, br)
            y2 = _moe_experts(_route_plan(counts), xloc, moe_w1b, moe_w3b, moe_w2b, li)
            pending = (x1, y2, route, gate2)
            new_k.append(_prompt_to_batch_major(k).reshape(BATCH, SEQ, N_KV_HEADS, HEAD_DIM))
            new_v.append(_prompt_to_batch_major(v).reshape(BATCH, SEQ, N_KV_HEADS, HEAD_DIM))

    gf = norm_f.reshape(1, D)
    y_prompt = _final_norm(pending, gf, batch=BATCH, steps=SEQ, stream=0)
    y_sample = _final_norm(pending, gf, batch=DEC_BATCH, steps=DEC_SEQ, stream=1)
    return (y_prompt, y_sample, jnp.stack(new_states, axis=1), jnp.stack(new_k, axis=1), jnp.stack(new_v, axis=1))
```

```python
import functools

import jax
import jax.numpy as jnp
from jax import lax
from jax.experimental import pallas as pl
from jax.experimental.pallas import tpu as pltpu

F32 = jnp.float32
BF16 = jnp.bfloat16

D = 1024
BATCH = 32
SEQ = 256
DEPTH = 4
DEC_BATCH = 8
DEC_SEQ = 1024
PAST_LEN = 256
GRID_W = 64
EPS = 1e-6
D_A = 512
A_BLOCKS = 8
A_BS = 64
A_CONV = 4
A_C = 8.0
D_B = 512
POOL_WINDOWS = (2, 4, 8, 16)
B_GS = 128
D_C = 512
C_CONV = 31
HEAD_DIM = 64
N_Q_HEADS = 8
N_KV_HEADS = 2
D_ATT = 512
D_KV = 128
ROPE_THETA = 10000.0
D_FF = 2816
N_EXPERTS = 8
D_FF_E = 1408
D_IN_EVEN = 1536
D_IN_ODD = 1792

SUB = 8
LANES = 128
N_PROMPT = BATCH * SEQ
N_SAMPLE = DEC_BATCH * DEC_SEQ
N_TOK = N_PROMPT + N_SAMPLE
P_GROUPS = BATCH // SUB
R_PROMPT = SEQ * SUB
R_SAMPLE = DEC_SEQ * SUB
TM = 512
TILES_PER_STREAM = N_PROMPT // TM
MOE_TM = 512
MOE_TILES = 2 * N_TOK // MOE_TM + N_EXPERTS
SLAB = D // LANES
NEG_BIG = -3.0e38
VMEM_LIMIT = 56 * 1024 * 1024


def _params(n_axes, vmem=VMEM_LIMIT):
    return pltpu.CompilerParams(dimension_semantics=("arbitrary",) * n_axes, vmem_limit_bytes=vmem)


def _rms(x, g):
    ms = jnp.mean(x * x, axis=-1, keepdims=True)
    return x * lax.rsqrt(ms + EPS) * g


def _modulate(xn, shift, scale):
    tm = xn.shape[0]
    h = xn.reshape(tm // SUB, SUB, D) * (1.0 + scale)[None] + shift[None]
    return h.reshape(tm, D)


def _gated_add(x, gate, y):
    tm = x.shape[0]
    return x + (y.reshape(tm // SUB, SUB, D) * gate[None]).reshape(tm, D)


def _mod_spec():
    return pl.BlockSpec((None, SUB, D), lambda i, *_: (i // TILES_PER_STREAM, 0, 0))


def _row_spec(width, tm=TM):
    return pl.BlockSpec((tm, width), lambda i, *_: (i, 0))


def _full_spec(shape):
    nd = len(shape)
    return pl.BlockSpec(shape, lambda i, *_: (0,) * nd)


def _ada_kernel(c_ref, w_ref, b_ref, o_ref):
    c = c_ref[...]
    s = (c * jax.nn.sigmoid(c)).astype(BF16)
    o_ref[...] = jnp.dot(s, w_ref[...].astype(BF16), preferred_element_type=F32) + b_ref[...]


def _ada_params(cond16, w_mod, b_mod):
    return pl.pallas_call(
        _ada_kernel,
        grid=(DEPTH, 6),
        in_specs=[
            pl.BlockSpec((16, D), lambda l, j: (0, 0)),
            pl.BlockSpec((None, D, D), lambda l, j: (l, 0, j)),
            pl.BlockSpec((None, None, 1, D), lambda l, j: (l, j, 0, 0)),
        ],
        out_specs=pl.BlockSpec((None, None, 16, D), lambda l, j: (l, j, 0, 0)),
        out_shape=jax.ShapeDtypeStruct((DEPTH, 6, 16, D), F32),
        compiler_params=_params(2),
        name="ada_params",
    )(cond16, w_mod, b_mod.reshape(DEPTH, 6, 1, D))


T_TILE = TM // SUB


def _first_in_kernel(xp_ref, xs_ref, g_ref, sh_ref, sc_ref, w_ref, z_ref, x_ref, cols):
    is_sample = pl.program_id(0) >= TILES_PER_STREAM
    for b in range(SUB):
        xb = jnp.where(is_sample, xs_ref[b], xp_ref[b])
        for c in range(SLAB):
            cols[c, pl.ds(b, T_TILE, stride=SUB), :] = xb[:, c * LANES:(c + 1) * LANES]
    x = jnp.concatenate([cols[c] for c in range(SLAB)], axis=1)
    x_ref[...] = x
    h = _modulate(_rms(x, g_ref[...]), sh_ref[...], sc_ref[...])
    z_ref[...] = jnp.dot(h.astype(BF16), w_ref[...], preferred_element_type=F32)


def _first_in(x_prompt, x_sample, g, shift, scale, w_bf16):
    t_tiles = SEQ // T_TILE
    last_p = TILES_PER_STREAM - 1
    return pl.pallas_call(
        _first_in_kernel,
        grid=(N_TOK // TM,),
        in_specs=[
            pl.BlockSpec((SUB, T_TILE, D), lambda i: (jnp.minimum(i, last_p) // t_tiles,
                                                      jnp.minimum(i, last_p) % t_tiles, 0)),
            pl.BlockSpec((SUB, T_TILE, D), lambda i: (0, jnp.maximum(i - TILES_PER_STREAM, 0), 0)),
            _full_spec((1, D)), _mod_spec(), _mod_spec(), _full_spec((D, D_IN_EVEN))],
        out_specs=[_row_spec(D_IN_EVEN), _row_spec(D)],
        out_shape=[jax.ShapeDtypeStruct((N_TOK, D_IN_EVEN), F32), jax.ShapeDtypeStruct((N_TOK, D), F32)],
        scratch_shapes=[pltpu.VMEM((SLAB, TM, LANES), F32)],
        compiler_params=_params(1),
        name="first_in",
    )(x_prompt, x_sample, g, shift, scale, w_bf16)


RG_CHUNK = 256
RG_PAD = 16


def _rglru_kernel(u_ref, ga_ref, cw_ref, cb_ref, wg_ref, bg_ref, lam_ref, h0_ref,
                  y_ref, hl_ref, xpad, a0, b0, a1, b1, *, rows):
    steps = rows // SUB
    n_chunks = rows // RG_CHUNK

    xpad[0:RG_PAD, :] = jnp.zeros((RG_PAD, LANES), F32)
    xpad[RG_PAD + rows:RG_PAD + rows + RG_PAD, :] = jnp.zeros((RG_PAD, LANES), F32)

    def copy_chunk(c, carry):
        r = pl.multiple_of(c * RG_CHUNK, RG_CHUNK)
        xpad[pl.ds(RG_PAD + r, RG_CHUNK), :] = u_ref[pl.ds(r, RG_CHUNK), :]
        return carry

    lax.fori_loop(0, n_chunks, copy_chunk, 0)

    lam = lam_ref[...]
    softplus_neg = jnp.maximum(-lam, 0.0) + jnp.log1p(jnp.exp(-jnp.abs(lam)))
    decay = A_C * softplus_neg
    decay_log2 = decay * (-1.4426950408889634)
    a_refs = (a0, a1)
    b_refs = (b0, b1)

    def gate_chunk(c, carry):
        r = pl.multiple_of(c * RG_CHUNK, RG_CHUNK)
        xc = jnp.zeros((RG_CHUNK, LANES), F32) + cb_ref[...]
        for k in range(A_CONV):
            xc = xc + cw_ref[k:k + 1, :] * xpad[pl.ds(r + SUB * k, RG_CHUNK), :]
        pre = jnp.dot(xc.astype(BF16), wg_ref[...], preferred_element_type=F32) + bg_ref[...]
        for d in range(2):
            rg = jax.nn.sigmoid(pre[:, d * 256:d * 256 + LANES])
            ig = jax.nn.sigmoid(pre[:, d * 256 + LANES:(d + 1) * 256])
            a = jnp.exp2(rg * decay_log2[:, d * LANES:(d + 1) * LANES])
            one_minus_a2 = jnp.tanh(rg * decay[:, d * LANES:(d + 1) * LANES]) * (a * a + 1.0)
            root = one_minus_a2 * lax.rsqrt(jnp.maximum(one_minus_a2, 1e-30))
            a_refs[d][pl.ds(r, RG_CHUNK), :] = a
            b_refs[d][pl.ds(r, RG_CHUNK), :] = root * (ig * xc)
        return carry

    lax.fori_loop(0, n_chunks, gate_chunk, 0)

    def step(t, carry):
        hf, hb = carry
        rf = pl.multiple_of(t * SUB, SUB)
        rb = pl.multiple_of((steps - 1 - t) * SUB, SUB)
        hf = a0[pl.ds(rf, SUB), :] * hf + b0[pl.ds(rf, SUB), :]
        b0[pl.ds(rf, SUB), :] = hf
        hb = a1[pl.ds(rb, SUB), :] * hb + b1[pl.ds(rb, SUB), :]
        b1[pl.ds(rb, SUB), :] = hb
        return hf, hb

    hf, hb = lax.fori_loop(0, steps, step, (h0_ref[0], h0_ref[1]), unroll=8)
    hl_ref[0] = hf
    hl_ref[1] = hb

    def out_chunk(c, carry):
        r = pl.multiple_of(c * RG_CHUNK, RG_CHUNK)
        y = (b0[pl.ds(r, RG_CHUNK), :] + b1[pl.ds(r, RG_CHUNK), :]) * jax.nn.gelu(ga_ref[pl.ds(r, RG_CHUNK), :])
        y_ref[pl.ds(r, RG_CHUNK), :] = y.astype(BF16)
        return carry

    lax.fori_loop(0, n_chunks, out_chunk, 0)


def _rglru_call(z, cw, cb, wg, bg, lam, h0, *, rows, groups, row_block0):
    n_cb = D_A // LANES
    in_specs = [
        pl.BlockSpec((rows, LANES), lambda g, j: (row_block0 + g, j)),
        pl.BlockSpec((rows, LANES), lambda g, j: (row_block0 + g, n_cb + j)),
        pl.BlockSpec((A_CONV, LANES), lambda g, j: (0, j)),
        pl.BlockSpec((1, LANES), lambda g, j: (0, j)),
        pl.BlockSpec((None, LANES, 4 * LANES), lambda g, j: (j, 0, 0)),
        pl.BlockSpec((None, 1, 4 * LANES), lambda g, j: (j, 0, 0)),
        pl.BlockSpec((None, 1, 2 * LANES), lambda g, j: (j, 0, 0)),
        pl.BlockSpec((None, 2, SUB, LANES), lambda g, j: (g, 0, 0, j)),
    ]
    return pl.pallas_call(
        functools.partial(_rglru_kernel, rows=rows),
        grid=(groups, n_cb),
        in_specs=in_specs,
        out_specs=[
            pl.BlockSpec((rows, LANES), lambda g, j: (g, j)),
            pl.BlockSpec((None, 2, SUB, LANES), lambda g, j: (g, 0, 0, j)),
        ],
        out_shape=[
            jax.ShapeDtypeStruct((groups * rows, D_A), BF16),
            jax.ShapeDtypeStruct((groups, 2, SUB, D_A), F32),
        ],
        scratch_shapes=[pltpu.VMEM((rows + 2 * RG_PAD, LANES), F32)] + [pltpu.VMEM((rows, LANES), F32)] * 4,
        compiler_params=_params(2),
        name=f"rglru_{rows}",
    )(z, z, cw, cb, wg, bg, lam, h0)


POOL_CHUNK = 256
POOL_PAD = 64


def _pool_kernel(u_ref, w_ref, s_ref, y_ref, xpad, *, rows):
    steps = rows // SUB
    n_chunks = rows // POOL_CHUNK
    j = pl.program_id(1)

    xpad[0:POOL_PAD, :] = jnp.zeros((POOL_PAD, LANES), F32)
    xpad[POOL_PAD + rows:POOL_PAD + rows + POOL_PAD, :] = jnp.zeros((POOL_PAD, LANES), F32)

    def copy_chunk(c, carry):
        r = pl.multiple_of(c * POOL_CHUNK, POOL_CHUNK)
        xpad[pl.ds(POOL_PAD + r, POOL_CHUNK), :] = u_ref[pl.ds(r, POOL_CHUNK), :]
        return carry

    lax.fori_loop(0, n_chunks, copy_chunk, 0)

    for gi, win in enumerate(POOL_WINDOWS):
        half = win // 2

        @pl.when(j == gi)
        def _(half=half, win=win):
            def chunk(c, carry):
                r = pl.multiple_of(c * POOL_CHUNK, POOL_CHUNK)
                acc = xpad[pl.ds(POOL_PAD + r - SUB * half, POOL_CHUNK), :]
                for s in range(1, win):
                    acc = acc + xpad[pl.ds(POOL_PAD + r + SUB * (s - half), POOL_CHUNK), :]
                row = r + lax.broadcasted_iota(jnp.int32, (POOL_CHUNK, LANES), 0)
                t = lax.shift_right_logical(row, SUB.bit_length() - 1)
                cnt = jnp.minimum(t + half, steps) - jnp.maximum(t - half, 0)
                dlt = acc / cnt.astype(F32) - xpad[pl.ds(POOL_PAD + r, POOL_CHUNK), :]
                y = jnp.dot(dlt.astype(BF16), w_ref[...], preferred_element_type=F32) * s_ref[...]
                y_ref[pl.ds(r, POOL_CHUNK), :] = y.astype(BF16)
                return carry

            lax.fori_loop(0, n_chunks, chunk, 0)


def _pool_call(z, w_pool_bf16, s_pool, *, rows, groups, row_block0):
    n_cb = D_B // LANES
    col0 = 2 * D_A // LANES
    in_specs = [
        pl.BlockSpec((rows, LANES), lambda g, j: (row_block0 + g, col0 + j)),
        pl.BlockSpec((None, B_GS, B_GS), lambda g, j: (j, 0, 0)),
        pl.BlockSpec((1, LANES), lambda g, j: (0, j)),
    ]
    return pl.pallas_call(
        functools.partial(_pool_kernel, rows=rows),
        grid=(groups, n_cb),
        in_specs=in_specs,
        out_specs=pl.BlockSpec((rows, LANES), lambda g, j: (g, j)),
        out_shape=jax.ShapeDtypeStruct((groups * rows, D_B), BF16),
        scratch_shapes=[pltpu.VMEM((rows + 2 * POOL_PAD, LANES), F32)],
        compiler_params=_params(2),
        name=f"pool_{rows}",
    )(z, w_pool_bf16, s_pool)


def _stream_specs(width):
    prompt = pl.BlockSpec((TM, width), lambda i, *_: (jnp.minimum(i, TILES_PER_STREAM - 1), 0))
    sample = pl.BlockSpec((TM, width), lambda i, *_: (jnp.maximum(i - TILES_PER_STREAM, 0), 0))
    return [prompt, sample]


def _pick_stream(prompt_ref, sample_ref):
    return jnp.where(pl.program_id(0) >= TILES_PER_STREAM, sample_ref[...], prompt_ref[...])


def _even_tail_kernel(x_ref, yap_ref, yas_ref, ybp_ref, ybs_ref, wa_ref, wb_ref, gate1_ref,
                      g2_ref, sh_ref, sc_ref, gate2_ref, w1_ref, w3_ref, w2_ref, o_ref):
    y = jnp.dot(_pick_stream(yap_ref, yas_ref), wa_ref[...], preferred_element_type=F32)
    y = y + jnp.dot(_pick_stream(ybp_ref, ybs_ref), wb_ref[...], preferred_element_type=F32)
    x1 = _gated_add(x_ref[...], gate1_ref[...], y)
    h = _modulate(_rms(x1, g2_ref[...]), sh_ref[...], sc_ref[...]).astype(BF16)
    a = jnp.dot(h, w1_ref[...], preferred_element_type=F32)
    b = jnp.dot(h, w3_ref[...], preferred_element_type=F32)
    u = (a * jax.nn.sigmoid(a) * b).astype(BF16)
    f = jnp.dot(u, w2_ref[...], preferred_element_type=F32)
    o_ref[...] = _gated_add(x1, gate2_ref[...], f)


def _even_tail(x, ya, yb, wa, wb, gate1, g2, shift2, scale2, gate2, w1, w3, w2, li):
    half = D // 2
    resident = pl.Buffered(1)
    return pl.pallas_call(
        _even_tail_kernel,
        grid=(N_TOK // TM,),
        in_specs=[_row_spec(D)] + _stream_specs(half) + _stream_specs(half) +
                 [pl.BlockSpec((half, D), lambda i: (0, 0), pipeline_mode=resident),
                  pl.BlockSpec((half, D), lambda i: (0, 0), pipeline_mode=resident),
                  _mod_spec(), _full_spec((1, D)), _mod_spec(), _mod_spec(), _mod_spec(),
                  pl.BlockSpec((None, D, D_FF), lambda i: (li, 0, 0), pipeline_mode=resident),
                  pl.BlockSpec((None, D, D_FF), lambda i: (li, 0, 0), pipeline_mode=resident),
                  pl.BlockSpec((None, D_FF, D), lambda i: (li, 0, 0), pipeline_mode=resident)],
        out_specs=_row_spec(D),
        out_shape=jax.ShapeDtypeStruct((N_TOK, D), F32),
        compiler_params=_params(1),
        name="even_tail",
    )(x, *ya, *yb, wa, wb, gate1, g2, shift2, scale2, gate2, w1, w3, w2)


def _head_rms(x, ones_bd, g):
    sq = x * x
    hi = sq.astype(BF16)
    lo = (sq - hi.astype(F32)).astype(BF16)
    ms = jnp.dot(hi, ones_bd, preferred_element_type=F32) + jnp.dot(lo, ones_bd, preferred_element_type=F32)
    return x * lax.rsqrt(ms + EPS) * g


def _rope(x, cos, sin_signed):
    w = x.shape[-1]
    lane = lax.broadcasted_iota(jnp.int32, x.shape, 1)
    first = (lane % 32) < 16
    partner = jnp.where(first, pltpu.roll(x, w - 16, 1), pltpu.roll(x, 16, 1))
    return x * cos + partner * sin_signed


Q_BLOCKS = D_ATT // LANES


def _odd_in_kernel(x_ref, g_ref, sh_ref, sc_ref, w_ref, bdq_ref, bdk_ref, qg_ref, kg_ref, cos_ref, sin_ref,
                   glu_ref, q0_ref, q1_ref, q2_ref, q3_ref, k_ref, v_ref):
    i = pl.program_id(0)
    h = _modulate(_rms(x_ref[...], g_ref[...]), sh_ref[...], sc_ref[...])
    z = jnp.dot(h.astype(BF16), w_ref[...], preferred_element_type=F32)
    glu_ref[...] = z[:, :D_C] * jax.nn.sigmoid(z[:, D_C:2 * D_C])
    o1 = 2 * D_C
    o2 = o1 + D_ATT
    o3 = o2 + D_KV
    q = _head_rms(z[:, o1:o2], bdq_ref[...], qg_ref[...])
    k = _head_rms(z[:, o2:o3], bdk_ref[...], kg_ref[...])
    cos = cos_ref[...]
    sin = sin_ref[...]
    is_sample = i >= TILES_PER_STREAM
    q_r = _rope(q, jnp.concatenate([cos] * 4, axis=1), jnp.concatenate([sin] * 4, axis=1))
    k_r = _rope(k, cos, sin)
    q = jnp.where(is_sample, q_r, q)
    for c, q_ref in enumerate((q0_ref, q1_ref, q2_ref, q3_ref)):
        q_ref[...] = q[:, c * LANES:(c + 1) * LANES]
    k_ref[...] = jnp.where(is_sample, k_r, k)
    v_ref[...] = z[:, o3:]


def _odd_in(x, g, shift, scale, w_bf16, bdq, bdk, qg, kg, cos_t, sin_t):
    rope_spec = pl.BlockSpec((TM, D_KV), lambda i: (jnp.maximum(i - TILES_PER_STREAM, 0), 0))
    return pl.pallas_call(
        _odd_in_kernel,
        grid=(N_TOK // TM,),
        in_specs=[_row_spec(D), _full_spec((1, D)), _mod_spec(), _mod_spec(), _full_spec((D, D_IN_ODD)),
                  _full_spec((D_ATT, D_ATT)), _full_spec((D_KV, D_KV)), _full_spec((1, D_ATT)),
                  _full_spec((1, D_KV)), rope_spec, rope_spec],
        out_specs=[_row_spec(D_C)] + [_row_spec(LANES)] * (Q_BLOCKS + 2),
        out_shape=[jax.ShapeDtypeStruct((N_TOK, D_C), F32)] +
                  [jax.ShapeDtypeStruct((N_TOK, LANES), F32)] * (Q_BLOCKS + 2),
        compiler_params=_params(1),
        name="odd_in",
    )(x, g, shift, scale, w_bf16, bdq, bdk, qg, kg, cos_t, sin_t)


def _attend(q_refs, o_refs, q_rows, k_all, v_all):
    scale = HEAD_DIM ** -0.5 * 1.4426950408889634
    heads_per_block = LANES // HEAD_DIM
    group = N_Q_HEADS // N_KV_HEADS
    kv = []
    for h in range(N_KV_HEADS):
        kv.append((k_all[:, h * HEAD_DIM:(h + 1) * HEAD_DIM].astype(BF16),
                   v_all[:, h * HEAD_DIM:(h + 1) * HEAD_DIM].astype(BF16)))
    for c in range(Q_BLOCKS):
        qc = (q_refs[c][q_rows, :] * scale).astype(BF16)
        outs = []
        for sub in range(heads_per_block):
            kh, vh = kv[(c * heads_per_block + sub) // group]
            qh = qc[:, sub * HEAD_DIM:(sub + 1) * HEAD_DIM]
            s = lax.dot_general(qh, kh, (((1,), (1,)), ((), ())), preferred_element_type=F32)
            m = jnp.max(s, axis=-1, keepdims=True)
            p = jnp.exp2(s - m)
            l = jnp.sum(p, axis=-1, keepdims=True)
            outs.append(jnp.dot(p.astype(BF16), vh, preferred_element_type=F32) / l)
        o_refs[c][q_rows, :] = jnp.concatenate(outs, axis=1)


def _attn_prompt_kernel(*refs):
    q_refs, (k_ref, v_ref), o_refs = refs[:Q_BLOCKS], refs[Q_BLOCKS:Q_BLOCKS + 2], refs[Q_BLOCKS + 2:]
    rows = pl.ds(pl.program_id(1), SEQ, stride=SUB)
    _attend(q_refs, o_refs, rows, k_ref[rows, :], v_ref[rows, :])


def _attn_sample_kernel(*refs):
    q_refs, (k_ref, v_ref, ck_ref, cv_ref), o_refs = refs[:Q_BLOCKS], refs[Q_BLOCKS:Q_BLOCKS + 4], refs[Q_BLOCKS + 4:]
    b = pl.program_id(1)
    kv_rows = pl.ds(b, DEC_SEQ, stride=SUB)
    k_all = jnp.concatenate([ck_ref[...], k_ref[kv_rows, :]], axis=0)
    v_all = jnp.concatenate([cv_ref[...], v_ref[kv_rows, :]], axis=0)
    _attend(q_refs, o_refs, pl.ds(b, SEQ, stride=SUB), k_all, v_all)


def _attention(qs, k, v, cache_k_l, cache_v_l):
    chunk = pl.BlockSpec((R_PROMPT, LANES), lambda g, b: (g, 0))
    out_shape = [jax.ShapeDtypeStruct((N_PROMPT, LANES), F32)] * Q_BLOCKS
    att_p = pl.pallas_call(
        _attn_prompt_kernel,
        grid=(P_GROUPS, SUB),
        in_specs=[chunk] * (Q_BLOCKS + 2),
        out_specs=[chunk] * Q_BLOCKS,
        out_shape=out_shape,
        compiler_params=_params(2),
        name="attn_prompt",
    )(*qs, k, v)
    q_chunks = DEC_SEQ // SEQ
    q_chunk = pl.BlockSpec((R_PROMPT, LANES), lambda c, b: (P_GROUPS + c, 0))
    kv_all = pl.BlockSpec((R_SAMPLE, LANES), lambda c, b: (1, 0))
    cache = pl.BlockSpec((None, PAST_LEN, D_KV), lambda c, b: (b, 0, 0))
    att_s = pl.pallas_call(
        _attn_sample_kernel,
        grid=(q_chunks, DEC_BATCH),
        in_specs=[q_chunk] * Q_BLOCKS + [kv_all, kv_all, cache, cache],
        out_specs=[pl.BlockSpec((R_PROMPT, LANES), lambda c, b: (c, 0))] * Q_BLOCKS,
        out_shape=out_shape,
        compiler_params=_params(2),
        name="attn_sample",
    )(*qs, k, v, cache_k_l, cache_v_l)
    return att_p, att_s


CV_CHUNK = 64
CV_PAD = (C_CONV // 2) * SUB


def _conv_kernel(u_ref, w_ref, b_ref, y_ref, xpad, *, rows):
    n_copy = rows // 256
    xpad[0:CV_PAD, :] = jnp.zeros((CV_PAD, LANES), F32)
    xpad[CV_PAD + rows:CV_PAD + rows + CV_PAD, :] = jnp.zeros((CV_PAD, LANES), F32)

    def copy_chunk(c, carry):
        r = pl.multiple_of(c * 256, 256)
        xpad[pl.ds(CV_PAD + r, 256), :] = u_ref[pl.ds(r, 256), :]
        return carry

    lax.fori_loop(0, n_copy, copy_chunk, 0)

    def chunk(c, carry):
        r = pl.multiple_of(c * CV_CHUNK, CV_CHUNK)
        acc = jnp.zeros((CV_CHUNK, LANES), F32) + b_ref[...]
        for k in range(C_CONV):
            acc = acc + w_ref[k:k + 1, :] * xpad[pl.ds(r + SUB * k, CV_CHUNK), :]
        y_ref[pl.ds(r, CV_CHUNK), :] = acc
        return carry

    lax.fori_loop(0, rows // CV_CHUNK, chunk, 0)


def _conv_call(glu, w, b, *, rows, groups, row_block0):
    n_cb = D_C // LANES
    in_specs = [
        pl.BlockSpec((rows, LANES), lambda g, j: (row_block0 + g, j)),
        pl.BlockSpec((C_CONV, LANES), lambda g, j: (0, j)),
        pl.BlockSpec((1, LANES), lambda g, j: (0, j)),
    ]
    return pl.pallas_call(
        functools.partial(_conv_kernel, rows=rows),
        grid=(groups, n_cb),
        in_specs=in_specs,
        out_specs=pl.BlockSpec((rows, LANES), lambda g, j: (g, j)),
        out_shape=jax.ShapeDtypeStruct((groups * rows, D_C), F32),
        scratch_shapes=[pltpu.VMEM((rows + 2 * CV_PAD, LANES), F32)],
        compiler_params=_params(2),
        name=f"conv_{rows}",
    )(glu, w, b)


def _odd_out_kernel(x_ref, hcp_ref, hcs_ref, *refs):
    att_refs, refs = refs[:2 * Q_BLOCKS], refs[2 * Q_BLOCKS:]
    (lng_ref, lnb_ref, wc_ref, wa_ref, gate_ref, g2_ref, sh_ref, sc_ref, wrh_ref, wrl_ref, br_ref, tril_ref,
     x1_ref, xloc_ref, route_ref, cnt_ref) = refs
    att = jnp.concatenate([_pick_stream(att_refs[c], att_refs[Q_BLOCKS + c]) for c in range(Q_BLOCKS)], axis=1)
    hc = _pick_stream(hcp_ref, hcs_ref)
    mu = jnp.mean(hc, axis=-1, keepdims=True)
    xc = hc - mu
    var = jnp.mean(xc * xc, axis=-1, keepdims=True)
    ln = xc * lax.rsqrt(var + EPS) * lng_ref[...] + lnb_ref[...]
    yc = (ln * jax.nn.sigmoid(ln)).astype(BF16)
    y = jnp.dot(yc, wc_ref[...], preferred_element_type=F32)
    y = y + jnp.dot(att.astype(BF16), wa_ref[...], preferred_element_type=F32)
    x1 = _gated_add(x_ref[...], gate_ref[...], y)
    x1_ref[...] = x1
    h2 = _modulate(_rms(x1, g2_ref[...]), sh_ref[...], sc_ref[...])
    h2_hi = h2.astype(BF16)
    h2_lo = (h2 - h2_hi.astype(F32)).astype(BF16)
    logits = (jnp.dot(h2_hi, wrh_ref[...], preferred_element_type=F32)
              + jnp.dot(h2_lo, wrh_ref[...], preferred_element_type=F32)
              + jnp.dot(h2_hi, wrl_ref[...], preferred_element_type=F32)) + br_ref[...]
    lane = lax.broadcasted_iota(jnp.int32, logits.shape, 1).astype(F32)
    lg = jnp.where(lane < N_EXPERTS, logits, NEG_BIG)
    m1 = jnp.max(lg, axis=-1, keepdims=True)
    i1 = jnp.min(jnp.where(lg == m1, lane, float(LANES)), axis=-1, keepdims=True)
    lg2 = jnp.where(lane == i1, NEG_BIG, lg)
    m2 = jnp.max(lg2, axis=-1, keepdims=True)
    i2 = jnp.min(jnp.where(lg2 == m2, lane, float(LANES)), axis=-1, keepdims=True)
    e = jnp.exp(m2 - m1)
    den = 1.0 + e

    e1 = jnp.where(lane == i1, 1.0, 0.0)
    e2 = jnp.where(lane == i2, 1.0, 0.0)
    before1 = jnp.dot(tril_ref[...], e1.astype(BF16), preferred_element_type=F32)
    before2 = jnp.dot(tril_ref[...], e2.astype(BF16), preferred_element_type=F32)
    cnt1 = jnp.sum(e1, axis=0, keepdims=True)
    cnt2 = jnp.sum(e2, axis=0, keepdims=True)
    start = (jnp.sum(jnp.where(i1 < lane, 1.0, 0.0), axis=0, keepdims=True)
             + jnp.sum(jnp.where(i2 < lane, 1.0, 0.0), axis=0, keepdims=True))
    pos0 = jnp.sum(e1 * (start + before1), axis=-1, keepdims=True)
    pos1 = jnp.sum(e2 * (start + cnt1 + before2), axis=-1, keepdims=True)
    route = jnp.where(lane == 0.0, i1, jnp.where(lane == 1.0, i2,
                      jnp.where(lane == 2.0, 1.0 / den, jnp.where(lane == 3.0, e / den,
                                jnp.where(lane == 4.0, pos0, jnp.where(lane == 5.0, pos1, 0.0))))))
    route_ref[...] = route
    cnt_ref[...] = jnp.broadcast_to(cnt1 + cnt2, (SUB, LANES))

    q = lax.broadcasted_iota(jnp.int32, (TM, 2 * TM), 1).astype(F32)
    sel = jnp.where(jnp.logical_or(q == pos0, q == pos1), 1.0, 0.0).astype(BF16)
    xloc = lax.dot_general(sel, h2_hi, (((0,), (0,)), ((), ())), preferred_element_type=F32)
    for j in range(SLAB):
        xloc_ref[pl.ds(j, 2 * TM, stride=SLAB), :] = xloc[:, j * LANES:(j + 1) * LANES]


def _odd_out(x, hc, att, lng, lnb, wc, wa, gate, g2, shift2, scale2, wr_hi, wr_lo, br_pad):
    half = D // 2
    att_specs = [_stream_specs(LANES)[0]] * Q_BLOCKS + [_stream_specs(LANES)[1]] * Q_BLOCKS
    return pl.pallas_call(
        _odd_out_kernel,
        grid=(N_TOK // TM,),
        in_specs=[_row_spec(D)] + _stream_specs(half) + att_specs +
                 [_full_spec((1, half)), _full_spec((1, half)),
                  _full_spec((half, D)), _full_spec((half, D)), _mod_spec(),
                  _full_spec((1, D)), _mod_spec(), _mod_spec(),
                  _full_spec((D, LANES)), _full_spec((D, LANES)), _full_spec((1, LANES)), _full_spec((TM, TM))],
        out_specs=[_row_spec(D), pl.BlockSpec((2 * TM * SLAB, LANES), lambda i: (i, 0)), _row_spec(LANES),
                   pl.BlockSpec((SUB, LANES), lambda i: (i, 0))],
        out_shape=[jax.ShapeDtypeStruct((N_TOK, D), F32), jax.ShapeDtypeStruct((2 * N_TOK * SLAB, LANES), F32),
                   jax.ShapeDtypeStruct((N_TOK, LANES), F32),
                   jax.ShapeDtypeStruct((N_TOK // TM * SUB, LANES), F32)],
        compiler_params=_params(1),
        name="odd_out",
    )(x, *hc, *att[0], *att[1], lng, lnb, wc, wa, gate, g2, shift2, scale2, wr_hi, wr_lo, br_pad,
      jnp.tril(jnp.ones((TM, TM), BF16), -1))


N_CHUNKS = N_TOK // TM


def _moe_kernel(te_ref, nv_ref, nu_ref, c0_ref, c1_ref, src_ref, dst_ref, len_ref,
                xloc_hbm, w1_ref, w3_ref, w2_ref, yloc_hbm, xbuf, obuf, gsem, ssem):
    del te_ref
    i = pl.program_id(0)
    n_used = nu_ref[0]
    slot = i % 2
    other = 1 - slot

    def rows(start, n):
        return pl.ds(pl.multiple_of(start * SLAB, SLAB), n * SLAB)

    def for_pieces(tile, fn):
        def body(c, carry):
            k = tile * N_CHUNKS + c
            n = len_ref[k]

            @pl.when(n > 0)
            def _():
                fn(src_ref[k], dst_ref[k], n)

            return carry

        lax.fori_loop(c0_ref[tile], c1_ref[tile], body, 0)

    def start_gather(tile, s):
        for_pieces(tile, lambda src, dst, n: pltpu.make_async_copy(
            xloc_hbm.at[rows(src, n), :], xbuf.at[s, rows(dst, n), :], gsem.at[s]).start())

    def start_scatter(tile, s):
        for_pieces(tile, lambda src, dst, n: pltpu.make_async_copy(
            obuf.at[s, rows(dst, n), :], yloc_hbm.at[rows(src, n), :], ssem.at[s]).start())

    def wait_gather(s, n):
        pltpu.make_async_copy(xloc_hbm.at[rows(0, n), :], xbuf.at[s, rows(0, n), :], gsem.at[s]).wait()

    def wait_scatter(s, n):
        pltpu.make_async_copy(obuf.at[s, rows(0, n), :], yloc_hbm.at[rows(0, n), :], ssem.at[s]).wait()

    @pl.when(i == 0)
    def _():
        xbuf[...] = jnp.zeros(xbuf.shape, F32)
        start_gather(0, 0)

    @pl.when(i < n_used)
    def _():
        wait_gather(slot, nv_ref[i])

        @pl.when(i + 1 < n_used)
        def _():
            start_gather(i + 1, other)

        @pl.when(i >= 2)
        def _():
            wait_scatter(slot, nv_ref[jnp.maximum(i - 2, 0)])

        x = jnp.concatenate([xbuf[slot, pl.ds(j, MOE_TM, stride=SLAB), :] for j in range(SLAB)], axis=1)
        h = x.astype(BF16)
        a = jnp.dot(h, w1_ref[...], preferred_element_type=F32)
        b = jnp.dot(h, w3_ref[...], preferred_element_type=F32)
        u = (a * jax.nn.sigmoid(a) * b).astype(BF16)
        y = jnp.dot(u, w2_ref[...], preferred_element_type=F32)
        for j in range(SLAB):
            obuf[slot, pl.ds(j, MOE_TM, stride=SLAB), :] = y[:, j * LANES:(j + 1) * LANES]
        start_scatter(i, slot)

        @pl.when(i == n_used - 1)
        def _():
            @pl.when(i >= 1)
            def _():
                wait_scatter(other, nv_ref[jnp.maximum(i - 1, 0)])

            wait_scatter(slot, nv_ref[i])


def _moe_experts(plan, xloc, w1, w3, w2, li):
    def weight(rows, cols):
        return pl.BlockSpec((None, None, rows, cols), lambda i, te, *_: (li, te[i], 0, 0))

    grid_spec = pltpu.PrefetchScalarGridSpec(
        num_scalar_prefetch=len(plan),
        grid=(MOE_TILES,),
        in_specs=[pl.BlockSpec(memory_space=pl.ANY),
                  weight(D, D_FF_E), weight(D, D_FF_E), weight(D_FF_E, D)],
        out_specs=pl.BlockSpec(memory_space=pl.ANY),
        scratch_shapes=[
            pltpu.VMEM((2, MOE_TM * SLAB, LANES), F32),
            pltpu.VMEM((2, MOE_TM * SLAB, LANES), F32),
            pltpu.SemaphoreType.DMA((2,)),
            pltpu.SemaphoreType.DMA((2,)),
        ],
    )
    return pl.pallas_call(
        _moe_kernel,
        grid_spec=grid_spec,
        out_shape=jax.ShapeDtypeStruct((2 * N_TOK * SLAB, LANES), F32),
        compiler_params=_params(1),
        name="moe_experts",
    )(*plan, xloc, w1, w3, w2)


def _route_plan(counts):
    cnt = counts.reshape(N_CHUNKS, SUB, LANES)[:, 0, :N_EXPERTS].astype(jnp.int32)
    run_local = jnp.cumsum(cnt, axis=1) - cnt
    run_group = jnp.cumsum(cnt, axis=0) - cnt
    total = jnp.sum(cnt, axis=0)
    tiles = (total + MOE_TM - 1) // MOE_TM
    tile_end = jnp.cumsum(tiles)
    tile_ids = jnp.arange(MOE_TILES, dtype=jnp.int32)
    tile_expert = jnp.minimum(jnp.sum((tile_ids[:, None] >= tile_end[None, :]).astype(jnp.int32), axis=1),
                              N_EXPERTS - 1)
    first = (tile_ids - (tile_end - tiles)[tile_expert]) * MOE_TM
    used = tile_ids < tile_end[-1]
    tile_valid = jnp.where(used, jnp.clip(total[tile_expert] - first, 0, MOE_TM), 0).astype(jnp.int32)
    a = run_group[:, tile_expert].T
    n = cnt[:, tile_expert].T
    lo = jnp.maximum(a, first[:, None])
    hi = jnp.minimum(a + n, first[:, None] + MOE_TM)
    length = jnp.where(used[:, None], jnp.maximum(hi - lo, 0), 0)
    chunk_row0 = jnp.arange(N_CHUNKS, dtype=jnp.int32)[None, :] * (2 * TM)
    src = jnp.where(length > 0, chunk_row0 + run_local[:, tile_expert].T + (lo - a), 0)
    dst = jnp.where(length > 0, lo - first[:, None], 0)
    n_used = tile_end[-1:].astype(jnp.int32)
    has = length > 0
    c_first = jnp.argmax(has, axis=1).astype(jnp.int32)
    c_stop = jnp.where(jnp.any(has, axis=1), N_CHUNKS - jnp.argmax(has[:, ::-1], axis=1), 0).astype(jnp.int32)
    flat = lambda t: t.astype(jnp.int32).reshape(-1)
    return tile_expert, tile_valid, n_used, c_first, c_stop, flat(src), flat(dst), flat(length)


def _moe_combined(x_ref, y_ref, route_ref, gate_ref):
    y = jnp.concatenate([y_ref[pl.ds(j, 2 * TM, stride=SLAB), :] for j in range(SLAB)], axis=1).astype(BF16)
    q = lax.broadcasted_iota(jnp.int32, (TM, 2 * TM), 1).astype(F32)
    pick = jnp.where(q == route_ref[:, 4:5], route_ref[:, 2:3],
                     jnp.where(q == route_ref[:, 5:6], route_ref[:, 3:4], 0.0)).astype(BF16)
    mix = jnp.dot(pick, y, preferred_element_type=F32)
    return _gated_add(x_ref[...], gate_ref[...], mix)


def _moe_pending_specs(tile0=0, stream=None):
    gate = _mod_spec() if stream is None else pl.BlockSpec((None, SUB, D), lambda i: (stream, 0, 0))
    return [pl.BlockSpec((TM, D), lambda i: (tile0 + i, 0)),
            pl.BlockSpec((TM * 2 * SLAB, LANES), lambda i: (tile0 + i, 0)),
            pl.BlockSpec((TM, LANES), lambda i: (tile0 + i, 0)), gate]


def _even_in_moe_kernel(x1_ref, y_ref, route_ref, gate_ref, g_ref, sh_ref, sc_ref, w_ref, z_ref, x_ref):
    x = _moe_combined(x1_ref, y_ref, route_ref, gate_ref)
    x_ref[...] = x
    h = _modulate(_rms(x, g_ref[...]), sh_ref[...], sc_ref[...])
    z_ref[...] = jnp.dot(h.astype(BF16), w_ref[...], preferred_element_type=F32)


def _even_in_moe(pending, g, shift, scale, w_bf16):
    return pl.pallas_call(
        _even_in_moe_kernel,
        grid=(N_TOK // TM,),
        in_specs=_moe_pending_specs() + [_full_spec((1, D)), _mod_spec(), _mod_spec(), _full_spec((D, D_IN_EVEN))],
        out_specs=[_row_spec(D_IN_EVEN), _row_spec(D)],
        out_shape=[jax.ShapeDtypeStruct((N_TOK, D_IN_EVEN), F32), jax.ShapeDtypeStruct((N_TOK, D), F32)],
        compiler_params=_params(1),
        name="even_in_moe",
    )(*pending, g, shift, scale, w_bf16)


def _final_kernel(x1_ref, y_ref, route_ref, gate_ref, g_ref, o_ref, cols):
    y = _rms(_moe_combined(x1_ref, y_ref, route_ref, gate_ref), g_ref[...])
    for c in range(SLAB):
        cols[c] = y[:, c * LANES:(c + 1) * LANES]
    for b in range(SUB):
        for c in range(SLAB):
            o_ref[b, :, c * LANES:(c + 1) * LANES] = cols[c, pl.ds(b, T_TILE, stride=SUB), :]


def _final_norm(pending, g, *, batch, steps, stream):
    t_tiles = steps // T_TILE
    return pl.pallas_call(
        _final_kernel,
        grid=(batch * steps // TM,),
        in_specs=_moe_pending_specs(stream * TILES_PER_STREAM, stream) + [_full_spec((1, D))],
        out_specs=pl.BlockSpec((SUB, T_TILE, D), lambda i: (i // t_tiles, i % t_tiles, 0)),
        out_shape=jax.ShapeDtypeStruct((batch, steps, D), F32),
        scratch_shapes=[pltpu.VMEM((SLAB, TM, LANES), F32)],
        compiler_params=_params(1),
        name=f"final_norm_{steps}",
    )(*pending, g)


def _block_diag_gates(w_r, w_i, b_r, b_i):
    def bd(w):
        w4 = w.reshape(4, 2, A_BS, A_BS)
        z = jnp.zeros((4, A_BS, A_BS), w.dtype)
        top = jnp.concatenate([w4[:, 0], z], axis=2)
        bot = jnp.concatenate([z, w4[:, 1]], axis=2)
        return jnp.concatenate([top, bot], axis=1)

    wg = jnp.concatenate([bd(w_r[0]), bd(w_i[0]), bd(w_r[1]), bd(w_i[1])], axis=2)
    bg = jnp.concatenate([b_r[0].reshape(4, 1, LANES), b_i[0].reshape(4, 1, LANES),
                          b_r[1].reshape(4, 1, LANES), b_i[1].reshape(4, 1, LANES)], axis=2)
    return wg.astype(BF16), bg


def _head_mean_matrix(width):
    idx = jnp.arange(width) // HEAD_DIM
    return ((idx[:, None] == idx[None, :]).astype(F32) / HEAD_DIM).astype(BF16)


def _rope_tables():
    pos = jnp.arange(DEC_SEQ)
    row = (pos // GRID_W).astype(F32)
    col = (pos % GRID_W).astype(F32)
    n_freq = HEAD_DIM // 4
    inv = ROPE_THETA ** (-jnp.arange(n_freq, dtype=F32) / n_freq)
    ang = jnp.stack([row[:, None] * inv, col[:, None] * inv], axis=1)
    cos = jnp.cos(ang)
    sin = jnp.sin(ang)
    cos_h = jnp.stack([cos, cos], axis=2).reshape(DEC_SEQ, HEAD_DIM)
    sin_h = jnp.stack([-sin, sin], axis=2).reshape(DEC_SEQ, HEAD_DIM)
    cos_t = jnp.tile(cos_h, (1, 2))
    sin_t = jnp.tile(sin_h, (1, 2))
    cos_t = jnp.broadcast_to(cos_t[:, None, :], (DEC_SEQ, SUB, D_KV)).reshape(N_SAMPLE, D_KV)
    sin_t = jnp.broadcast_to(sin_t[:, None, :], (DEC_SEQ, SUB, D_KV)).reshape(N_SAMPLE, D_KV)
    return cos_t, sin_t


def _prompt_to_batch_major(a):
    w = a.shape[-1]
    return a[:N_PROMPT].reshape(P_GROUPS, SEQ, SUB, w).transpose(0, 2, 1, 3).reshape(BATCH, SEQ, w)


def kernel(x_prompt, x_sample, c, state_rglru, cache_k, cache_v, c_ctx, w_mod, b_mod, norm1, norm2, ev_w_in, a_conv_w, a_conv_b, a_w_r, a_b_r, a_w_i, a_b_i, a_lam, b_w_pool, b_scale, ev_w_out, od_w_in, c_conv_w, c_conv_b, c_ln_g, c_ln_b, q_norm, k_norm, od_w_out, ff_w1, ff_w3, ff_w2, moe_w_router, moe_b_router, moe_w1, moe_w3, moe_w2, norm_f):
    cond16 = jnp.concatenate([c_ctx[None, :], c, jnp.zeros((16 - 1 - DEC_BATCH, D), F32)], axis=0)
    mods = _ada_params(cond16, w_mod, b_mod)
    mods = jnp.stack([jnp.broadcast_to(mods[:, :, 0:1], (DEPTH, 6, SUB, D)), mods[:, :, 1:1 + SUB]], axis=2)

    cos_t, sin_t = _rope_tables()
    bdq = _head_mean_matrix(D_ATT)
    bdk = _head_mean_matrix(D_KV)
    cache_k4 = cache_k.reshape(DEC_BATCH, DEPTH // 2, PAST_LEN, D_KV)
    cache_v4 = cache_v.reshape(DEC_BATCH, DEPTH // 2, PAST_LEN, D_KV)
    ff_w1b, ff_w3b, ff_w2b = ff_w1.astype(BF16), ff_w3.astype(BF16), ff_w2.astype(BF16)
    moe_w1b, moe_w3b, moe_w2b = moe_w1.astype(BF16), moe_w3.astype(BF16), moe_w2.astype(BF16)

    x = pending = None
    new_states, new_k, new_v = [], [], []
    for layer in range(DEPTH):
        li = layer // 2
        shift1, scale1, gate1, shift2, scale2, gate2 = [mods[layer, j] for j in range(6)]
        g1 = norm1[layer].reshape(1, D)
        g2 = norm2[layer].reshape(1, D)
        if layer % 2 == 0:
            if layer == 0:
                z, x = _first_in(x_prompt, x_sample, g1, shift1, scale1, ev_w_in[li].astype(BF16))
            else:
                z, x = _even_in_moe(pending, g1, shift1, scale1, ev_w_in[li].astype(BF16))
            wg, bg = _block_diag_gates(a_w_r[li], a_w_i[li], a_b_r[li], a_b_i[li])
            lam = jnp.concatenate([a_lam[li, 0].reshape(4, 1, LANES), a_lam[li, 1].reshape(4, 1, LANES)], axis=2)
            cb = a_conv_b[li].reshape(1, D_A)
            h0_p = jnp.zeros((P_GROUPS, 2, SUB, D_A), F32)
            h0_s = state_rglru[:, li].transpose(1, 0, 2)[None]
            ya_p, h_last = _rglru_call(z, a_conv_w[li], cb, wg, bg, lam, h0_p,
                                       rows=R_PROMPT, groups=P_GROUPS, row_block0=0)
            ya_s, _ = _rglru_call(z, a_conv_w[li], cb, wg, bg, lam, h0_s,
                                  rows=R_SAMPLE, groups=1, row_block0=1)
            wp = b_w_pool[li].astype(BF16)
            sp = b_scale[li].reshape(1, D_B)
            yb_p = _pool_call(z, wp, sp, rows=R_PROMPT, groups=P_GROUPS, row_block0=0)
            yb_s = _pool_call(z, wp, sp, rows=R_SAMPLE, groups=1, row_block0=1)
            w_out = ev_w_out[li].astype(BF16)
            x = _even_tail(x, (ya_p, ya_s), (yb_p, yb_s), w_out[:D_A], w_out[D_A:], gate1,
                           g2, shift2, scale2, gate2, ff_w1b, ff_w3b, ff_w2b, li)
            new_states.append(h_last.transpose(0, 2, 1, 3).reshape(BATCH, 2, D_A))
        else:
            qg = jnp.tile(q_norm[li], N_Q_HEADS).reshape(1, D_ATT)
            kg = jnp.tile(k_norm[li], N_KV_HEADS).reshape(1, D_KV)
            glu, *qs, k, v = _odd_in(x, g1, shift1, scale1, od_w_in[li].astype(BF16), bdq, bdk, qg, kg,
                                     cos_t, sin_t)
            att = _attention(qs, k, v, cache_k4[:, li], cache_v4[:, li])
            cw = c_conv_w[li]
            cb = c_conv_b[li].reshape(1, D_C)
            hc = (_conv_call(glu, cw, cb, rows=R_PROMPT, groups=P_GROUPS, row_block0=0),
                  _conv_call(glu, cw, cb, rows=R_SAMPLE, groups=1, row_block0=1))
            w_out = od_w_out[li].astype(BF16)
            wr = jnp.zeros((D, LANES), F32).at[:, :N_EXPERTS].set(moe_w_router[li])
            br = jnp.zeros((1, LANES), F32).at[0, :N_EXPERTS].set(moe_b_router[li])
            wr_hi = wr.astype(BF16)
            wr_lo = (wr - wr_hi.astype(F32)).astype(BF16)
            x1, xloc, route, counts = _odd_out(x, hc, att, c_ln_g[li].reshape(1, D_C), c_ln_b[li].reshape(1, D_C),
                                     w_out[:D_C], w_out[D_C:], gate1, g2, shift2, scale2, wr_hi, wr_lo, br)
            y2 = _moe_experts(_route_plan(counts), xloc, moe_w1b, moe_w3b, moe_w2b, li)
            pending = (x1, y2, route, gate2)
            new_k.append(_prompt_to_batch_major(k).reshape(BATCH, SEQ, N_KV_HEADS, HEAD_DIM))
            new_v.append(_prompt_to_batch_major(v).reshape(BATCH, SEQ, N_KV_HEADS, HEAD_DIM))

    gf = norm_f.reshape(1, D)
    y_prompt = _final_norm(pending, gf, batch=BATCH, steps=SEQ, stream=0)
    y_sample = _final_norm(pending, gf, batch=DEC_BATCH, steps=DEC_SEQ, stream=1)
    return (y_prompt, y_sample, jnp.stack(new_states, axis=1), jnp.stack(new_k, axis=1), jnp.stack(new_v, axis=1))
```

```python
import functools

import jax
import jax.numpy as jnp
from jax import lax
from jax.experimental import pallas as pl
from jax.experimental.pallas import tpu as pltpu

F32 = jnp.float32
BF16 = jnp.bfloat16

D = 1024
BATCH = 32
SEQ = 256
DEPTH = 4
DEC_BATCH = 8
DEC_SEQ = 1024
PAST_LEN = 256
GRID_W = 64
EPS = 1e-6
D_A = 512
A_BLOCKS = 8
A_BS = 64
A_CONV = 4
A_C = 8.0
D_B = 512
POOL_WINDOWS = (2, 4, 8, 16)
B_GS = 128
D_C = 512
C_CONV = 31
HEAD_DIM = 64
N_Q_HEADS = 8
N_KV_HEADS = 2
D_ATT = 512
D_KV = 128
ROPE_THETA = 10000.0
D_FF = 2816
N_EXPERTS = 8
D_FF_E = 1408
D_IN_EVEN = 1536
D_IN_ODD = 1792

SUB = 8
LANES = 128
N_PROMPT = BATCH * SEQ
N_SAMPLE = DEC_BATCH * DEC_SEQ
N_TOK = N_PROMPT + N_SAMPLE
P_GROUPS = BATCH // SUB
R_PROMPT = SEQ * SUB
R_SAMPLE = DEC_SEQ * SUB
TM = 512
TILES_PER_STREAM = N_PROMPT // TM
MOE_TM = 512
MOE_TILES = 2 * N_TOK // MOE_TM + N_EXPERTS
CH = 256
N_CHUNKS = N_TOK // CH
SLAB = D // LANES
NEG_BIG = -3.0e38
VMEM_LIMIT = 56 * 1024 * 1024


def _params(n_axes, vmem=VMEM_LIMIT):
    return pltpu.CompilerParams(dimension_semantics=("arbitrary",) * n_axes, vmem_limit_bytes=vmem)


def _rms(x, g):
    ms = jnp.mean(x * x, axis=-1, keepdims=True)
    return x * lax.rsqrt(ms + EPS) * g


def _modulate(xn, shift, scale):
    tm = xn.shape[0]
    h = xn.reshape(tm // SUB, SUB, D) * (1.0 + scale)[None] + shift[None]
    return h.reshape(tm, D)


def _gated_add(x, gate, y):
    tm = x.shape[0]
    return x + (y.reshape(tm // SUB, SUB, D) * gate[None]).reshape(tm, D)


def _mod_spec():
    return pl.BlockSpec((None, SUB, D), lambda i, *_: (i // TILES_PER_STREAM, 0, 0))


def _row_spec(width, tm=TM):
    return pl.BlockSpec((tm, width), lambda i, *_: (i, 0))


def _full_spec(shape):
    nd = len(shape)
    return pl.BlockSpec(shape, lambda i, *_: (0,) * nd)


def _ada_kernel(c_ref, w_ref, b_ref, o_ref):
    c = c_ref[...]
    s = (c * jax.nn.sigmoid(c)).astype(BF16)
    o_ref[...] = jnp.dot(s, w_ref[...].astype(BF16), preferred_element_type=F32) + b_ref[...]


def _ada_params(cond16, w_mod, b_mod):
    return pl.pallas_call(
        _ada_kernel,
        grid=(DEPTH, 6),
        in_specs=[
            pl.BlockSpec((16, D), lambda l, j: (0, 0)),
            pl.BlockSpec((None, D, D), lambda l, j: (l, 0, j)),
            pl.BlockSpec((None, None, 1, D), lambda l, j: (l, j, 0, 0)),
        ],
        out_specs=pl.BlockSpec((None, None, 16, D), lambda l, j: (l, j, 0, 0)),
        out_shape=jax.ShapeDtypeStruct((DEPTH, 6, 16, D), F32),
        compiler_params=_params(2),
        name="ada_params",
    )(cond16, w_mod, b_mod.reshape(DEPTH, 6, 1, D))


T_TILE = TM // SUB


def _first_in_kernel(xp_ref, xs_ref, g_ref, sh_ref, sc_ref, w_ref, z_ref, x_ref, cols):
    is_sample = pl.program_id(0) >= TILES_PER_STREAM
    for b in range(SUB):
        xb = jnp.where(is_sample, xs_ref[b], xp_ref[b])
        for c in range(SLAB):
            cols[c, pl.ds(b, T_TILE, stride=SUB), :] = xb[:, c * LANES:(c + 1) * LANES]
    x = jnp.concatenate([cols[c] for c in range(SLAB)], axis=1)
    x_ref[...] = x
    h = _modulate(_rms(x, g_ref[...]), sh_ref[...], sc_ref[...])
    z_ref[...] = jnp.dot(h.astype(BF16), w_ref[...], preferred_element_type=F32)


def _first_in(x_prompt, x_sample, g, shift, scale, w_bf16):
    t_tiles = SEQ // T_TILE
    last_p = TILES_PER_STREAM - 1
    return pl.pallas_call(
        _first_in_kernel,
        grid=(N_TOK // TM,),
        in_specs=[
            pl.BlockSpec((SUB, T_TILE, D), lambda i: (jnp.minimum(i, last_p) // t_tiles,
                                                      jnp.minimum(i, last_p) % t_tiles, 0)),
            pl.BlockSpec((SUB, T_TILE, D), lambda i: (0, jnp.maximum(i - TILES_PER_STREAM, 0), 0)),
            _full_spec((1, D)), _mod_spec(), _mod_spec(), _full_spec((D, D_IN_EVEN))],
        out_specs=[_row_spec(D_IN_EVEN), _row_spec(D)],
        out_shape=[jax.ShapeDtypeStruct((N_TOK, D_IN_EVEN), F32), jax.ShapeDtypeStruct((N_TOK, D), F32)],
        scratch_shapes=[pltpu.VMEM((SLAB, TM, LANES), F32)],
        compiler_params=_params(1),
        name="first_in",
    )(x_prompt, x_sample, g, shift, scale, w_bf16)


RG_CHUNK = 256
RG_PAD = 16


def _rglru_kernel(u_ref, ga_ref, cw_ref, cb_ref, wg_ref, bg_ref, lam_ref, h0_ref,
                  y_ref, hl_ref, xpad, a0, b0, a1, b1, *, rows):
    steps = rows // SUB
    n_chunks = rows // RG_CHUNK

    xpad[0:RG_PAD, :] = jnp.zeros((RG_PAD, LANES), F32)
    xpad[RG_PAD + rows:RG_PAD + rows + RG_PAD, :] = jnp.zeros((RG_PAD, LANES), F32)

    def copy_chunk(c, carry):
        r = pl.multiple_of(c * RG_CHUNK, RG_CHUNK)
        xpad[pl.ds(RG_PAD + r, RG_CHUNK), :] = u_ref[pl.ds(r, RG_CHUNK), :]
        return carry

    lax.fori_loop(0, n_chunks, copy_chunk, 0)

    lam = lam_ref[...]
    softplus_neg = jnp.maximum(-lam, 0.0) + jnp.log1p(jnp.exp(-jnp.abs(lam)))
    decay = A_C * softplus_neg
    decay_log2 = decay * (-1.4426950408889634)
    a_refs = (a0, a1)
    b_refs = (b0, b1)

    def gate_chunk(c, carry):
        r = pl.multiple_of(c * RG_CHUNK, RG_CHUNK)
        xc = jnp.zeros((RG_CHUNK, LANES), F32) + cb_ref[...]
        for k in range(A_CONV):
            xc = xc + cw_ref[k:k + 1, :] * xpad[pl.ds(r + SUB * k, RG_CHUNK), :]
        pre = jnp.dot(xc.astype(BF16), wg_ref[...], preferred_element_type=F32) + bg_ref[...]
        for d in range(2):
            rg = 0.5 * jnp.tanh(0.5 * pre[:, d * 256:d * 256 + LANES]) + 0.5
            ig = 0.5 * jnp.tanh(0.5 * pre[:, d * 256 + LANES:(d + 1) * 256]) + 0.5
            a = jnp.exp2(rg * decay_log2[:, d * LANES:(d + 1) * LANES])
            one_minus_a2 = jnp.tanh(rg * decay[:, d * LANES:(d + 1) * LANES]) * (a * a + 1.0)
            root = one_minus_a2 * lax.rsqrt(jnp.maximum(one_minus_a2, 1e-30))
            a_refs[d][pl.ds(r, RG_CHUNK), :] = a
            b_refs[d][pl.ds(r, RG_CHUNK), :] = root * (ig * xc)
        return carry

    lax.fori_loop(0, n_chunks, gate_chunk, 0)

    def step(t, carry):
        hf, hb = carry
        rf = pl.multiple_of(t * SUB, SUB)
        rb = pl.multiple_of((steps - 1 - t) * SUB, SUB)
        hf = a0[pl.ds(rf, SUB), :] * hf + b0[pl.ds(rf, SUB), :]
        b0[pl.ds(rf, SUB), :] = hf
        hb = a1[pl.ds(rb, SUB), :] * hb + b1[pl.ds(rb, SUB), :]
        b1[pl.ds(rb, SUB), :] = hb
        return hf, hb

    hf, hb = lax.fori_loop(0, steps, step, (h0_ref[0], h0_ref[1]), unroll=8)
    hl_ref[0] = hf
    hl_ref[1] = hb

    def out_chunk(c, carry):
        r = pl.multiple_of(c * RG_CHUNK, RG_CHUNK)
        y = (b0[pl.ds(r, RG_CHUNK), :] + b1[pl.ds(r, RG_CHUNK), :]) * jax.nn.gelu(ga_ref[pl.ds(r, RG_CHUNK), :])
        y_ref[pl.ds(r, RG_CHUNK), :] = y.astype(BF16)
        return carry

    lax.fori_loop(0, n_chunks, out_chunk, 0)


def _rglru_call(z, cw, cb, wg, bg, lam, h0, *, rows, groups, row_block0):
    n_cb = D_A // LANES
    in_specs = [
        pl.BlockSpec((rows, LANES), lambda g, j: (row_block0 + g, j)),
        pl.BlockSpec((rows, LANES), lambda g, j: (row_block0 + g, n_cb + j)),
        pl.BlockSpec((A_CONV, LANES), lambda g, j: (0, j)),
        pl.BlockSpec((1, LANES), lambda g, j: (0, j)),
        pl.BlockSpec((None, LANES, 4 * LANES), lambda g, j: (j, 0, 0)),
        pl.BlockSpec((None, 1, 4 * LANES), lambda g, j: (j, 0, 0)),
        pl.BlockSpec((None, 1, 2 * LANES), lambda g, j: (j, 0, 0)),
        pl.BlockSpec((None, 2, SUB, LANES), lambda g, j: (g, 0, 0, j)),
    ]
    return pl.pallas_call(
        functools.partial(_rglru_kernel, rows=rows),
        grid=(groups, n_cb),
        in_specs=in_specs,
        out_specs=[
            pl.BlockSpec((rows, LANES), lambda g, j: (g, j)),
            pl.BlockSpec((None, 2, SUB, LANES), lambda g, j: (g, 0, 0, j)),
        ],
        out_shape=[
            jax.ShapeDtypeStruct((groups * rows, D_A), BF16),
            jax.ShapeDtypeStruct((groups, 2, SUB, D_A), F32),
        ],
        scratch_shapes=[pltpu.VMEM((rows + 2 * RG_PAD, LANES), F32)] + [pltpu.VMEM((rows, LANES), F32)] * 4,
        compiler_params=_params(2),
        name=f"rglru_{rows}",
    )(z, z, cw, cb, wg, bg, lam, h0)


POOL_CHUNK = 256
POOL_PAD = 64


def _pool_kernel(u_ref, w_ref, s_ref, y_ref, xpad, *, rows):
    steps = rows // SUB
    n_chunks = rows // POOL_CHUNK
    j = pl.program_id(1)

    xpad[0:POOL_PAD, :] = jnp.zeros((POOL_PAD, LANES), F32)
    xpad[POOL_PAD + rows:POOL_PAD + rows + POOL_PAD, :] = jnp.zeros((POOL_PAD, LANES), F32)

    def copy_chunk(c, carry):
        r = pl.multiple_of(c * POOL_CHUNK, POOL_CHUNK)
        xpad[pl.ds(POOL_PAD + r, POOL_CHUNK), :] = u_ref[pl.ds(r, POOL_CHUNK), :]
        return carry

    lax.fori_loop(0, n_chunks, copy_chunk, 0)

    for gi, win in enumerate(POOL_WINDOWS):
        half = win // 2

        @pl.when(j == gi)
        def _(half=half, win=win):
            def chunk(c, carry):
                r = pl.multiple_of(c * POOL_CHUNK, POOL_CHUNK)
                acc = xpad[pl.ds(POOL_PAD + r - SUB * half, POOL_CHUNK), :]
                for s in range(1, win):
                    acc = acc + xpad[pl.ds(POOL_PAD + r + SUB * (s - half), POOL_CHUNK), :]
                row = r + lax.broadcasted_iota(jnp.int32, (POOL_CHUNK, LANES), 0)
                t = lax.shift_right_logical(row, SUB.bit_length() - 1)
                cnt = jnp.minimum(t + half, steps) - jnp.maximum(t - half, 0)
                dlt = acc / cnt.astype(F32) - xpad[pl.ds(POOL_PAD + r, POOL_CHUNK), :]
                y = jnp.dot(dlt.astype(BF16), w_ref[...], preferred_element_type=F32) * s_ref[...]
                y_ref[pl.ds(r, POOL_CHUNK), :] = y.astype(BF16)
                return carry

            lax.fori_loop(0, n_chunks, chunk, 0)


def _pool_call(z, w_pool_bf16, s_pool, *, rows, groups, row_block0):
    n_cb = D_B // LANES
    col0 = 2 * D_A // LANES
    in_specs = [
        pl.BlockSpec((rows, LANES), lambda g, j: (row_block0 + g, col0 + j)),
        pl.BlockSpec((None, B_GS, B_GS), lambda g, j: (j, 0, 0)),
        pl.BlockSpec((1, LANES), lambda g, j: (0, j)),
    ]
    return pl.pallas_call(
        functools.partial(_pool_kernel, rows=rows),
        grid=(groups, n_cb),
        in_specs=in_specs,
        out_specs=pl.BlockSpec((rows, LANES), lambda g, j: (g, j)),
        out_shape=jax.ShapeDtypeStruct((groups * rows, D_B), BF16),
        scratch_shapes=[pltpu.VMEM((rows + 2 * POOL_PAD, LANES), F32)],
        compiler_params=_params(2),
        name=f"pool_{rows}",
    )(z, w_pool_bf16, s_pool)


def _stream_specs(width):
    prompt = pl.BlockSpec((TM, width), lambda i, *_: (jnp.minimum(i, TILES_PER_STREAM - 1), 0))
    sample = pl.BlockSpec((TM, width), lambda i, *_: (jnp.maximum(i - TILES_PER_STREAM, 0), 0))
    return [prompt, sample]


def _pick_stream(prompt_ref, sample_ref):
    return jnp.where(pl.program_id(0) >= TILES_PER_STREAM, sample_ref[...], prompt_ref[...])


def _even_tail_kernel(x_ref, yap_ref, yas_ref, ybp_ref, ybs_ref, wa_ref, wb_ref, gate1_ref,
                      g2_ref, sh_ref, sc_ref, gate2_ref, w1_ref, w3_ref, w2_ref, *rest):
    n_cast = (len(rest) - 1) // 2
    cast_in, o_ref, cast_out = rest[:n_cast], rest[n_cast], rest[n_cast + 1:]
    for src, dst in zip(cast_in, cast_out):
        dst[...] = src[...].astype(BF16)
    y = jnp.dot(_pick_stream(yap_ref, yas_ref), wa_ref[...], preferred_element_type=F32)
    y = y + jnp.dot(_pick_stream(ybp_ref, ybs_ref), wb_ref[...], preferred_element_type=F32)
    x1 = _gated_add(x_ref[...], gate1_ref[...], y)
    h = _modulate(_rms(x1, g2_ref[...]), sh_ref[...], sc_ref[...]).astype(BF16)
    a = jnp.dot(h, w1_ref[...], preferred_element_type=F32)
    b = jnp.dot(h, w3_ref[...], preferred_element_type=F32)
    u = (a * jax.nn.sigmoid(a) * b).astype(BF16)
    f = jnp.dot(u, w2_ref[...], preferred_element_type=F32)
    o_ref[...] = _gated_add(x1, gate2_ref[...], f)


def _even_tail(x, ya, yb, wa, wb, gate1, g2, shift2, scale2, gate2, w1, w3, w2, li, to_cast):
    half = D // 2
    steps = N_TOK // TM
    resident = pl.Buffered(1)
    cast_in_specs, cast_out_specs, cast_shapes, cast_args = [], [], [], []
    for w in to_cast:
        _, n_e, rows, cols = w.shape
        blk = n_e * rows // steps
        cast_args.append(w.reshape(w.shape[0], n_e * rows, cols))
        cast_in_specs.append(pl.BlockSpec((None, blk, cols), lambda i: (li, i, 0)))
        cast_out_specs.append(pl.BlockSpec((blk, cols), lambda i: (i, 0)))
        cast_shapes.append(jax.ShapeDtypeStruct((n_e * rows, cols), BF16))
    outs = pl.pallas_call(
        _even_tail_kernel,
        grid=(steps,),
        in_specs=[_row_spec(D)] + _stream_specs(half) + _stream_specs(half) +
                 [pl.BlockSpec((half, D), lambda i: (0, 0), pipeline_mode=resident),
                  pl.BlockSpec((half, D), lambda i: (0, 0), pipeline_mode=resident),
                  _mod_spec(), _full_spec((1, D)), _mod_spec(), _mod_spec(), _mod_spec(),
                  pl.BlockSpec((None, D, D_FF), lambda i: (li, 0, 0), pipeline_mode=resident),
                  pl.BlockSpec((None, D, D_FF), lambda i: (li, 0, 0), pipeline_mode=resident),
                  pl.BlockSpec((None, D_FF, D), lambda i: (li, 0, 0), pipeline_mode=resident)] + cast_in_specs,
        out_specs=[_row_spec(D)] + cast_out_specs,
        out_shape=[jax.ShapeDtypeStruct((N_TOK, D), F32)] + cast_shapes,
        compiler_params=_params(1),
        name="even_tail",
    )(x, *ya, *yb, wa, wb, gate1, g2, shift2, scale2, gate2, w1, w3, w2, *cast_args)
    casted = [o.reshape(w.shape[1:]) for o, w in zip(outs[1:], to_cast)]
    return outs[0], casted


def _head_rms(x, ones_bd, g):
    sq = x * x
    hi = sq.astype(BF16)
    lo = (sq - hi.astype(F32)).astype(BF16)
    ms = jnp.dot(hi, ones_bd, preferred_element_type=F32) + jnp.dot(lo, ones_bd, preferred_element_type=F32)
    return x * lax.rsqrt(ms + EPS) * g


def _rope(x, cos, sin_signed):
    w = x.shape[-1]
    lane = lax.broadcasted_iota(jnp.int32, x.shape, 1)
    first = (lane % 32) < 16
    partner = jnp.where(first, pltpu.roll(x, w - 16, 1), pltpu.roll(x, 16, 1))
    return x * cos + partner * sin_signed


Q_BLOCKS = D_ATT // LANES


def _odd_in_kernel(x_ref, g_ref, sh_ref, sc_ref, w_ref, bdq_ref, bdk_ref, qg_ref, kg_ref, cos_ref, sin_ref,
                   glu_ref, q0_ref, q1_ref, q2_ref, q3_ref, k_ref, v_ref):
    i = pl.program_id(0)
    h = _modulate(_rms(x_ref[...], g_ref[...]), sh_ref[...], sc_ref[...])
    z = jnp.dot(h.astype(BF16), w_ref[...], preferred_element_type=F32)
    glu_ref[...] = z[:, :D_C] * jax.nn.sigmoid(z[:, D_C:2 * D_C])
    o1 = 2 * D_C
    o2 = o1 + D_ATT
    o3 = o2 + D_KV
    q = _head_rms(z[:, o1:o2], bdq_ref[...], qg_ref[...])
    k = _head_rms(z[:, o2:o3], bdk_ref[...], kg_ref[...])
    cos = cos_ref[...]
    sin = sin_ref[...]
    is_sample = i >= TILES_PER_STREAM
    q_r = _rope(q, jnp.concatenate([cos] * 4, axis=1), jnp.concatenate([sin] * 4, axis=1))
    k_r = _rope(k, cos, sin)
    q = jnp.where(is_sample, q_r, q)
    for c, q_ref in enumerate((q0_ref, q1_ref, q2_ref, q3_ref)):
        q_ref[...] = q[:, c * LANES:(c + 1) * LANES]
    k_ref[...] = jnp.where(is_sample, k_r, k)
    v_ref[...] = z[:, o3:]


def _odd_in(x, g, shift, scale, w_bf16, bdq, bdk, qg, kg, cos_t, sin_t):
    rope_spec = pl.BlockSpec((TM, D_KV), lambda i: (jnp.maximum(i - TILES_PER_STREAM, 0), 0))
    return pl.pallas_call(
        _odd_in_kernel,
        grid=(N_TOK // TM,),
        in_specs=[_row_spec(D), _full_spec((1, D)), _mod_spec(), _mod_spec(), _full_spec((D, D_IN_ODD)),
                  _full_spec((D_ATT, D_ATT)), _full_spec((D_KV, D_KV)), _full_spec((1, D_ATT)),
                  _full_spec((1, D_KV)), rope_spec, rope_spec],
        out_specs=[_row_spec(D_C)] + [_row_spec(LANES)] * (Q_BLOCKS + 2),
        out_shape=[jax.ShapeDtypeStruct((N_TOK, D_C), F32)] +
                  [jax.ShapeDtypeStruct((N_TOK, LANES), F32)] * (Q_BLOCKS + 2),
        compiler_params=_params(1),
        name="odd_in",
    )(x, g, shift, scale, w_bf16, bdq, bdk, qg, kg, cos_t, sin_t)


def _attend(q_refs, o_refs, q_rows, k_all, v_all):
    scale = HEAD_DIM ** -0.5 * 1.4426950408889634
    heads_per_block = LANES // HEAD_DIM
    group = N_Q_HEADS // N_KV_HEADS
    kv = []
    for h in range(N_KV_HEADS):
        kv.append((k_all[:, h * HEAD_DIM:(h + 1) * HEAD_DIM].astype(BF16),
                   v_all[:, h * HEAD_DIM:(h + 1) * HEAD_DIM].astype(BF16)))
    for c in range(Q_BLOCKS):
        qc = (q_refs[c][q_rows, :] * scale).astype(BF16)
        outs = []
        for sub in range(heads_per_block):
            kh, vh = kv[(c * heads_per_block + sub) // group]
            qh = qc[:, sub * HEAD_DIM:(sub + 1) * HEAD_DIM]
            s = lax.dot_general(qh, kh, (((1,), (1,)), ((), ())), preferred_element_type=F32)
            m = jnp.max(s, axis=-1, keepdims=True)
            p = jnp.exp2(s - m)
            l = jnp.sum(p, axis=-1, keepdims=True)
            outs.append(jnp.dot(p.astype(BF16), vh, preferred_element_type=F32) / l)
        o_refs[c][q_rows, :] = jnp.concatenate(outs, axis=1)


def _attn_prompt_kernel(*refs):
    q_refs, (k_ref, v_ref), o_refs = refs[:Q_BLOCKS], refs[Q_BLOCKS:Q_BLOCKS + 2], refs[Q_BLOCKS + 2:]
    rows = pl.ds(pl.program_id(1), SEQ, stride=SUB)
    _attend(q_refs, o_refs, rows, k_ref[rows, :], v_ref[rows, :])


def _attn_sample_kernel(*refs):
    q_refs, (k_ref, v_ref, ck_ref, cv_ref), o_refs = refs[:Q_BLOCKS], refs[Q_BLOCKS:Q_BLOCKS + 4], refs[Q_BLOCKS + 4:]
    b = pl.program_id(1)
    kv_rows = pl.ds(b, DEC_SEQ, stride=SUB)
    k_all = jnp.concatenate([ck_ref[...], k_ref[kv_rows, :]], axis=0)
    v_all = jnp.concatenate([cv_ref[...], v_ref[kv_rows, :]], axis=0)
    _attend(q_refs, o_refs, pl.ds(b, SEQ, stride=SUB), k_all, v_all)


def _attention(qs, k, v, cache_k_l, cache_v_l):
    chunk = pl.BlockSpec((R_PROMPT, LANES), lambda g, b: (g, 0))
    out_shape = [jax.ShapeDtypeStruct((N_PROMPT, LANES), F32)] * Q_BLOCKS
    att_p = pl.pallas_call(
        _attn_prompt_kernel,
        grid=(P_GROUPS, SUB),
        in_specs=[chunk] * (Q_BLOCKS + 2),
        out_specs=[chunk] * Q_BLOCKS,
        out_shape=out_shape,
        compiler_params=_params(2),
        name="attn_prompt",
    )(*qs, k, v)
    q_chunks = DEC_SEQ // SEQ
    q_chunk = pl.BlockSpec((R_PROMPT, LANES), lambda c, b: (P_GROUPS + c, 0))
    kv_all = pl.BlockSpec((R_SAMPLE, LANES), lambda c, b: (1, 0))
    cache = pl.BlockSpec((None, PAST_LEN, D_KV), lambda c, b: (b, 0, 0))
    att_s = pl.pallas_call(
        _attn_sample_kernel,
        grid=(q_chunks, DEC_BATCH),
        in_specs=[q_chunk] * Q_BLOCKS + [kv_all, kv_all, cache, cache],
        out_specs=[pl.BlockSpec((R_PROMPT, LANES), lambda c, b: (c, 0))] * Q_BLOCKS,
        out_shape=out_shape,
        compiler_params=_params(2),
        name="attn_sample",
    )(*qs, k, v, cache_k_l, cache_v_l)
    return att_p, att_s


CV_CHUNK = 64
CV_PAD = (C_CONV // 2) * SUB


def _conv_kernel(u_ref, w_ref, b_ref, y_ref, xpad, *, rows):
    n_copy = rows // 256
    xpad[0:CV_PAD, :] = jnp.zeros((CV_PAD, LANES), F32)
    xpad[CV_PAD + rows:CV_PAD + rows + CV_PAD, :] = jnp.zeros((CV_PAD, LANES), F32)

    def copy_chunk(c, carry):
        r = pl.multiple_of(c * 256, 256)
        xpad[pl.ds(CV_PAD + r, 256), :] = u_ref[pl.ds(r, 256), :]
        return carry

    lax.fori_loop(0, n_copy, copy_chunk, 0)

    def chunk(c, carry):
        r = pl.multiple_of(c * CV_CHUNK, CV_CHUNK)
        acc = jnp.zeros((CV_CHUNK, LANES), F32) + b_ref[...]
        for k in range(C_CONV):
            acc = acc + w_ref[k:k + 1, :] * xpad[pl.ds(r + SUB * k, CV_CHUNK), :]
        y_ref[pl.ds(r, CV_CHUNK), :] = acc
        return carry

    lax.fori_loop(0, rows // CV_CHUNK, chunk, 0)


def _conv_call(glu, w, b, *, rows, groups, row_block0):
    n_cb = D_C // LANES
    in_specs = [
        pl.BlockSpec((rows, LANES), lambda g, j: (row_block0 + g, j)),
        pl.BlockSpec((C_CONV, LANES), lambda g, j: (0, j)),
        pl.BlockSpec((1, LANES), lambda g, j: (0, j)),
    ]
    return pl.pallas_call(
        functools.partial(_conv_kernel, rows=rows),
        grid=(groups, n_cb),
        in_specs=in_specs,
        out_specs=pl.BlockSpec((rows, LANES), lambda g, j: (g, j)),
        out_shape=jax.ShapeDtypeStruct((groups * rows, D_C), F32),
        scratch_shapes=[pltpu.VMEM((rows + 2 * CV_PAD, LANES), F32)],
        compiler_params=_params(2),
        name=f"conv_{rows}",
    )(glu, w, b)


def _odd_out_kernel(x_ref, hcp_ref, hcs_ref, *refs):
    att_refs, refs = refs[:2 * Q_BLOCKS], refs[2 * Q_BLOCKS:]
    (lng_ref, lnb_ref, wc_ref, wa_ref, gate_ref, g2_ref, sh_ref, sc_ref, wrh_ref, wrl_ref, br_ref, tril_ref,
     x1_ref, xloc_ref, route_ref, cnt_ref) = refs
    att = jnp.concatenate([_pick_stream(att_refs[c], att_refs[Q_BLOCKS + c]) for c in range(Q_BLOCKS)], axis=1)
    hc = _pick_stream(hcp_ref, hcs_ref)
    mu = jnp.mean(hc, axis=-1, keepdims=True)
    xc = hc - mu
    var = jnp.mean(xc * xc, axis=-1, keepdims=True)
    ln = xc * lax.rsqrt(var + EPS) * lng_ref[...] + lnb_ref[...]
    yc = (ln * jax.nn.sigmoid(ln)).astype(BF16)
    y = jnp.dot(yc, wc_ref[...], preferred_element_type=F32)
    y = y + jnp.dot(att.astype(BF16), wa_ref[...], preferred_element_type=F32)
    x1 = _gated_add(x_ref[...], gate_ref[...], y)
    x1_ref[...] = x1
    h2 = _modulate(_rms(x1, g2_ref[...]), sh_ref[...], sc_ref[...])
    h2_hi = h2.astype(BF16)
    h2_lo = (h2 - h2_hi.astype(F32)).astype(BF16)
    logits = (jnp.dot(h2_hi, wrh_ref[...], preferred_element_type=F32)
              + jnp.dot(h2_lo, wrh_ref[...], preferred_element_type=F32)
              + jnp.dot(h2_hi, wrl_ref[...], preferred_element_type=F32)) + br_ref[...]
    lane = lax.broadcasted_iota(jnp.int32, logits.shape, 1).astype(F32)
    lg = jnp.where(lane < N_EXPERTS, logits, NEG_BIG)
    m1 = jnp.max(lg, axis=-1, keepdims=True)
    i1 = jnp.min(jnp.where(lg == m1, lane, float(LANES)), axis=-1, keepdims=True)
    lg2 = jnp.where(lane == i1, NEG_BIG, lg)
    m2 = jnp.max(lg2, axis=-1, keepdims=True)
    i2 = jnp.min(jnp.where(lg2 == m2, lane, float(LANES)), axis=-1, keepdims=True)
    e = jnp.exp(m2 - m1)
    den = 1.0 + e

    q = lax.broadcasted_iota(jnp.int32, (CH, 2 * CH), 1).astype(F32)
    route_ref[...] = jnp.where(lane == 0.0, i1, jnp.where(lane == 1.0, i2,
                               jnp.where(lane == 2.0, 1.0 / den, jnp.where(lane == 3.0, e / den, 0.0))))
    lane_c = lax.broadcasted_iota(jnp.int32, (CH, LANES), 1).astype(F32)
    for c in range(TM // CH):
        rows = slice(c * CH, (c + 1) * CH)
        i1c = route_ref[rows, 0:1]
        i2c = route_ref[rows, 1:2]
        e1 = jnp.where(lane_c == i1c, 1.0, 0.0)
        e2 = jnp.where(lane_c == i2c, 1.0, 0.0)
        before1 = jnp.dot(tril_ref[...], e1.astype(BF16), preferred_element_type=F32)
        before2 = jnp.dot(tril_ref[...], e2.astype(BF16), preferred_element_type=F32)
        cnt1 = jnp.sum(e1, axis=0, keepdims=True)
        cnt2 = jnp.sum(e2, axis=0, keepdims=True)
        start = (jnp.sum(jnp.where(i1c < lane_c, 1.0, 0.0), axis=0, keepdims=True)
                 + jnp.sum(jnp.where(i2c < lane_c, 1.0, 0.0), axis=0, keepdims=True))
        pos0 = jnp.sum(e1 * (start + before1), axis=-1, keepdims=True)
        pos1 = jnp.sum(e2 * (start + cnt1 + before2), axis=-1, keepdims=True)
        route_ref[rows, :] = jnp.where(lane_c == 4.0, pos0, jnp.where(lane_c == 5.0, pos1, route_ref[rows, :]))
        cnt_ref[c * SUB:(c + 1) * SUB, :] = jnp.broadcast_to(cnt1 + cnt2, (SUB, LANES))
        sel = jnp.where(jnp.logical_or(q == pos0, q == pos1), 1.0, 0.0).astype(BF16)
        xloc = lax.dot_general(sel, h2_hi[rows], (((0,), (0,)), ((), ())), preferred_element_type=F32)
        for j in range(SLAB):
            xloc_ref[pl.ds(c * 2 * CH * SLAB + j, 2 * CH, stride=SLAB), :] = xloc[:, j * LANES:(j + 1) * LANES]


def _odd_out(x, hc, att, lng, lnb, wc, wa, gate, g2, shift2, scale2, wr_hi, wr_lo, br_pad):
    half = D // 2
    att_specs = [_stream_specs(LANES)[0]] * Q_BLOCKS + [_stream_specs(LANES)[1]] * Q_BLOCKS
    return pl.pallas_call(
        _odd_out_kernel,
        grid=(N_TOK // TM,),
        in_specs=[_row_spec(D)] + _stream_specs(half) + att_specs +
                 [_full_spec((1, half)), _full_spec((1, half)),
                  _full_spec((half, D)), _full_spec((half, D)), _mod_spec(),
                  _full_spec((1, D)), _mod_spec(), _mod_spec(),
                  _full_spec((D, LANES)), _full_spec((D, LANES)), _full_spec((1, LANES)), _full_spec((CH, CH))],
        out_specs=[_row_spec(D), pl.BlockSpec((2 * TM * SLAB, LANES), lambda i: (i, 0)), _row_spec(LANES),
                   pl.BlockSpec((TM // CH * SUB, LANES), lambda i: (i, 0))],
        out_shape=[jax.ShapeDtypeStruct((N_TOK, D), F32), jax.ShapeDtypeStruct((2 * N_TOK * SLAB, LANES), F32),
                   jax.ShapeDtypeStruct((N_TOK, LANES), F32),
                   jax.ShapeDtypeStruct((N_CHUNKS * SUB, LANES), F32)],
        compiler_params=_params(1),
        name="odd_out",
    )(x, *hc, *att[0], *att[1], lng, lnb, wc, wa, gate, g2, shift2, scale2, wr_hi, wr_lo, br_pad,
      jnp.tril(jnp.ones((CH, CH), BF16), -1))


def _moe_kernel(te_ref, nv_ref, nu_ref, c0_ref, c1_ref, src_ref, dst_ref, len_ref,
                xloc_hbm, w1_ref, w3_ref, w2_ref, yloc_hbm, xbuf, obuf, gsem, ssem):
    del te_ref
    i = pl.program_id(0)
    n_used = nu_ref[0]
    slot = i % 2
    other = 1 - slot

    def rows(start, n):
        return pl.ds(pl.multiple_of(start * SLAB, SLAB), n * SLAB)

    def for_pieces(tile, fn):
        first = c0_ref[tile]

        def body(c, carry):
            k = tile * N_CHUNKS + first + c
            n = len_ref[k]

            @pl.when(n > 0)
            def _():
                fn(src_ref[k], dst_ref[k], n)

            return carry

        lax.fori_loop(0, c1_ref[tile] - first, body, 0)

    def start_gather(tile, s):
        for_pieces(tile, lambda src, dst, n: pltpu.make_async_copy(
            xloc_hbm.at[rows(src, n), :], xbuf.at[s, rows(dst, n), :], gsem.at[s]).start())

    def start_scatter(tile, s):
        for_pieces(tile, lambda src, dst, n: pltpu.make_async_copy(
            obuf.at[s, rows(dst, n), :], yloc_hbm.at[rows(src, n), :], ssem.at[s]).start())

    def wait_gather(s, n):
        pltpu.make_async_copy(xloc_hbm.at[rows(0, n), :], xbuf.at[s, rows(0, n), :], gsem.at[s]).wait()

    def wait_scatter(s, n):
        pltpu.make_async_copy(obuf.at[s, rows(0, n), :], yloc_hbm.at[rows(0, n), :], ssem.at[s]).wait()

    @pl.when(i == 0)
    def _():
        xbuf[...] = jnp.zeros(xbuf.shape, F32)
        start_gather(0, 0)

    @pl.when(i < n_used)
    def _():
        wait_gather(slot, nv_ref[i])

        @pl.when(i + 1 < n_used)
        def _():
            start_gather(i + 1, other)

        @pl.when(i >= 2)
        def _():
            wait_scatter(slot, nv_ref[jnp.maximum(i - 2, 0)])

        x = jnp.concatenate([xbuf[slot, pl.ds(j, MOE_TM, stride=SLAB), :] for j in range(SLAB)], axis=1)
        h = x.astype(BF16)
        a = jnp.dot(h, w1_ref[...], preferred_element_type=F32)
        b = jnp.dot(h, w3_ref[...], preferred_element_type=F32)
        u = (a * jax.nn.sigmoid(a) * b).astype(BF16)
        y = jnp.dot(u, w2_ref[...], preferred_element_type=F32)
        for j in range(SLAB):
            obuf[slot, pl.ds(j, MOE_TM, stride=SLAB), :] = y[:, j * LANES:(j + 1) * LANES]
        start_scatter(i, slot)

        @pl.when(i == n_used - 1)
        def _():
            @pl.when(i >= 1)
            def _():
                wait_scatter(other, nv_ref[jnp.maximum(i - 1, 0)])

            wait_scatter(slot, nv_ref[i])


def _moe_experts(plan, xloc, w1, w3, w2):
    def weight(rows, cols):
        return pl.BlockSpec((None, rows, cols), lambda i, te, *_: (te[i], 0, 0))

    grid_spec = pltpu.PrefetchScalarGridSpec(
        num_scalar_prefetch=len(plan),
        grid=(MOE_TILES,),
        in_specs=[pl.BlockSpec(memory_space=pl.ANY),
                  weight(D, D_FF_E), weight(D, D_FF_E), weight(D_FF_E, D)],
        out_specs=pl.BlockSpec(memory_space=pl.ANY),
        scratch_shapes=[
            pltpu.VMEM((2, MOE_TM * SLAB, LANES), F32),
            pltpu.VMEM((2, MOE_TM * SLAB, LANES), F32),
            pltpu.SemaphoreType.DMA((2,)),
            pltpu.SemaphoreType.DMA((2,)),
        ],
    )
    return pl.pallas_call(
        _moe_kernel,
        grid_spec=grid_spec,
        out_shape=jax.ShapeDtypeStruct((2 * N_TOK * SLAB, LANES), F32),
        compiler_params=_params(1),
        name="moe_experts",
    )(*plan, xloc, w1, w3, w2)


def _route_plan(counts):
    cnt = counts.reshape(N_CHUNKS, SUB, LANES)[:, 0, :N_EXPERTS].astype(jnp.int32)
    run_local = jnp.cumsum(cnt, axis=1) - cnt
    run_group = jnp.cumsum(cnt, axis=0) - cnt
    total = jnp.sum(cnt, axis=0)
    tiles = (total + MOE_TM - 1) // MOE_TM
    tile_end = jnp.cumsum(tiles)
    tile_ids = jnp.arange(MOE_TILES, dtype=jnp.int32)
    tile_expert = jnp.minimum(jnp.sum((tile_ids[:, None] >= tile_end[None, :]).astype(jnp.int32), axis=1),
                              N_EXPERTS - 1)
    first = (tile_ids - (tile_end - tiles)[tile_expert]) * MOE_TM
    used = tile_ids < tile_end[-1]
    tile_valid = jnp.where(used, jnp.clip(total[tile_expert] - first, 0, MOE_TM), 0).astype(jnp.int32)
    a = run_group[:, tile_expert].T
    n = cnt[:, tile_expert].T
    lo = jnp.maximum(a, first[:, None])
    hi = jnp.minimum(a + n, first[:, None] + MOE_TM)
    length = jnp.where(used[:, None], jnp.maximum(hi - lo, 0), 0)
    chunk_row0 = jnp.arange(N_CHUNKS, dtype=jnp.int32)[None, :] * (2 * CH)
    src = jnp.where(length > 0, chunk_row0 + run_local[:, tile_expert].T + (lo - a), 0)
    dst = jnp.where(length > 0, lo - first[:, None], 0)
    n_used = tile_end[-1:].astype(jnp.int32)
    seen = jnp.cumsum((length > 0).astype(jnp.int32), axis=1)
    c_first = jnp.sum((seen == 0).astype(jnp.int32), axis=1)
    c_stop = N_CHUNKS - jnp.sum(jnp.logical_and(seen == seen[:, -1:], length == 0).astype(jnp.int32), axis=1)
    flat = lambda t: t.astype(jnp.int32).reshape(-1)
    return tile_expert, tile_valid, n_used, c_first, c_stop, flat(src), flat(dst), flat(length)


def _moe_combined(x_ref, y_ref, route_ref, gate_ref):
    q = lax.broadcasted_iota(jnp.int32, (CH, 2 * CH), 1).astype(F32)
    mix = []
    for c in range(TM // CH):
        y = jnp.concatenate([y_ref[pl.ds(c * 2 * CH * SLAB + j, 2 * CH, stride=SLAB), :] for j in range(SLAB)],
                            axis=1).astype(BF16)
        rows = slice(c * CH, (c + 1) * CH)
        pick = jnp.where(q == route_ref[rows, 4:5], route_ref[rows, 2:3],
                         jnp.where(q == route_ref[rows, 5:6], route_ref[rows, 3:4], 0.0)).astype(BF16)
        mix.append(jnp.dot(pick, y, preferred_element_type=F32))
    return _gated_add(x_ref[...], gate_ref[...], jnp.concatenate(mix, axis=0))


def _moe_pending_specs(tile0=0, stream=None):
    gate = _mod_spec() if stream is None else pl.BlockSpec((None, SUB, D), lambda i: (stream, 0, 0))
    return [pl.BlockSpec((TM, D), lambda i: (tile0 + i, 0)),
            pl.BlockSpec((TM * 2 * SLAB, LANES), lambda i: (tile0 + i, 0)),
            pl.BlockSpec((TM, LANES), lambda i: (tile0 + i, 0)), gate]


def _even_in_moe_kernel(x1_ref, y_ref, route_ref, gate_ref, g_ref, sh_ref, sc_ref, w_ref, z_ref, x_ref):
    x = _moe_combined(x1_ref, y_ref, route_ref, gate_ref)
    x_ref[...] = x
    h = _modulate(_rms(x, g_ref[...]), sh_ref[...], sc_ref[...])
    z_ref[...] = jnp.dot(h.astype(BF16), w_ref[...], preferred_element_type=F32)


def _even_in_moe(pending, g, shift, scale, w_bf16):
    return pl.pallas_call(
        _even_in_moe_kernel,
        grid=(N_TOK // TM,),
        in_specs=_moe_pending_specs() + [_full_spec((1, D)), _mod_spec(), _mod_spec(), _full_spec((D, D_IN_EVEN))],
        out_specs=[_row_spec(D_IN_EVEN), _row_spec(D)],
        out_shape=[jax.ShapeDtypeStruct((N_TOK, D_IN_EVEN), F32), jax.ShapeDtypeStruct((N_TOK, D), F32)],
        compiler_params=_params(1),
        name="even_in_moe",
    )(*pending, g, shift, scale, w_bf16)


def _final_kernel(x1_ref, y_ref, route_ref, gate_ref, g_ref, o_ref, cols):
    y = _rms(_moe_combined(x1_ref, y_ref, route_ref, gate_ref), g_ref[...])
    for c in range(SLAB):
        cols[c] = y[:, c * LANES:(c + 1) * LANES]
    for b in range(SUB):
        for c in range(SLAB):
            o_ref[b, :, c * LANES:(c + 1) * LANES] = cols[c, pl.ds(b, T_TILE, stride=SUB), :]


def _final_norm(pending, g, *, batch, steps, stream):
    t_tiles = steps // T_TILE
    return pl.pallas_call(
        _final_kernel,
        grid=(batch * steps // TM,),
        in_specs=_moe_pending_specs(stream * TILES_PER_STREAM, stream) + [_full_spec((1, D))],
        out_specs=pl.BlockSpec((SUB, T_TILE, D), lambda i: (i // t_tiles, i % t_tiles, 0)),
        out_shape=jax.ShapeDtypeStruct((batch, steps, D), F32),
        scratch_shapes=[pltpu.VMEM((SLAB, TM, LANES), F32)],
        compiler_params=_params(1),
        name=f"final_norm_{steps}",
    )(*pending, g)


def _block_diag_gates(w_r, w_i, b_r, b_i):
    def bd(w):
        w4 = w.reshape(4, 2, A_BS, A_BS)
        z = jnp.zeros((4, A_BS, A_BS), w.dtype)
        top = jnp.concatenate([w4[:, 0], z], axis=2)
        bot = jnp.concatenate([z, w4[:, 1]], axis=2)
        return jnp.concatenate([top, bot], axis=1)

    wg = jnp.concatenate([bd(w_r[0]), bd(w_i[0]), bd(w_r[1]), bd(w_i[1])], axis=2)
    bg = jnp.concatenate([b_r[0].reshape(4, 1, LANES), b_i[0].reshape(4, 1, LANES),
                          b_r[1].reshape(4, 1, LANES), b_i[1].reshape(4, 1, LANES)], axis=2)
    return wg.astype(BF16), bg


def _head_mean_matrix(width):
    idx = jnp.arange(width) // HEAD_DIM
    return ((idx[:, None] == idx[None, :]).astype(F32) / HEAD_DIM).astype(BF16)


def _rope_tables():
    pos = jnp.arange(DEC_SEQ)
    row = (pos // GRID_W).astype(F32)
    col = (pos % GRID_W).astype(F32)
    n_freq = HEAD_DIM // 4
    inv = ROPE_THETA ** (-jnp.arange(n_freq, dtype=F32) / n_freq)
    ang = jnp.stack([row[:, None] * inv, col[:, None] * inv], axis=1)
    cos = jnp.cos(ang)
    sin = jnp.sin(ang)
    cos_h = jnp.stack([cos, cos], axis=2).reshape(DEC_SEQ, HEAD_DIM)
    sin_h = jnp.stack([-sin, sin], axis=2).reshape(DEC_SEQ, HEAD_DIM)
    cos_t = jnp.tile(cos_h, (1, 2))
    sin_t = jnp.tile(sin_h, (1, 2))
    cos_t = jnp.broadcast_to(cos_t[:, None, :], (DEC_SEQ, SUB, D_KV)).reshape(N_SAMPLE, D_KV)
    sin_t = jnp.broadcast_to(sin_t[:, None, :], (DEC_SEQ, SUB, D_KV)).reshape(N_SAMPLE, D_KV)
    return cos_t, sin_t


def _prompt_to_batch_major(a):
    w = a.shape[-1]
    return a[:N_PROMPT].reshape(P_GROUPS, SEQ, SUB, w).transpose(0, 2, 1, 3).reshape(BATCH, SEQ, w)


def kernel(x_prompt, x_sample, c, state_rglru, cache_k, cache_v, c_ctx, w_mod, b_mod, norm1, norm2, ev_w_in, a_conv_w, a_conv_b, a_w_r, a_b_r, a_w_i, a_b_i, a_lam, b_w_pool, b_scale, ev_w_out, od_w_in, c_conv_w, c_conv_b, c_ln_g, c_ln_b, q_norm, k_norm, od_w_out, ff_w1, ff_w3, ff_w2, moe_w_router, moe_b_router, moe_w1, moe_w3, moe_w2, norm_f):
    cond16 = jnp.concatenate([c_ctx[None, :], c, jnp.zeros((16 - 1 - DEC_BATCH, D), F32)], axis=0)
    mods = _ada_params(cond16, w_mod, b_mod)
    mods = jnp.stack([jnp.broadcast_to(mods[:, :, 0:1], (DEPTH, 6, SUB, D)), mods[:, :, 1:1 + SUB]], axis=2)

    cos_t, sin_t = _rope_tables()
    bdq = _head_mean_matrix(D_ATT)
    bdk = _head_mean_matrix(D_KV)
    cache_k4 = cache_k.reshape(DEC_BATCH, DEPTH // 2, PAST_LEN, D_KV)
    cache_v4 = cache_v.reshape(DEC_BATCH, DEPTH // 2, PAST_LEN, D_KV)
    ff_w1b, ff_w3b, ff_w2b = ff_w1.astype(BF16), ff_w3.astype(BF16), ff_w2.astype(BF16)

    x = pending = moe_wb = None
    new_states, new_k, new_v = [], [], []
    for layer in range(DEPTH):
        li = layer // 2
        shift1, scale1, gate1, shift2, scale2, gate2 = [mods[layer, j] for j in range(6)]
        g1 = norm1[layer].reshape(1, D)
        g2 = norm2[layer].reshape(1, D)
        if layer % 2 == 0:
            if layer == 0:
                z, x = _first_in(x_prompt, x_sample, g1, shift1, scale1, ev_w_in[li].astype(BF16))
            else:
                z, x = _even_in_moe(pending, g1, shift1, scale1, ev_w_in[li].astype(BF16))
            wg, bg = _block_diag_gates(a_w_r[li], a_w_i[li], a_b_r[li], a_b_i[li])
            lam = jnp.concatenate([a_lam[li, 0].reshape(4, 1, LANES), a_lam[li, 1].reshape(4, 1, LANES)], axis=2)
            cb = a_conv_b[li].reshape(1, D_A)
            h0_p = jnp.zeros((P_GROUPS, 2, SUB, D_A), F32)
            h0_s = state_rglru[:, li].transpose(1, 0, 2)[None]
            ya_p, h_last = _rglru_call(z, a_conv_w[li], cb, wg, bg, lam, h0_p,
                                       rows=R_PROMPT, groups=P_GROUPS, row_block0=0)
            ya_s, _ = _rglru_call(z, a_conv_w[li], cb, wg, bg, lam, h0_s,
                                  rows=R_SAMPLE, groups=1, row_block0=1)
            wp = b_w_pool[li].astype(BF16)
            sp = b_scale[li].reshape(1, D_B)
            yb_p = _pool_call(z, wp, sp, rows=R_PROMPT, groups=P_GROUPS, row_block0=0)
            yb_s = _pool_call(z, wp, sp, rows=R_SAMPLE, groups=1, row_block0=1)
            w_out = ev_w_out[li].astype(BF16)
            x, moe_wb = _even_tail(x, (ya_p, ya_s), (yb_p, yb_s), w_out[:D_A], w_out[D_A:], gate1,
                                   g2, shift2, scale2, gate2, ff_w1b, ff_w3b, ff_w2b, li, (moe_w1, moe_w3, moe_w2))
            new_states.append(h_last.transpose(0, 2, 1, 3).reshape(BATCH, 2, D_A))
        else:
            qg = jnp.tile(q_norm[li], N_Q_HEADS).reshape(1, D_ATT)
            kg = jnp.tile(k_norm[li], N_KV_HEADS).reshape(1, D_KV)
            glu, *qs, k, v = _odd_in(x, g1, shift1, scale1, od_w_in[li].astype(BF16), bdq, bdk, qg, kg,
                                     cos_t, sin_t)
            att = _attention(qs, k, v, cache_k4[:, li], cache_v4[:, li])
            cw = c_conv_w[li]
            cb = c_conv_b[li].reshape(1, D_C)
            hc = (_conv_call(glu, cw, cb, rows=R_PROMPT, groups=P_GROUPS, row_block0=0),
                  _conv_call(glu, cw, cb, rows=R_SAMPLE, groups=1, row_block0=1))
            w_out = od_w_out[li].astype(BF16)
            wr = jnp.zeros((D, LANES), F32).at[:, :N_EXPERTS].set(moe_w_router[li])
            br = jnp.zeros((1, LANES), F32).at[0, :N_EXPERTS].set(moe_b_router[li])
            wr_hi = wr.astype(BF16)
            wr_lo = (wr - wr_hi.astype(F32)).astype(BF16)
            x1, xloc, route, counts = _odd_out(x, hc, att, c_ln_g[li].reshape(1, D_C), c_ln_b[li].reshape(1, D_C),
                                     w_out[:D_C], w_out[D_C:], gate1, g2, shift2, scale2, wr_hi, wr_lo, br)
            y2 = _moe_experts(_route_plan(counts), xloc, *moe_wb)
            pending = (x1, y2, route, gate2)
            new_k.append(_prompt_to_batch_major(k).reshape(BATCH, SEQ, N_KV_HEADS, HEAD_DIM))
            new_v.append(_prompt_to_batch_major(v).reshape(BATCH, SEQ, N_KV_HEADS, HEAD_DIM))

    gf = norm_f.reshape(1, D)
    y_prompt = _final_norm(pending, gf, batch=BATCH, steps=SEQ, stream=0)
    y_sample = _final_norm(pending, gf, batch=DEC_BATCH, steps=DEC_SEQ, stream=1)
    return (y_prompt, y_sample, jnp.stack(new_states, axis=1), jnp.stack(new_k, axis=1), jnp.stack(new_v, axis=1))
```

```python
import functools

import jax
import jax.numpy as jnp
from jax import lax
from jax.experimental import pallas as pl
from jax.experimental.pallas import tpu as pltpu

F32 = jnp.float32
BF16 = jnp.bfloat16

D = 1024
BATCH = 32
SEQ = 256
DEPTH = 4
DEC_BATCH = 8
DEC_SEQ = 1024
PAST_LEN = 256
GRID_W = 64
EPS = 1e-6
D_A = 512
A_BLOCKS = 8
A_BS = 64
A_CONV = 4
A_C = 8.0
D_B = 512
POOL_WINDOWS = (2, 4, 8, 16)
B_GS = 128
D_C = 512
C_CONV = 31
HEAD_DIM = 64
N_Q_HEADS = 8
N_KV_HEADS = 2
D_ATT = 512
D_KV = 128
ROPE_THETA = 10000.0
D_FF = 2816
N_EXPERTS = 8
D_FF_E = 1408
D_IN_EVEN = 1536
D_IN_ODD = 1792

SUB = 8
LANES = 128
N_PROMPT = BATCH * SEQ
N_SAMPLE = DEC_BATCH * DEC_SEQ
N_TOK = N_PROMPT + N_SAMPLE
P_GROUPS = BATCH // SUB
R_PROMPT = SEQ * SUB
R_SAMPLE = DEC_SEQ * SUB
TM = 512
TILES_PER_STREAM = N_PROMPT // TM
MOE_TM = 512
MOE_TILES = 2 * N_TOK // MOE_TM + N_EXPERTS
CH = TM
N_CHUNKS = N_TOK // CH
SLAB = D // LANES
NEG_BIG = -3.0e38
VMEM_LIMIT = 56 * 1024 * 1024


def _params(n_axes, vmem=VMEM_LIMIT):
    return pltpu.CompilerParams(dimension_semantics=("arbitrary",) * n_axes, vmem_limit_bytes=vmem)


def _rms(x, g):
    ms = jnp.mean(x * x, axis=-1, keepdims=True)
    return x * lax.rsqrt(ms + EPS) * g


def _modulate(xn, shift, scale):
    tm = xn.shape[0]
    h = xn.reshape(tm // SUB, SUB, D) * (1.0 + scale)[None] + shift[None]
    return h.reshape(tm, D)


def _gated_add(x, gate, y):
    tm = x.shape[0]
    return x + (y.reshape(tm // SUB, SUB, D) * gate[None]).reshape(tm, D)


def _mod_spec():
    return pl.BlockSpec((None, SUB, D), lambda i, *_: (i // TILES_PER_STREAM, 0, 0))


def _row_spec(width, tm=TM):
    return pl.BlockSpec((tm, width), lambda i, *_: (i, 0))


def _full_spec(shape):
    nd = len(shape)
    return pl.BlockSpec(shape, lambda i, *_: (0,) * nd)


def _ada_kernel(c_ref, w_ref, b_ref, o_ref):
    c = c_ref[...]
    s = (c * jax.nn.sigmoid(c)).astype(BF16)
    o_ref[...] = jnp.dot(s, w_ref[...].astype(BF16), preferred_element_type=F32) + b_ref[...]


def _ada_params(cond16, w_mod, b_mod):
    return pl.pallas_call(
        _ada_kernel,
        grid=(DEPTH, 6),
        in_specs=[
            pl.BlockSpec((16, D), lambda l, j: (0, 0)),
            pl.BlockSpec((None, D, D), lambda l, j: (l, 0, j)),
            pl.BlockSpec((None, None, 1, D), lambda l, j: (l, j, 0, 0)),
        ],
        out_specs=pl.BlockSpec((None, None, 16, D), lambda l, j: (l, j, 0, 0)),
        out_shape=jax.ShapeDtypeStruct((DEPTH, 6, 16, D), F32),
        compiler_params=_params(2),
        name="ada_params",
    )(cond16, w_mod, b_mod.reshape(DEPTH, 6, 1, D))


T_TILE = TM // SUB


def _first_in_kernel(xp_ref, xs_ref, g_ref, sh_ref, sc_ref, w_ref, z_ref, x_ref, cols):
    is_sample = pl.program_id(0) >= TILES_PER_STREAM
    for b in range(SUB):
        xb = jnp.where(is_sample, xs_ref[b], xp_ref[b])
        for c in range(SLAB):
            cols[c, pl.ds(b, T_TILE, stride=SUB), :] = xb[:, c * LANES:(c + 1) * LANES]
    x = jnp.concatenate([cols[c] for c in range(SLAB)], axis=1)
    x_ref[...] = x
    h = _modulate(_rms(x, g_ref[...]), sh_ref[...], sc_ref[...])
    z_ref[...] = jnp.dot(h.astype(BF16), w_ref[...], preferred_element_type=F32)


def _first_in(x_prompt, x_sample, g, shift, scale, w_bf16):
    t_tiles = SEQ // T_TILE
    last_p = TILES_PER_STREAM - 1
    return pl.pallas_call(
        _first_in_kernel,
        grid=(N_TOK // TM,),
        in_specs=[
            pl.BlockSpec((SUB, T_TILE, D), lambda i: (jnp.minimum(i, last_p) // t_tiles,
                                                      jnp.minimum(i, last_p) % t_tiles, 0)),
            pl.BlockSpec((SUB, T_TILE, D), lambda i: (0, jnp.maximum(i - TILES_PER_STREAM, 0), 0)),
            _full_spec((1, D)), _mod_spec(), _mod_spec(), _full_spec((D, D_IN_EVEN))],
        out_specs=[_row_spec(D_IN_EVEN), _row_spec(D)],
        out_shape=[jax.ShapeDtypeStruct((N_TOK, D_IN_EVEN), F32), jax.ShapeDtypeStruct((N_TOK, D), F32)],
        scratch_shapes=[pltpu.VMEM((SLAB, TM, LANES), F32)],
        compiler_params=_params(1),
        name="first_in",
    )(x_prompt, x_sample, g, shift, scale, w_bf16)


RG_CHUNK = 256
RG_PAD = 16


def _rglru_kernel(u_ref, ga_ref, cw_ref, cb_ref, wg_ref, bg_ref, lam_ref, h0_ref,
                  y_ref, hl_ref, xpad, a0, b0, a1, b1, *, rows):
    steps = rows // SUB
    n_chunks = rows // RG_CHUNK

    xpad[0:RG_PAD, :] = jnp.zeros((RG_PAD, LANES), F32)
    xpad[RG_PAD + rows:RG_PAD + rows + RG_PAD, :] = jnp.zeros((RG_PAD, LANES), F32)

    def copy_chunk(c, carry):
        r = pl.multiple_of(c * RG_CHUNK, RG_CHUNK)
        xpad[pl.ds(RG_PAD + r, RG_CHUNK), :] = u_ref[pl.ds(r, RG_CHUNK), :]
        return carry

    lax.fori_loop(0, n_chunks, copy_chunk, 0)

    lam = lam_ref[...]
    softplus_neg = jnp.maximum(-lam, 0.0) + jnp.log1p(jnp.exp(-jnp.abs(lam)))
    decay = A_C * softplus_neg
    decay_log2 = decay * (-1.4426950408889634)
    a_refs = (a0, a1)
    b_refs = (b0, b1)

    def gate_chunk(c, carry):
        r = pl.multiple_of(c * RG_CHUNK, RG_CHUNK)
        xc = jnp.zeros((RG_CHUNK, LANES), F32) + cb_ref[...]
        for k in range(A_CONV):
            xc = xc + cw_ref[k:k + 1, :] * xpad[pl.ds(r + SUB * k, RG_CHUNK), :]
        pre = jnp.dot(xc.astype(BF16), wg_ref[...], preferred_element_type=F32) + bg_ref[...]
        for d in range(2):
            rg = 0.5 * jnp.tanh(0.5 * pre[:, d * 256:d * 256 + LANES]) + 0.5
            ig = 0.5 * jnp.tanh(0.5 * pre[:, d * 256 + LANES:(d + 1) * 256]) + 0.5
            a = jnp.exp2(rg * decay_log2[:, d * LANES:(d + 1) * LANES])
            one_minus_a2 = jnp.tanh(rg * decay[:, d * LANES:(d + 1) * LANES]) * (a * a + 1.0)
            root = one_minus_a2 * lax.rsqrt(jnp.maximum(one_minus_a2, 1e-30))
            a_refs[d][pl.ds(r, RG_CHUNK), :] = a
            b_refs[d][pl.ds(r, RG_CHUNK), :] = root * (ig * xc)
        return carry

    lax.fori_loop(0, n_chunks, gate_chunk, 0)

    def step(t, carry):
        hf, hb = carry
        rf = pl.multiple_of(t * SUB, SUB)
        rb = pl.multiple_of((steps - 1 - t) * SUB, SUB)
        hf = a0[pl.ds(rf, SUB), :] * hf + b0[pl.ds(rf, SUB), :]
        b0[pl.ds(rf, SUB), :] = hf
        hb = a1[pl.ds(rb, SUB), :] * hb + b1[pl.ds(rb, SUB), :]
        b1[pl.ds(rb, SUB), :] = hb
        return hf, hb

    hf, hb = lax.fori_loop(0, steps, step, (h0_ref[0], h0_ref[1]), unroll=8)
    hl_ref[0] = hf
    hl_ref[1] = hb

    def out_chunk(c, carry):
        r = pl.multiple_of(c * RG_CHUNK, RG_CHUNK)
        y = (b0[pl.ds(r, RG_CHUNK), :] + b1[pl.ds(r, RG_CHUNK), :]) * jax.nn.gelu(ga_ref[pl.ds(r, RG_CHUNK), :])
        y_ref[pl.ds(r, RG_CHUNK), :] = y.astype(BF16)
        return carry

    lax.fori_loop(0, n_chunks, out_chunk, 0)


def _rglru_call(z, cw, cb, wg, bg, lam, h0, *, rows, groups, row_block0):
    n_cb = D_A // LANES
    in_specs = [
        pl.BlockSpec((rows, LANES), lambda g, j: (row_block0 + g, j)),
        pl.BlockSpec((rows, LANES), lambda g, j: (row_block0 + g, n_cb + j)),
        pl.BlockSpec((A_CONV, LANES), lambda g, j: (0, j)),
        pl.BlockSpec((1, LANES), lambda g, j: (0, j)),
        pl.BlockSpec((None, LANES, 4 * LANES), lambda g, j: (j, 0, 0)),
        pl.BlockSpec((None, 1, 4 * LANES), lambda g, j: (j, 0, 0)),
        pl.BlockSpec((None, 1, 2 * LANES), lambda g, j: (j, 0, 0)),
        pl.BlockSpec((None, 2, SUB, LANES), lambda g, j: (g, 0, 0, j)),
    ]
    return pl.pallas_call(
        functools.partial(_rglru_kernel, rows=rows),
        grid=(groups, n_cb),
        in_specs=in_specs,
        out_specs=[
            pl.BlockSpec((rows, LANES), lambda g, j: (g, j)),
            pl.BlockSpec((None, 2, SUB, LANES), lambda g, j: (g, 0, 0, j)),
        ],
        out_shape=[
            jax.ShapeDtypeStruct((groups * rows, D_A), BF16),
            jax.ShapeDtypeStruct((groups, 2, SUB, D_A), F32),
        ],
        scratch_shapes=[pltpu.VMEM((rows + 2 * RG_PAD, LANES), F32)] + [pltpu.VMEM((rows, LANES), F32)] * 4,
        compiler_params=_params(2),
        name=f"rglru_{rows}",
    )(z, z, cw, cb, wg, bg, lam, h0)


POOL_CHUNK = 256
POOL_PAD = 64


def _pool_kernel(u_ref, w_ref, s_ref, y_ref, xpad, *, rows):
    steps = rows // SUB
    n_chunks = rows // POOL_CHUNK
    j = pl.program_id(1)

    xpad[0:POOL_PAD, :] = jnp.zeros((POOL_PAD, LANES), F32)
    xpad[POOL_PAD + rows:POOL_PAD + rows + POOL_PAD, :] = jnp.zeros((POOL_PAD, LANES), F32)

    def copy_chunk(c, carry):
        r = pl.multiple_of(c * POOL_CHUNK, POOL_CHUNK)
        xpad[pl.ds(POOL_PAD + r, POOL_CHUNK), :] = u_ref[pl.ds(r, POOL_CHUNK), :]
        return carry

    lax.fori_loop(0, n_chunks, copy_chunk, 0)

    for gi, win in enumerate(POOL_WINDOWS):
        half = win // 2

        @pl.when(j == gi)
        def _(half=half, win=win):
            def chunk(c, carry):
                r = pl.multiple_of(c * POOL_CHUNK, POOL_CHUNK)
                acc = xpad[pl.ds(POOL_PAD + r - SUB * half, POOL_CHUNK), :]
                for s in range(1, win):
                    acc = acc + xpad[pl.ds(POOL_PAD + r + SUB * (s - half), POOL_CHUNK), :]
                row = r + lax.broadcasted_iota(jnp.int32, (POOL_CHUNK, LANES), 0)
                t = lax.shift_right_logical(row, SUB.bit_length() - 1)
                cnt = jnp.minimum(t + half, steps) - jnp.maximum(t - half, 0)
                dlt = acc / cnt.astype(F32) - xpad[pl.ds(POOL_PAD + r, POOL_CHUNK), :]
                y = jnp.dot(dlt.astype(BF16), w_ref[...], preferred_element_type=F32) * s_ref[...]
                y_ref[pl.ds(r, POOL_CHUNK), :] = y.astype(BF16)
                return carry

            lax.fori_loop(0, n_chunks, chunk, 0)


def _pool_call(z, w_pool_bf16, s_pool, *, rows, groups, row_block0):
    n_cb = D_B // LANES
    col0 = 2 * D_A // LANES
    in_specs = [
        pl.BlockSpec((rows, LANES), lambda g, j: (row_block0 + g, col0 + j)),
        pl.BlockSpec((None, B_GS, B_GS), lambda g, j: (j, 0, 0)),
        pl.BlockSpec((1, LANES), lambda g, j: (0, j)),
    ]
    return pl.pallas_call(
        functools.partial(_pool_kernel, rows=rows),
        grid=(groups, n_cb),
        in_specs=in_specs,
        out_specs=pl.BlockSpec((rows, LANES), lambda g, j: (g, j)),
        out_shape=jax.ShapeDtypeStruct((groups * rows, D_B), BF16),
        scratch_shapes=[pltpu.VMEM((rows + 2 * POOL_PAD, LANES), F32)],
        compiler_params=_params(2),
        name=f"pool_{rows}",
    )(z, w_pool_bf16, s_pool)


def _stream_specs(width):
    prompt = pl.BlockSpec((TM, width), lambda i, *_: (jnp.minimum(i, TILES_PER_STREAM - 1), 0))
    sample = pl.BlockSpec((TM, width), lambda i, *_: (jnp.maximum(i - TILES_PER_STREAM, 0), 0))
    return [prompt, sample]


def _pick_stream(prompt_ref, sample_ref):
    return jnp.where(pl.program_id(0) >= TILES_PER_STREAM, sample_ref[...], prompt_ref[...])


def _even_tail_kernel(x_ref, yap_ref, yas_ref, ybp_ref, ybs_ref, wa_ref, wb_ref, gate1_ref,
                      g2_ref, sh_ref, sc_ref, gate2_ref, w1_ref, w3_ref, w2_ref, *rest):
    e1_ref, e3_ref, e2_ref, o_ref, e13_out, e2_out = rest
    e13_out[:, :D_FF_E] = e1_ref[...].astype(BF16)
    e13_out[:, D_FF_E:] = e3_ref[...].astype(BF16)
    e2_out[...] = e2_ref[...].astype(BF16)
    y = jnp.dot(_pick_stream(yap_ref, yas_ref), wa_ref[...], preferred_element_type=F32)
    y = y + jnp.dot(_pick_stream(ybp_ref, ybs_ref), wb_ref[...], preferred_element_type=F32)
    x1 = _gated_add(x_ref[...], gate1_ref[...], y)
    h = _modulate(_rms(x1, g2_ref[...]), sh_ref[...], sc_ref[...]).astype(BF16)
    a = jnp.dot(h, w1_ref[...], preferred_element_type=F32)
    b = jnp.dot(h, w3_ref[...], preferred_element_type=F32)
    u = (a * jax.nn.sigmoid(a) * b).astype(BF16)
    f = jnp.dot(u, w2_ref[...], preferred_element_type=F32)
    o_ref[...] = _gated_add(x1, gate2_ref[...], f)


def _even_tail(x, ya, yb, wa, wb, gate1, g2, shift2, scale2, gate2, w1, w3, w2, li, moe_w):
    half = D // 2
    steps = N_TOK // TM
    resident = pl.Buffered(1)
    cast_in_specs, cast_args = [], []
    for w in moe_w:
        _, n_e, rows, cols = w.shape
        cast_args.append(w.reshape(w.shape[0], n_e * rows, cols))
        cast_in_specs.append(pl.BlockSpec((None, n_e * rows // steps, cols), lambda i: (li, i, 0)))
    r13, r2 = N_EXPERTS * D // steps, N_EXPERTS * D_FF_E // steps
    cast_out_specs = [pl.BlockSpec((r13, 2 * D_FF_E), lambda i: (i, 0)), pl.BlockSpec((r2, D), lambda i: (i, 0))]
    cast_shapes = [jax.ShapeDtypeStruct((N_EXPERTS * D, 2 * D_FF_E), BF16),
                   jax.ShapeDtypeStruct((N_EXPERTS * D_FF_E, D), BF16)]
    outs = pl.pallas_call(
        _even_tail_kernel,
        grid=(steps,),
        in_specs=[_row_spec(D)] + _stream_specs(half) + _stream_specs(half) +
                 [pl.BlockSpec((half, D), lambda i: (0, 0), pipeline_mode=resident),
                  pl.BlockSpec((half, D), lambda i: (0, 0), pipeline_mode=resident),
                  _mod_spec(), _full_spec((1, D)), _mod_spec(), _mod_spec(), _mod_spec(),
                  pl.BlockSpec((None, D, D_FF), lambda i: (li, 0, 0), pipeline_mode=resident),
                  pl.BlockSpec((None, D, D_FF), lambda i: (li, 0, 0), pipeline_mode=resident),
                  pl.BlockSpec((None, D_FF, D), lambda i: (li, 0, 0), pipeline_mode=resident)] + cast_in_specs,
        out_specs=[_row_spec(D)] + cast_out_specs,
        out_shape=[jax.ShapeDtypeStruct((N_TOK, D), F32)] + cast_shapes,
        compiler_params=_params(1),
        name="even_tail",
    )(x, *ya, *yb, wa, wb, gate1, g2, shift2, scale2, gate2, w1, w3, w2, *cast_args)
    return outs[0], (outs[1].reshape(N_EXPERTS, D, 2 * D_FF_E), outs[2].reshape(N_EXPERTS, D_FF_E, D))


def _head_rms(x, ones_bd, g):
    sq = x * x
    hi = sq.astype(BF16)
    lo = (sq - hi.astype(F32)).astype(BF16)
    ms = jnp.dot(hi, ones_bd, preferred_element_type=F32) + jnp.dot(lo, ones_bd, preferred_element_type=F32)
    return x * lax.rsqrt(ms + EPS) * g


def _rope(x, cos, sin_signed):
    w = x.shape[-1]
    lane = lax.broadcasted_iota(jnp.int32, x.shape, 1)
    first = (lane % 32) < 16
    partner = jnp.where(first, pltpu.roll(x, w - 16, 1), pltpu.roll(x, 16, 1))
    return x * cos + partner * sin_signed


Q_BLOCKS = D_ATT // LANES


def _odd_in_kernel(x_ref, g_ref, sh_ref, sc_ref, w_ref, bdq_ref, bdk_ref, qg_ref, kg_ref, cos_ref, sin_ref,
                   glu_ref, q0_ref, q1_ref, q2_ref, q3_ref, k_ref, v_ref):
    i = pl.program_id(0)
    h = _modulate(_rms(x_ref[...], g_ref[...]), sh_ref[...], sc_ref[...])
    z = jnp.dot(h.astype(BF16), w_ref[...], preferred_element_type=F32)
    glu_ref[...] = z[:, :D_C] * jax.nn.sigmoid(z[:, D_C:2 * D_C])
    o1 = 2 * D_C
    o2 = o1 + D_ATT
    o3 = o2 + D_KV
    q = _head_rms(z[:, o1:o2], bdq_ref[...], qg_ref[...])
    k = _head_rms(z[:, o2:o3], bdk_ref[...], kg_ref[...])
    cos = cos_ref[...]
    sin = sin_ref[...]
    is_sample = i >= TILES_PER_STREAM
    q_r = _rope(q, jnp.concatenate([cos] * 4, axis=1), jnp.concatenate([sin] * 4, axis=1))
    k_r = _rope(k, cos, sin)
    q = jnp.where(is_sample, q_r, q)
    for c, q_ref in enumerate((q0_ref, q1_ref, q2_ref, q3_ref)):
        q_ref[...] = q[:, c * LANES:(c + 1) * LANES]
    k_ref[...] = jnp.where(is_sample, k_r, k)
    v_ref[...] = z[:, o3:]


def _odd_in(x, g, shift, scale, w_bf16, bdq, bdk, qg, kg, cos_t, sin_t):
    rope_spec = pl.BlockSpec((TM, D_KV), lambda i: (jnp.maximum(i - TILES_PER_STREAM, 0), 0))
    return pl.pallas_call(
        _odd_in_kernel,
        grid=(N_TOK // TM,),
        in_specs=[_row_spec(D), _full_spec((1, D)), _mod_spec(), _mod_spec(), _full_spec((D, D_IN_ODD)),
                  _full_spec((D_ATT, D_ATT)), _full_spec((D_KV, D_KV)), _full_spec((1, D_ATT)),
                  _full_spec((1, D_KV)), rope_spec, rope_spec],
        out_specs=[_row_spec(D_C)] + [_row_spec(LANES)] * (Q_BLOCKS + 2),
        out_shape=[jax.ShapeDtypeStruct((N_TOK, D_C), F32)] +
                  [jax.ShapeDtypeStruct((N_TOK, LANES), F32)] * (Q_BLOCKS + 2),
        compiler_params=_params(1),
        name="odd_in",
    )(x, g, shift, scale, w_bf16, bdq, bdk, qg, kg, cos_t, sin_t)


def _attend(q_refs, o_refs, q_rows, k_all, v_all):
    scale = HEAD_DIM ** -0.5 * 1.4426950408889634
    heads_per_block = LANES // HEAD_DIM
    group = N_Q_HEADS // N_KV_HEADS
    kv = []
    for h in range(N_KV_HEADS):
        kv.append((k_all[:, h * HEAD_DIM:(h + 1) * HEAD_DIM].astype(BF16),
                   v_all[:, h * HEAD_DIM:(h + 1) * HEAD_DIM].astype(BF16)))
    for c in range(Q_BLOCKS):
        qc = (q_refs[c][q_rows, :] * scale).astype(BF16)
        outs = []
        for sub in range(heads_per_block):
            kh, vh = kv[(c * heads_per_block + sub) // group]
            qh = qc[:, sub * HEAD_DIM:(sub + 1) * HEAD_DIM]
            s = lax.dot_general(qh, kh, (((1,), (1,)), ((), ())), preferred_element_type=F32)
            m = jnp.max(s, axis=-1, keepdims=True)
            p = jnp.exp2(s - m)
            l = jnp.sum(p, axis=-1, keepdims=True)
            outs.append(jnp.dot(p.astype(BF16), vh, preferred_element_type=F32) / l)
        o_refs[c][q_rows, :] = jnp.concatenate(outs, axis=1)


def _attn_prompt_kernel(*refs):
    q_refs, (k_ref, v_ref), o_refs = refs[:Q_BLOCKS], refs[Q_BLOCKS:Q_BLOCKS + 2], refs[Q_BLOCKS + 2:]
    rows = pl.ds(pl.program_id(1), SEQ, stride=SUB)
    _attend(q_refs, o_refs, rows, k_ref[rows, :], v_ref[rows, :])


def _attn_sample_kernel(*refs):
    q_refs, (k_ref, v_ref, ck_ref, cv_ref), o_refs = refs[:Q_BLOCKS], refs[Q_BLOCKS:Q_BLOCKS + 4], refs[Q_BLOCKS + 4:]
    b = pl.program_id(1)
    kv_rows = pl.ds(b, DEC_SEQ, stride=SUB)
    k_all = jnp.concatenate([ck_ref[...], k_ref[kv_rows, :]], axis=0)
    v_all = jnp.concatenate([cv_ref[...], v_ref[kv_rows, :]], axis=0)
    _attend(q_refs, o_refs, pl.ds(b, SEQ, stride=SUB), k_all, v_all)


def _attention(qs, k, v, cache_k_l, cache_v_l):
    chunk = pl.BlockSpec((R_PROMPT, LANES), lambda g, b: (g, 0))
    out_shape = [jax.ShapeDtypeStruct((N_PROMPT, LANES), F32)] * Q_BLOCKS
    att_p = pl.pallas_call(
        _attn_prompt_kernel,
        grid=(P_GROUPS, SUB),
        in_specs=[chunk] * (Q_BLOCKS + 2),
        out_specs=[chunk] * Q_BLOCKS,
        out_shape=out_shape,
        compiler_params=_params(2),
        name="attn_prompt",
    )(*qs, k, v)
    q_chunks = DEC_SEQ // SEQ
    q_chunk = pl.BlockSpec((R_PROMPT, LANES), lambda c, b: (P_GROUPS + c, 0))
    kv_all = pl.BlockSpec((R_SAMPLE, LANES), lambda c, b: (1, 0))
    cache = pl.BlockSpec((None, PAST_LEN, D_KV), lambda c, b: (b, 0, 0))
    att_s = pl.pallas_call(
        _attn_sample_kernel,
        grid=(q_chunks, DEC_BATCH),
        in_specs=[q_chunk] * Q_BLOCKS + [kv_all, kv_all, cache, cache],
        out_specs=[pl.BlockSpec((R_PROMPT, LANES), lambda c, b: (c, 0))] * Q_BLOCKS,
        out_shape=out_shape,
        compiler_params=_params(2),
        name="attn_sample",
    )(*qs, k, v, cache_k_l, cache_v_l)
    return att_p, att_s


CV_CHUNK = 128
CV_PAD = (C_CONV // 2) * SUB


def _conv_kernel(u_ref, w_ref, b_ref, y_ref, xpad, *, rows):
    n_copy = rows // 256
    xpad[0:CV_PAD, :] = jnp.zeros((CV_PAD, LANES), F32)
    xpad[CV_PAD + rows:CV_PAD + rows + CV_PAD, :] = jnp.zeros((CV_PAD, LANES), F32)

    def copy_chunk(c, carry):
        r = pl.multiple_of(c * 256, 256)
        xpad[pl.ds(CV_PAD + r, 256), :] = u_ref[pl.ds(r, 256), :]
        return carry

    lax.fori_loop(0, n_copy, copy_chunk, 0)

    def chunk(c, carry):
        r = pl.multiple_of(c * CV_CHUNK, CV_CHUNK)
        acc = jnp.zeros((CV_CHUNK, LANES), F32) + b_ref[...]
        for k in range(C_CONV):
            acc = acc + w_ref[k:k + 1, :] * xpad[pl.ds(r + SUB * k, CV_CHUNK), :]
        y_ref[pl.ds(r, CV_CHUNK), :] = acc
        return carry

    lax.fori_loop(0, rows // CV_CHUNK, chunk, 0)


def _conv_call(glu, w, b, *, rows, groups, row_block0):
    n_cb = D_C // LANES
    in_specs = [
        pl.BlockSpec((rows, LANES), lambda g, j: (row_block0 + g, j)),
        pl.BlockSpec((C_CONV, LANES), lambda g, j: (0, j)),
        pl.BlockSpec((1, LANES), lambda g, j: (0, j)),
    ]
    return pl.pallas_call(
        functools.partial(_conv_kernel, rows=rows),
        grid=(groups, n_cb),
        in_specs=in_specs,
        out_specs=pl.BlockSpec((rows, LANES), lambda g, j: (g, j)),
        out_shape=jax.ShapeDtypeStruct((groups * rows, D_C), F32),
        scratch_shapes=[pltpu.VMEM((rows + 2 * CV_PAD, LANES), F32)],
        compiler_params=_params(2),
        name=f"conv_{rows}",
    )(glu, w, b)


def _odd_out_kernel(x_ref, hcp_ref, hcs_ref, *refs):
    att_refs, refs = refs[:2 * Q_BLOCKS], refs[2 * Q_BLOCKS:]
    (lng_ref, lnb_ref, wc_ref, wa_ref, gate_ref, g2_ref, sh_ref, sc_ref, wrh_ref, wrl_ref, br_ref, tril_ref,
     x1_ref, xloc_ref, route_ref, cnt_ref) = refs
    att = jnp.concatenate([_pick_stream(att_refs[c], att_refs[Q_BLOCKS + c]) for c in range(Q_BLOCKS)], axis=1)
    hc = _pick_stream(hcp_ref, hcs_ref)
    mu = jnp.mean(hc, axis=-1, keepdims=True)
    xc = hc - mu
    var = jnp.mean(xc * xc, axis=-1, keepdims=True)
    ln = xc * lax.rsqrt(var + EPS) * lng_ref[...] + lnb_ref[...]
    yc = (ln * jax.nn.sigmoid(ln)).astype(BF16)
    y = jnp.dot(yc, wc_ref[...], preferred_element_type=F32)
    y = y + jnp.dot(att.astype(BF16), wa_ref[...], preferred_element_type=F32)
    x1 = _gated_add(x_ref[...], gate_ref[...], y)
    x1_ref[...] = x1
    h2 = _modulate(_rms(x1, g2_ref[...]), sh_ref[...], sc_ref[...])
    h2_hi = h2.astype(BF16)
    h2_lo = (h2 - h2_hi.astype(F32)).astype(BF16)
    logits = (jnp.dot(h2_hi, wrh_ref[...], preferred_element_type=F32)
              + jnp.dot(h2_lo, wrh_ref[...], preferred_element_type=F32)
              + jnp.dot(h2_hi, wrl_ref[...], preferred_element_type=F32)) + br_ref[...]
    lane = lax.broadcasted_iota(jnp.int32, logits.shape, 1).astype(F32)
    lg = jnp.where(lane < N_EXPERTS, logits, NEG_BIG)
    m1 = jnp.max(lg, axis=-1, keepdims=True)
    i1 = jnp.min(jnp.where(lg == m1, lane, float(LANES)), axis=-1, keepdims=True)
    lg2 = jnp.where(lane == i1, NEG_BIG, lg)
    m2 = jnp.max(lg2, axis=-1, keepdims=True)
    i2 = jnp.min(jnp.where(lg2 == m2, lane, float(LANES)), axis=-1, keepdims=True)
    e = jnp.exp(m2 - m1)
    den = 1.0 + e

    q = lax.broadcasted_iota(jnp.int32, (CH, 2 * CH), 1).astype(F32)
    route_ref[...] = jnp.where(lane == 0.0, i1, jnp.where(lane == 1.0, i2,
                               jnp.where(lane == 2.0, 1.0 / den, jnp.where(lane == 3.0, e / den, 0.0))))
    lane_c = lax.broadcasted_iota(jnp.int32, (CH, LANES), 1).astype(F32)
    for c in range(TM // CH):
        rows = slice(c * CH, (c + 1) * CH)
        i1c = route_ref[rows, 0:1]
        i2c = route_ref[rows, 1:2]
        e1 = jnp.where(lane_c == i1c, 1.0, 0.0)
        e2 = jnp.where(lane_c == i2c, 1.0, 0.0)
        before1 = jnp.dot(tril_ref[...], e1.astype(BF16), preferred_element_type=F32)
        before2 = jnp.dot(tril_ref[...], e2.astype(BF16), preferred_element_type=F32)
        cnt1 = jnp.sum(e1, axis=0, keepdims=True)
        cnt2 = jnp.sum(e2, axis=0, keepdims=True)
        start = (jnp.sum(jnp.where(i1c < lane_c, 1.0, 0.0), axis=0, keepdims=True)
                 + jnp.sum(jnp.where(i2c < lane_c, 1.0, 0.0), axis=0, keepdims=True))
        pos0 = jnp.sum(e1 * (start + before1), axis=-1, keepdims=True)
        pos1 = jnp.sum(e2 * (start + cnt1 + before2), axis=-1, keepdims=True)
        route_ref[rows, :] = jnp.where(lane_c == 4.0, pos0, jnp.where(lane_c == 5.0, pos1, route_ref[rows, :]))
        cnt_ref[c * SUB:(c + 1) * SUB, :] = jnp.broadcast_to(cnt1 + cnt2, (SUB, LANES))
        sel = jnp.where(jnp.logical_or(q == pos0, q == pos1), 1.0, 0.0).astype(BF16)
        xloc = lax.dot_general(sel, h2_hi[rows], (((0,), (0,)), ((), ())), preferred_element_type=F32)
        for j in range(SLAB):
            xloc_ref[pl.ds(c * 2 * CH * SLAB + j, 2 * CH, stride=SLAB), :] = xloc[:, j * LANES:(j + 1) * LANES]


def _odd_out(x, hc, att, lng, lnb, wc, wa, gate, g2, shift2, scale2, wr_hi, wr_lo, br_pad):
    half = D // 2
    att_specs = [_stream_specs(LANES)[0]] * Q_BLOCKS + [_stream_specs(LANES)[1]] * Q_BLOCKS
    return pl.pallas_call(
        _odd_out_kernel,
        grid=(N_TOK // TM,),
        in_specs=[_row_spec(D)] + _stream_specs(half) + att_specs +
                 [_full_spec((1, half)), _full_spec((1, half)),
                  _full_spec((half, D)), _full_spec((half, D)), _mod_spec(),
                  _full_spec((1, D)), _mod_spec(), _mod_spec(),
                  _full_spec((D, LANES)), _full_spec((D, LANES)), _full_spec((1, LANES)), _full_spec((CH, CH))],
        out_specs=[_row_spec(D), pl.BlockSpec((2 * TM * SLAB, LANES), lambda i: (i, 0)), _row_spec(LANES),
                   pl.BlockSpec((TM // CH * SUB, LANES), lambda i: (i, 0))],
        out_shape=[jax.ShapeDtypeStruct((N_TOK, D), F32), jax.ShapeDtypeStruct((2 * N_TOK * SLAB, LANES), F32),
                   jax.ShapeDtypeStruct((N_TOK, LANES), F32),
                   jax.ShapeDtypeStruct((N_CHUNKS * SUB, LANES), F32)],
        compiler_params=_params(1),
        name="odd_out",
    )(x, *hc, *att[0], *att[1], lng, lnb, wc, wa, gate, g2, shift2, scale2, wr_hi, wr_lo, br_pad,
      jnp.tril(jnp.ones((CH, CH), BF16), -1))


def _moe_kernel(te_ref, nv_ref, nu_ref, c0_ref, c1_ref, src_ref, dst_ref, len_ref,
                xloc_hbm, w13_ref, w2_ref, yloc_hbm, xbuf, obuf, gsem, ssem):
    del te_ref
    i = pl.program_id(0)
    n_used = nu_ref[0]
    slot = i % 2
    other = 1 - slot

    def rows(start, n):
        return pl.ds(pl.multiple_of(start * SLAB, SLAB), n * SLAB)

    def for_pieces(tile, fn):
        first = c0_ref[tile]

        def body(c, carry):
            k = tile * N_CHUNKS + first + c
            n = len_ref[k]

            @pl.when(n > 0)
            def _():
                fn(src_ref[k], dst_ref[k], n)

            return carry

        lax.fori_loop(0, c1_ref[tile] - first, body, 0)

    def start_gather(tile, s):
        for_pieces(tile, lambda src, dst, n: pltpu.make_async_copy(
            xloc_hbm.at[rows(src, n), :], xbuf.at[s, rows(dst, n), :], gsem.at[s]).start())

    def start_scatter(tile, s):
        for_pieces(tile, lambda src, dst, n: pltpu.make_async_copy(
            obuf.at[s, rows(dst, n), :], yloc_hbm.at[rows(src, n), :], ssem.at[s]).start())

    def wait_gather(s, n):
        pltpu.make_async_copy(xloc_hbm.at[rows(0, n), :], xbuf.at[s, rows(0, n), :], gsem.at[s]).wait()

    def wait_scatter(s, n):
        pltpu.make_async_copy(obuf.at[s, rows(0, n), :], yloc_hbm.at[rows(0, n), :], ssem.at[s]).wait()

    @pl.when(i == 0)
    def _():
        xbuf[...] = jnp.zeros(xbuf.shape, F32)
        start_gather(0, 0)

    @pl.when(i < n_used)
    def _():
        wait_gather(slot, nv_ref[i])

        @pl.when(i + 1 < n_used)
        def _():
            start_gather(i + 1, other)

        @pl.when(i >= 2)
        def _():
            wait_scatter(slot, nv_ref[jnp.maximum(i - 2, 0)])

        x = jnp.concatenate([xbuf[slot, pl.ds(j, MOE_TM, stride=SLAB), :] for j in range(SLAB)], axis=1)
        h = x.astype(BF16)
        ab = jnp.dot(h, w13_ref[...], preferred_element_type=F32)
        a = ab[:, :D_FF_E]
        u = (a * jax.nn.sigmoid(a) * ab[:, D_FF_E:]).astype(BF16)
        y = jnp.dot(u, w2_ref[...], preferred_element_type=F32)
        for j in range(SLAB):
            obuf[slot, pl.ds(j, MOE_TM, stride=SLAB), :] = y[:, j * LANES:(j + 1) * LANES]
        start_scatter(i, slot)

        @pl.when(i == n_used - 1)
        def _():
            @pl.when(i >= 1)
            def _():
                wait_scatter(other, nv_ref[jnp.maximum(i - 1, 0)])

            wait_scatter(slot, nv_ref[i])


def _moe_experts(plan, xloc, w13, w2):
    def weight(rows, cols):
        return pl.BlockSpec((None, rows, cols), lambda i, te, *_: (te[i], 0, 0))

    grid_spec = pltpu.PrefetchScalarGridSpec(
        num_scalar_prefetch=len(plan),
        grid=(MOE_TILES,),
        in_specs=[pl.BlockSpec(memory_space=pl.ANY), weight(D, 2 * D_FF_E), weight(D_FF_E, D)],
        out_specs=pl.BlockSpec(memory_space=pl.ANY),
        scratch_shapes=[
            pltpu.VMEM((2, MOE_TM * SLAB, LANES), F32),
            pltpu.VMEM((2, MOE_TM * SLAB, LANES), F32),
            pltpu.SemaphoreType.DMA((2,)),
            pltpu.SemaphoreType.DMA((2,)),
        ],
    )
    return pl.pallas_call(
        _moe_kernel,
        grid_spec=grid_spec,
        out_shape=jax.ShapeDtypeStruct((2 * N_TOK * SLAB, LANES), F32),
        compiler_params=_params(1),
        name="moe_experts",
    )(*plan, xloc, w13, w2)


def _route_plan(counts):
    cnt = counts.reshape(N_CHUNKS, SUB, LANES)[:, 0, :N_EXPERTS].astype(jnp.int32)
    run_local = jnp.cumsum(cnt, axis=1) - cnt
    run_group = jnp.cumsum(cnt, axis=0) - cnt
    total = jnp.sum(cnt, axis=0)
    tiles = (total + MOE_TM - 1) // MOE_TM
    tile_end = jnp.cumsum(tiles)
    tile_ids = jnp.arange(MOE_TILES, dtype=jnp.int32)
    tile_expert = jnp.minimum(jnp.sum((tile_ids[:, None] >= tile_end[None, :]).astype(jnp.int32), axis=1),
                              N_EXPERTS - 1)
    first = (tile_ids - (tile_end - tiles)[tile_expert]) * MOE_TM
    used = tile_ids < tile_end[-1]
    tile_valid = jnp.where(used, jnp.clip(total[tile_expert] - first, 0, MOE_TM), 0).astype(jnp.int32)
    a = run_group[:, tile_expert].T
    n = cnt[:, tile_expert].T
    lo = jnp.maximum(a, first[:, None])
    hi = jnp.minimum(a + n, first[:, None] + MOE_TM)
    length = jnp.where(used[:, None], jnp.maximum(hi - lo, 0), 0)
    chunk_row0 = jnp.arange(N_CHUNKS, dtype=jnp.int32)[None, :] * (2 * CH)
    src = jnp.where(length > 0, chunk_row0 + run_local[:, tile_expert].T + (lo - a), 0)
    dst = jnp.where(length > 0, lo - first[:, None], 0)
    n_used = tile_end[-1:].astype(jnp.int32)
    seen = jnp.cumsum((length > 0).astype(jnp.int32), axis=1)
    c_first = jnp.sum((seen == 0).astype(jnp.int32), axis=1)
    c_stop = N_CHUNKS - jnp.sum(jnp.logical_and(seen == seen[:, -1:], length == 0).astype(jnp.int32), axis=1)
    flat = lambda t: t.astype(jnp.int32).reshape(-1)
    return tile_expert, tile_valid, n_used, c_first, c_stop, flat(src), flat(dst), flat(length)


def _moe_combined(x_ref, y_ref, route_ref, gate_ref):
    q = lax.broadcasted_iota(jnp.int32, (CH, 2 * CH), 1).astype(F32)
    mix = []
    for c in range(TM // CH):
        y = jnp.concatenate([y_ref[pl.ds(c * 2 * CH * SLAB + j, 2 * CH, stride=SLAB), :] for j in range(SLAB)],
                            axis=1).astype(BF16)
        rows = slice(c * CH, (c + 1) * CH)
        pick = jnp.where(q == route_ref[rows, 4:5], route_ref[rows, 2:3],
                         jnp.where(q == route_ref[rows, 5:6], route_ref[rows, 3:4], 0.0)).astype(BF16)
        mix.append(jnp.dot(pick, y, preferred_element_type=F32))
    return _gated_add(x_ref[...], gate_ref[...], jnp.concatenate(mix, axis=0))


def _moe_pending_specs(tile0=0, stream=None):
    gate = _mod_spec() if stream is None else pl.BlockSpec((None, SUB, D), lambda i: (stream, 0, 0))
    return [pl.BlockSpec((TM, D), lambda i: (tile0 + i, 0)),
            pl.BlockSpec((TM * 2 * SLAB, LANES), lambda i: (tile0 + i, 0)),
            pl.BlockSpec((TM, LANES), lambda i: (tile0 + i, 0)), gate]


def _even_in_moe_kernel(x1_ref, y_ref, route_ref, gate_ref, g_ref, sh_ref, sc_ref, w_ref, z_ref, x_ref):
    x = _moe_combined(x1_ref, y_ref, route_ref, gate_ref)
    x_ref[...] = x
    h = _modulate(_rms(x, g_ref[...]), sh_ref[...], sc_ref[...])
    z_ref[...] = jnp.dot(h.astype(BF16), w_ref[...], preferred_element_type=F32)


def _even_in_moe(pending, g, shift, scale, w_bf16):
    return pl.pallas_call(
        _even_in_moe_kernel,
        grid=(N_TOK // TM,),
        in_specs=_moe_pending_specs() + [_full_spec((1, D)), _mod_spec(), _mod_spec(), _full_spec((D, D_IN_EVEN))],
        out_specs=[_row_spec(D_IN_EVEN), _row_spec(D)],
        out_shape=[jax.ShapeDtypeStruct((N_TOK, D_IN_EVEN), F32), jax.ShapeDtypeStruct((N_TOK, D), F32)],
        compiler_params=_params(1),
        name="even_in_moe",
    )(*pending, g, shift, scale, w_bf16)


def _final_kernel(x1_ref, y_ref, route_ref, gate_ref, g_ref, o_ref, cols):
    y = _rms(_moe_combined(x1_ref, y_ref, route_ref, gate_ref), g_ref[...])
    for c in range(SLAB):
        cols[c] = y[:, c * LANES:(c + 1) * LANES]
    for b in range(SUB):
        for c in range(SLAB):
            o_ref[b, :, c * LANES:(c + 1) * LANES] = cols[c, pl.ds(b, T_TILE, stride=SUB), :]


def _final_norm(pending, g, *, batch, steps, stream):
    t_tiles = steps // T_TILE
    return pl.pallas_call(
        _final_kernel,
        grid=(batch * steps // TM,),
        in_specs=_moe_pending_specs(stream * TILES_PER_STREAM, stream) + [_full_spec((1, D))],
        out_specs=pl.BlockSpec((SUB, T_TILE, D), lambda i: (i // t_tiles, i % t_tiles, 0)),
        out_shape=jax.ShapeDtypeStruct((batch, steps, D), F32),
        scratch_shapes=[pltpu.VMEM((SLAB, TM, LANES), F32)],
        compiler_params=_params(1),
        name=f"final_norm_{steps}",
    )(*pending, g)


def _block_diag_gates(w_r, w_i, b_r, b_i):
    def bd(w):
        w4 = w.reshape(4, 2, A_BS, A_BS)
        z = jnp.zeros((4, A_BS, A_BS), w.dtype)
        top = jnp.concatenate([w4[:, 0], z], axis=2)
        bot = jnp.concatenate([z, w4[:, 1]], axis=2)
        return jnp.concatenate([top, bot], axis=1)

    wg = jnp.concatenate([bd(w_r[0]), bd(w_i[0]), bd(w_r[1]), bd(w_i[1])], axis=2)
    bg = jnp.concatenate([b_r[0].reshape(4, 1, LANES), b_i[0].reshape(4, 1, LANES),
                          b_r[1].reshape(4, 1, LANES), b_i[1].reshape(4, 1, LANES)], axis=2)
    return wg.astype(BF16), bg


def _head_mean_matrix(width):
    idx = jnp.arange(width) // HEAD_DIM
    return ((idx[:, None] == idx[None, :]).astype(F32) / HEAD_DIM).astype(BF16)


def _rope_tables():
    pos = jnp.arange(DEC_SEQ)
    row = (pos // GRID_W).astype(F32)
    col = (pos % GRID_W).astype(F32)
    n_freq = HEAD_DIM // 4
    inv = ROPE_THETA ** (-jnp.arange(n_freq, dtype=F32) / n_freq)
    ang = jnp.stack([row[:, None] * inv, col[:, None] * inv], axis=1)
    cos = jnp.cos(ang)
    sin = jnp.sin(ang)
    cos_h = jnp.stack([cos, cos], axis=2).reshape(DEC_SEQ, HEAD_DIM)
    sin_h = jnp.stack([-sin, sin], axis=2).reshape(DEC_SEQ, HEAD_DIM)
    cos_t = jnp.tile(cos_h, (1, 2))
    sin_t = jnp.tile(sin_h, (1, 2))
    cos_t = jnp.broadcast_to(cos_t[:, None, :], (DEC_SEQ, SUB, D_KV)).reshape(N_SAMPLE, D_KV)
    sin_t = jnp.broadcast_to(sin_t[:, None, :], (DEC_SEQ, SUB, D_KV)).reshape(N_SAMPLE, D_KV)
    return cos_t, sin_t


def _prompt_to_batch_major(a):
    w = a.shape[-1]
    return a[:N_PROMPT].reshape(P_GROUPS, SEQ, SUB, w).transpose(0, 2, 1, 3).reshape(BATCH, SEQ, w)


def kernel(x_prompt, x_sample, c, state_rglru, cache_k, cache_v, c_ctx, w_mod, b_mod, norm1, norm2, ev_w_in, a_conv_w, a_conv_b, a_w_r, a_b_r, a_w_i, a_b_i, a_lam, b_w_pool, b_scale, ev_w_out, od_w_in, c_conv_w, c_conv_b, c_ln_g, c_ln_b, q_norm, k_norm, od_w_out, ff_w1, ff_w3, ff_w2, moe_w_router, moe_b_router, moe_w1, moe_w3, moe_w2, norm_f):
    cond16 = jnp.concatenate([c_ctx[None, :], c, jnp.zeros((16 - 1 - DEC_BATCH, D), F32)], axis=0)
    mods = _ada_params(cond16, w_mod, b_mod)
    mods = jnp.stack([jnp.broadcast_to(mods[:, :, 0:1], (DEPTH, 6, SUB, D)), mods[:, :, 1:1 + SUB]], axis=2)

    cos_t, sin_t = _rope_tables()
    bdq = _head_mean_matrix(D_ATT)
    bdk = _head_mean_matrix(D_KV)
    cache_k4 = cache_k.reshape(DEC_BATCH, DEPTH // 2, PAST_LEN, D_KV)
    cache_v4 = cache_v.reshape(DEC_BATCH, DEPTH // 2, PAST_LEN, D_KV)
    ff_w1b, ff_w3b, ff_w2b = ff_w1.astype(BF16), ff_w3.astype(BF16), ff_w2.astype(BF16)

    x = pending = moe_wb = None
    new_states, new_k, new_v = [], [], []
    for layer in range(DEPTH):
        li = layer // 2
        shift1, scale1, gate1, shift2, scale2, gate2 = [mods[layer, j] for j in range(6)]
        g1 = norm1[layer].reshape(1, D)
        g2 = norm2[layer].reshape(1, D)
        if layer % 2 == 0:
            if layer == 0:
                z, x = _first_in(x_prompt, x_sample, g1, shift1, scale1, ev_w_in[li].astype(BF16))
            else:
                z, x = _even_in_moe(pending, g1, shift1, scale1, ev_w_in[li].astype(BF16))
            wg, bg = _block_diag_gates(a_w_r[li], a_w_i[li], a_b_r[li], a_b_i[li])
            lam = jnp.concatenate([a_lam[li, 0].reshape(4, 1, LANES), a_lam[li, 1].reshape(4, 1, LANES)], axis=2)
            cb = a_conv_b[li].reshape(1, D_A)
            h0_p = jnp.zeros((P_GROUPS, 2, SUB, D_A), F32)
            h0_s = state_rglru[:, li].transpose(1, 0, 2)[None]
            ya_p, h_last = _rglru_call(z, a_conv_w[li], cb, wg, bg, lam, h0_p,
                                       rows=R_PROMPT, groups=P_GROUPS, row_block0=0)
            ya_s, _ = _rglru_call(z, a_conv_w[li], cb, wg, bg, lam, h0_s,
                                  rows=R_SAMPLE, groups=1, row_block0=1)
            wp = b_w_pool[li].astype(BF16)
            sp = b_scale[li].reshape(1, D_B)
            yb_p = _pool_call(z, wp, sp, rows=R_PROMPT, groups=P_GROUPS, row_block0=0)
            yb_s = _pool_call(z, wp, sp, rows=R_SAMPLE, groups=1, row_block0=1)
            w_out = ev_w_out[li].astype(BF16)
            x, moe_wb = _even_tail(x, (ya_p, ya_s), (yb_p, yb_s), w_out[:D_A], w_out[D_A:], gate1,
                                   g2, shift2, scale2, gate2, ff_w1b, ff_w3b, ff_w2b, li, (moe_w1, moe_w3, moe_w2))
            new_states.append(h_last.transpose(0, 2, 1, 3).reshape(BATCH, 2, D_A))
        else:
            qg = jnp.tile(q_norm[li], N_Q_HEADS).reshape(1, D_ATT)
            kg = jnp.tile(k_norm[li], N_KV_HEADS).reshape(1, D_KV)
            glu, *qs, k, v = _odd_in(x, g1, shift1, scale1, od_w_in[li].astype(BF16), bdq, bdk, qg, kg,
                                     cos_t, sin_t)
            att = _attention(qs, k, v, cache_k4[:, li], cache_v4[:, li])
            cw = c_conv_w[li]
            cb = c_conv_b[li].reshape(1, D_C)
            hc = (_conv_call(glu, cw, cb, rows=R_PROMPT, groups=P_GROUPS, row_block0=0),
                  _conv_call(glu, cw, cb, rows=R_SAMPLE, groups=1, row_block0=1))
            w_out = od_w_out[li].astype(BF16)
            wr = jnp.zeros((D, LANES), F32).at[:, :N_EXPERTS].set(moe_w_router[li])
            br = jnp.zeros((1, LANES), F32).at[0, :N_EXPERTS].set(moe_b_router[li])
            wr_hi = wr.astype(BF16)
            wr_lo = (wr - wr_hi.astype(F32)).astype(BF16)
            x1, xloc, route, counts = _odd_out(x, hc, att, c_ln_g[li].reshape(1, D_C), c_ln_b[li].reshape(1, D_C),
                                     w_out[:D_C], w_out[D_C:], gate1, g2, shift2, scale2, wr_hi, wr_lo, br)
            y2 = _moe_experts(_route_plan(counts), xloc, *moe_wb)
            pending = (x1, y2, route, gate2)
            new_k.append(_prompt_to_batch_major(k).reshape(BATCH, SEQ, N_KV_HEADS, HEAD_DIM))
            new_v.append(_prompt_to_batch_major(v).reshape(BATCH, SEQ, N_KV_HEADS, HEAD_DIM))

    gf = norm_f.reshape(1, D)
    y_prompt = _final_norm(pending, gf, batch=BATCH, steps=SEQ, stream=0)
    y_sample = _final_norm(pending, gf, batch=DEC_BATCH, steps=DEC_SEQ, stream=1)
    return (y_prompt, y_sample, jnp.stack(new_states, axis=1), jnp.stack(new_k, axis=1), jnp.stack(new_v, axis=1))
```

```python
import functools

import jax
import jax.numpy as jnp
from jax import lax
from jax.experimental import pallas as pl
from jax.experimental.pallas import tpu as pltpu

F32 = jnp.float32
BF16 = jnp.bfloat16

D = 1024
BATCH = 32
SEQ = 256
DEPTH = 4
DEC_BATCH = 8
DEC_SEQ = 1024
PAST_LEN = 256
GRID_W = 64
EPS = 1e-6
D_A = 512
A_BLOCKS = 8
A_BS = 64
A_CONV = 4
A_C = 8.0
D_B = 512
POOL_WINDOWS = (2, 4, 8, 16)
B_GS = 128
D_C = 512
C_CONV = 31
HEAD_DIM = 64
N_Q_HEADS = 8
N_KV_HEADS = 2
D_ATT = 512
D_KV = 128
ROPE_THETA = 10000.0
D_FF = 2816
N_EXPERTS = 8
D_FF_E = 1408
D_IN_EVEN = 1536
D_IN_ODD = 1792

SUB = 8
LANES = 128
N_PROMPT = BATCH * SEQ
N_SAMPLE = DEC_BATCH * DEC_SEQ
N_TOK = N_PROMPT + N_SAMPLE
P_GROUPS = BATCH // SUB
R_PROMPT = SEQ * SUB
R_SAMPLE = DEC_SEQ * SUB
TM = 512
TILES_PER_STREAM = N_PROMPT // TM
MOE_TM = 512
MOE_TILES = 2 * N_TOK // MOE_TM + N_EXPERTS
N_CHUNKS = N_TOK // TM
SLAB = D // LANES
NEG_BIG = -3.0e38
VMEM_LIMIT = 56 * 1024 * 1024


def _params(n_axes, vmem=VMEM_LIMIT):
    return pltpu.CompilerParams(dimension_semantics=("arbitrary",) * n_axes, vmem_limit_bytes=vmem)


def _rms(x, g):
    ms = jnp.mean(x * x, axis=-1, keepdims=True)
    return x * lax.rsqrt(ms + EPS) * g


def _modulate(xn, shift, scale):
    tm = xn.shape[0]
    h = xn.reshape(tm // SUB, SUB, D) * (1.0 + scale)[None] + shift[None]
    return h.reshape(tm, D)


def _gated_add(x, gate, y):
    tm = x.shape[0]
    return x + (y.reshape(tm // SUB, SUB, D) * gate[None]).reshape(tm, D)


def _mod_spec():
    return pl.BlockSpec((None, SUB, D), lambda i, *_: (i // TILES_PER_STREAM, 0, 0))


def _row_spec(width, tm=TM):
    return pl.BlockSpec((tm, width), lambda i, *_: (i, 0))


def _full_spec(shape):
    nd = len(shape)
    return pl.BlockSpec(shape, lambda i, *_: (0,) * nd)


def _ada_kernel(c_ref, w_ref, b_ref, o_ref):
    c = c_ref[...]
    s = (c * jax.nn.sigmoid(c)).astype(BF16)
    o_ref[...] = jnp.dot(s, w_ref[...].astype(BF16), preferred_element_type=F32) + b_ref[...]


def _ada_params(cond16, w_mod, b_mod):
    return pl.pallas_call(
        _ada_kernel,
        grid=(DEPTH, 6),
        in_specs=[
            pl.BlockSpec((16, D), lambda l, j: (0, 0)),
            pl.BlockSpec((None, D, D), lambda l, j: (l, 0, j)),
            pl.BlockSpec((None, None, 1, D), lambda l, j: (l, j, 0, 0)),
        ],
        out_specs=pl.BlockSpec((None, None, 16, D), lambda l, j: (l, j, 0, 0)),
        out_shape=jax.ShapeDtypeStruct((DEPTH, 6, 16, D), F32),
        compiler_params=_params(2),
        name="ada_params",
    )(cond16, w_mod, b_mod.reshape(DEPTH, 6, 1, D))


T_TILE = TM // SUB


def _first_in_kernel(xp_ref, xs_ref, g_ref, sh_ref, sc_ref, w_ref, z_ref, x_ref, cols):
    is_sample = pl.program_id(0) >= TILES_PER_STREAM
    for b in range(SUB):
        xb = jnp.where(is_sample, xs_ref[b], xp_ref[b])
        for c in range(SLAB):
            cols[c, pl.ds(b, T_TILE, stride=SUB), :] = xb[:, c * LANES:(c + 1) * LANES]
    x = jnp.concatenate([cols[c] for c in range(SLAB)], axis=1)
    x_ref[...] = x
    h = _modulate(_rms(x, g_ref[...]), sh_ref[...], sc_ref[...])
    z_ref[...] = jnp.dot(h.astype(BF16), w_ref[...], preferred_element_type=F32)


def _first_in(x_prompt, x_sample, g, shift, scale, w_bf16):
    t_tiles = SEQ // T_TILE
    last_p = TILES_PER_STREAM - 1
    return pl.pallas_call(
        _first_in_kernel,
        grid=(N_TOK // TM,),
        in_specs=[
            pl.BlockSpec((SUB, T_TILE, D), lambda i: (jnp.minimum(i, last_p) // t_tiles,
                                                      jnp.minimum(i, last_p) % t_tiles, 0)),
            pl.BlockSpec((SUB, T_TILE, D), lambda i: (0, jnp.maximum(i - TILES_PER_STREAM, 0), 0)),
            _full_spec((1, D)), _mod_spec(), _mod_spec(), _full_spec((D, D_IN_EVEN))],
        out_specs=[_row_spec(D_IN_EVEN), _row_spec(D)],
        out_shape=[jax.ShapeDtypeStruct((N_TOK, D_IN_EVEN), F32), jax.ShapeDtypeStruct((N_TOK, D), F32)],
        scratch_shapes=[pltpu.VMEM((SLAB, TM, LANES), F32)],
        compiler_params=_params(1),
        name="first_in",
    )(x_prompt, x_sample, g, shift, scale, w_bf16)


RG_CHUNK = 256
RG_PAD = 16


def _rglru_kernel(u_ref, ga_ref, cw_ref, cb_ref, wg_ref, bg_ref, lam_ref, h0_ref,
                  y_ref, hl_ref, xpad, a0, b0, a1, b1, *, rows):
    steps = rows // SUB
    n_chunks = rows // RG_CHUNK

    xpad[0:RG_PAD, :] = jnp.zeros((RG_PAD, LANES), F32)
    xpad[RG_PAD + rows:RG_PAD + rows + RG_PAD, :] = jnp.zeros((RG_PAD, LANES), F32)

    def copy_chunk(c, carry):
        r = pl.multiple_of(c * RG_CHUNK, RG_CHUNK)
        xpad[pl.ds(RG_PAD + r, RG_CHUNK), :] = u_ref[pl.ds(r, RG_CHUNK), :]
        return carry

    lax.fori_loop(0, n_chunks, copy_chunk, 0)

    lam = lam_ref[...]
    softplus_neg = jnp.maximum(-lam, 0.0) + jnp.log1p(jnp.exp(-jnp.abs(lam)))
    decay = A_C * softplus_neg
    decay_log2 = decay * (-1.4426950408889634)
    a_refs = (a0, a1)
    b_refs = (b0, b1)

    def gate_chunk(c, carry):
        r = pl.multiple_of(c * RG_CHUNK, RG_CHUNK)
        xc = jnp.zeros((RG_CHUNK, LANES), F32) + cb_ref[...]
        for k in range(A_CONV):
            xc = xc + cw_ref[k:k + 1, :] * xpad[pl.ds(r + SUB * k, RG_CHUNK), :]
        pre = jnp.dot(xc.astype(BF16), wg_ref[...], preferred_element_type=F32) + bg_ref[...]
        for d in range(2):
            rg = 0.5 * jnp.tanh(0.5 * pre[:, d * 256:d * 256 + LANES]) + 0.5
            ig = 0.5 * jnp.tanh(0.5 * pre[:, d * 256 + LANES:(d + 1) * 256]) + 0.5
            a = jnp.exp2(rg * decay_log2[:, d * LANES:(d + 1) * LANES])
            one_minus_a2 = jnp.tanh(rg * decay[:, d * LANES:(d + 1) * LANES]) * (a * a + 1.0)
            root = one_minus_a2 * lax.rsqrt(jnp.maximum(one_minus_a2, 1e-30))
            a_refs[d][pl.ds(r, RG_CHUNK), :] = a
            b_refs[d][pl.ds(r, RG_CHUNK), :] = root * (ig * xc)
        return carry

    lax.fori_loop(0, n_chunks, gate_chunk, 0)

    def step(t, carry):
        hf, hb = carry
        rf = pl.multiple_of(t * SUB, SUB)
        rb = pl.multiple_of((steps - 1 - t) * SUB, SUB)
        hf = a0[pl.ds(rf, SUB), :] * hf + b0[pl.ds(rf, SUB), :]
        b0[pl.ds(rf, SUB), :] = hf
        hb = a1[pl.ds(rb, SUB), :] * hb + b1[pl.ds(rb, SUB), :]
        b1[pl.ds(rb, SUB), :] = hb
        return hf, hb

    hf, hb = lax.fori_loop(0, steps, step, (h0_ref[0], h0_ref[1]), unroll=8)
    hl_ref[0] = hf
    hl_ref[1] = hb

    def out_chunk(c, carry):
        r = pl.multiple_of(c * RG_CHUNK, RG_CHUNK)
        y = (b0[pl.ds(r, RG_CHUNK), :] + b1[pl.ds(r, RG_CHUNK), :]) * jax.nn.gelu(ga_ref[pl.ds(r, RG_CHUNK), :])
        y_ref[pl.ds(r, RG_CHUNK), :] = y.astype(BF16)
        return carry

    lax.fori_loop(0, n_chunks, out_chunk, 0)


def _rglru_call(z, cw, cb, wg, bg, lam, h0, *, rows, groups, row_block0):
    n_cb = D_A // LANES
    in_specs = [
        pl.BlockSpec((rows, LANES), lambda g, j: (row_block0 + g, j)),
        pl.BlockSpec((rows, LANES), lambda g, j: (row_block0 + g, n_cb + j)),
        pl.BlockSpec((A_CONV, LANES), lambda g, j: (0, j)),
        pl.BlockSpec((1, LANES), lambda g, j: (0, j)),
        pl.BlockSpec((None, LANES, 4 * LANES), lambda g, j: (j, 0, 0)),
        pl.BlockSpec((None, 1, 4 * LANES), lambda g, j: (j, 0, 0)),
        pl.BlockSpec((None, 1, 2 * LANES), lambda g, j: (j, 0, 0)),
        pl.BlockSpec((None, 2, SUB, LANES), lambda g, j: (g, 0, 0, j)),
    ]
    return pl.pallas_call(
        functools.partial(_rglru_kernel, rows=rows),
        grid=(groups, n_cb),
        in_specs=in_specs,
        out_specs=[
            pl.BlockSpec((rows, LANES), lambda g, j: (g, j)),
            pl.BlockSpec((None, 2, SUB, LANES), lambda g, j: (g, 0, 0, j)),
        ],
        out_shape=[
            jax.ShapeDtypeStruct((groups * rows, D_A), BF16),
            jax.ShapeDtypeStruct((groups, 2, SUB, D_A), F32),
        ],
        scratch_shapes=[pltpu.VMEM((rows + 2 * RG_PAD, LANES), F32)] + [pltpu.VMEM((rows, LANES), F32)] * 4,
        compiler_params=_params(2),
        name=f"rglru_{rows}",
    )(z, z, cw, cb, wg, bg, lam, h0)


POOL_CHUNK = 256
POOL_PAD = 64


def _pool_kernel(u_ref, w_ref, s_ref, y_ref, xpad, *, rows):
    steps = rows // SUB
    n_chunks = rows // POOL_CHUNK
    j = pl.program_id(1)

    xpad[0:POOL_PAD, :] = jnp.zeros((POOL_PAD, LANES), F32)
    xpad[POOL_PAD + rows:POOL_PAD + rows + POOL_PAD, :] = jnp.zeros((POOL_PAD, LANES), F32)

    def copy_chunk(c, carry):
        r = pl.multiple_of(c * POOL_CHUNK, POOL_CHUNK)
        xpad[pl.ds(POOL_PAD + r, POOL_CHUNK), :] = u_ref[pl.ds(r, POOL_CHUNK), :]
        return carry

    lax.fori_loop(0, n_chunks, copy_chunk, 0)

    for gi, win in enumerate(POOL_WINDOWS):
        half = win // 2

        @pl.when(j == gi)
        def _(half=half, win=win):
            def chunk(c, carry):
                r = pl.multiple_of(c * POOL_CHUNK, POOL_CHUNK)
                acc = xpad[pl.ds(POOL_PAD + r - SUB * half, POOL_CHUNK), :]
                for s in range(1, win):
                    acc = acc + xpad[pl.ds(POOL_PAD + r + SUB * (s - half), POOL_CHUNK), :]
                row = r + lax.broadcasted_iota(jnp.int32, (POOL_CHUNK, LANES), 0)
                t = lax.shift_right_logical(row, SUB.bit_length() - 1)
                cnt = jnp.minimum(t + half, steps) - jnp.maximum(t - half, 0)
                dlt = acc / cnt.astype(F32) - xpad[pl.ds(POOL_PAD + r, POOL_CHUNK), :]
                y = jnp.dot(dlt.astype(BF16), w_ref[...], preferred_element_type=F32) * s_ref[...]
                y_ref[pl.ds(r, POOL_CHUNK), :] = y.astype(BF16)
                return carry

            lax.fori_loop(0, n_chunks, chunk, 0)


def _pool_call(z, w_pool_bf16, s_pool, *, rows, groups, row_block0):
    n_cb = D_B // LANES
    col0 = 2 * D_A // LANES
    in_specs = [
        pl.BlockSpec((rows, LANES), lambda g, j: (row_block0 + g, col0 + j)),
        pl.BlockSpec((None, B_GS, B_GS), lambda g, j: (j, 0, 0)),
        pl.BlockSpec((1, LANES), lambda g, j: (0, j)),
    ]
    return pl.pallas_call(
        functools.partial(_pool_kernel, rows=rows),
        grid=(groups, n_cb),
        in_specs=in_specs,
        out_specs=pl.BlockSpec((rows, LANES), lambda g, j: (g, j)),
        out_shape=jax.ShapeDtypeStruct((groups * rows, D_B), BF16),
        scratch_shapes=[pltpu.VMEM((rows + 2 * POOL_PAD, LANES), F32)],
        compiler_params=_params(2),
        name=f"pool_{rows}",
    )(z, w_pool_bf16, s_pool)


def _stream_specs(width):
    prompt = pl.BlockSpec((TM, width), lambda i, *_: (jnp.minimum(i, TILES_PER_STREAM - 1), 0))
    sample = pl.BlockSpec((TM, width), lambda i, *_: (jnp.maximum(i - TILES_PER_STREAM, 0), 0))
    return [prompt, sample]


def _pick_stream(prompt_ref, sample_ref):
    return jnp.where(pl.program_id(0) >= TILES_PER_STREAM, sample_ref[...], prompt_ref[...])


def _even_tail_kernel(x_ref, yap_ref, yas_ref, ybp_ref, ybs_ref, wa_ref, wb_ref, gate1_ref,
                      g2_ref, sh_ref, sc_ref, gate2_ref, w1_ref, w3_ref, w2_ref, *rest):
    e1_ref, e3_ref, e2_ref, o_ref, e13_out, e2_out = rest
    e13_out[:, :D_FF_E] = e1_ref[...].astype(BF16)
    e13_out[:, D_FF_E:] = e3_ref[...].astype(BF16)
    e2_out[...] = e2_ref[...].astype(BF16)
    y = jnp.dot(_pick_stream(yap_ref, yas_ref), wa_ref[...], preferred_element_type=F32)
    y = y + jnp.dot(_pick_stream(ybp_ref, ybs_ref), wb_ref[...], preferred_element_type=F32)
    x1 = _gated_add(x_ref[...], gate1_ref[...], y)
    h = _modulate(_rms(x1, g2_ref[...]), sh_ref[...], sc_ref[...]).astype(BF16)
    a = jnp.dot(h, w1_ref[...], preferred_element_type=F32)
    b = jnp.dot(h, w3_ref[...], preferred_element_type=F32)
    u = (a * jax.nn.sigmoid(a) * b).astype(BF16)
    f = jnp.dot(u, w2_ref[...], preferred_element_type=F32)
    o_ref[...] = _gated_add(x1, gate2_ref[...], f)


def _even_tail(x, ya, yb, wa, wb, gate1, g2, shift2, scale2, gate2, w1, w3, w2, li, moe_w):
    half = D // 2
    steps = N_TOK // TM
    resident = pl.Buffered(1)
    cast_in_specs, cast_args = [], []
    for w in moe_w:
        _, n_e, rows, cols = w.shape
        cast_args.append(w.reshape(w.shape[0], n_e * rows, cols))
        cast_in_specs.append(pl.BlockSpec((None, n_e * rows // steps, cols), lambda i: (li, i, 0)))
    r13, r2 = N_EXPERTS * D // steps, N_EXPERTS * D_FF_E // steps
    cast_out_specs = [pl.BlockSpec((r13, 2 * D_FF_E), lambda i: (i, 0)), pl.BlockSpec((r2, D), lambda i: (i, 0))]
    cast_shapes = [jax.ShapeDtypeStruct((N_EXPERTS * D, 2 * D_FF_E), BF16),
                   jax.ShapeDtypeStruct((N_EXPERTS * D_FF_E, D), BF16)]
    outs = pl.pallas_call(
        _even_tail_kernel,
        grid=(steps,),
        in_specs=[_row_spec(D)] + _stream_specs(half) + _stream_specs(half) +
                 [pl.BlockSpec((half, D), lambda i: (0, 0), pipeline_mode=resident),
                  pl.BlockSpec((half, D), lambda i: (0, 0), pipeline_mode=resident),
                  _mod_spec(), _full_spec((1, D)), _mod_spec(), _mod_spec(), _mod_spec(),
                  pl.BlockSpec((None, D, D_FF), lambda i: (li, 0, 0), pipeline_mode=resident),
                  pl.BlockSpec((None, D, D_FF), lambda i: (li, 0, 0), pipeline_mode=resident),
                  pl.BlockSpec((None, D_FF, D), lambda i: (li, 0, 0), pipeline_mode=resident)] + cast_in_specs,
        out_specs=[_row_spec(D)] + cast_out_specs,
        out_shape=[jax.ShapeDtypeStruct((N_TOK, D), F32)] + cast_shapes,
        compiler_params=_params(1),
        name="even_tail",
    )(x, *ya, *yb, wa, wb, gate1, g2, shift2, scale2, gate2, w1, w3, w2, *cast_args)
    return outs[0], (outs[1].reshape(N_EXPERTS, D, 2 * D_FF_E), outs[2].reshape(N_EXPERTS, D_FF_E, D))


def _head_rms(x, ones_bd, g):
    sq = x * x
    hi = sq.astype(BF16)
    lo = (sq - hi.astype(F32)).astype(BF16)
    ms = jnp.dot(hi, ones_bd, preferred_element_type=F32) + jnp.dot(lo, ones_bd, preferred_element_type=F32)
    return x * lax.rsqrt(ms + EPS) * g


def _rope(x, cos, sin_signed):
    w = x.shape[-1]
    lane = lax.broadcasted_iota(jnp.int32, x.shape, 1)
    first = (lane % 32) < 16
    partner = jnp.where(first, pltpu.roll(x, w - 16, 1), pltpu.roll(x, 16, 1))
    return x * cos + partner * sin_signed


Q_BLOCKS = D_ATT // LANES


def _odd_in_kernel(x_ref, g_ref, sh_ref, sc_ref, w_ref, bdq_ref, bdk_ref, qg_ref, kg_ref, cos_ref, sin_ref,
                   glu_ref, q0_ref, q1_ref, q2_ref, q3_ref, k_ref, v_ref):
    i = pl.program_id(0)
    h = _modulate(_rms(x_ref[...], g_ref[...]), sh_ref[...], sc_ref[...])
    z = jnp.dot(h.astype(BF16), w_ref[...], preferred_element_type=F32)
    glu_ref[...] = z[:, :D_C] * jax.nn.sigmoid(z[:, D_C:2 * D_C])
    o1 = 2 * D_C
    o2 = o1 + D_ATT
    o3 = o2 + D_KV
    q = _head_rms(z[:, o1:o2], bdq_ref[...], qg_ref[...])
    k = _head_rms(z[:, o2:o3], bdk_ref[...], kg_ref[...])
    cos = cos_ref[...]
    sin = sin_ref[...]
    is_sample = i >= TILES_PER_STREAM
    q_r = _rope(q, jnp.concatenate([cos] * 4, axis=1), jnp.concatenate([sin] * 4, axis=1))
    k_r = _rope(k, cos, sin)
    q = jnp.where(is_sample, q_r, q)
    for c, q_ref in enumerate((q0_ref, q1_ref, q2_ref, q3_ref)):
        q_ref[...] = q[:, c * LANES:(c + 1) * LANES]
    k_ref[...] = jnp.where(is_sample, k_r, k)
    v_ref[...] = z[:, o3:]


def _odd_in(x, g, shift, scale, w_bf16, bdq, bdk, qg, kg, cos_t, sin_t):
    rope_spec = pl.BlockSpec((TM, D_KV), lambda i: (jnp.maximum(i - TILES_PER_STREAM, 0), 0))
    return pl.pallas_call(
        _odd_in_kernel,
        grid=(N_TOK // TM,),
        in_specs=[_row_spec(D), _full_spec((1, D)), _mod_spec(), _mod_spec(), _full_spec((D, D_IN_ODD)),
                  _full_spec((D_ATT, D_ATT)), _full_spec((D_KV, D_KV)), _full_spec((1, D_ATT)),
                  _full_spec((1, D_KV)), rope_spec, rope_spec],
        out_specs=[_row_spec(D_C)] + [_row_spec(LANES)] * (Q_BLOCKS + 2),
        out_shape=[jax.ShapeDtypeStruct((N_TOK, D_C), F32)] +
                  [jax.ShapeDtypeStruct((N_TOK, LANES), F32)] * (Q_BLOCKS + 2),
        compiler_params=_params(1),
        name="odd_in",
    )(x, g, shift, scale, w_bf16, bdq, bdk, qg, kg, cos_t, sin_t)


def _attend(q_refs, o_refs, q_rows, k_all, v_all):
    scale = HEAD_DIM ** -0.5 * 1.4426950408889634
    heads_per_block = LANES // HEAD_DIM
    group = N_Q_HEADS // N_KV_HEADS
    kv = []
    for h in range(N_KV_HEADS):
        kv.append((k_all[:, h * HEAD_DIM:(h + 1) * HEAD_DIM].astype(BF16),
                   v_all[:, h * HEAD_DIM:(h + 1) * HEAD_DIM].astype(BF16)))
    for c in range(Q_BLOCKS):
        qc = (q_refs[c][q_rows, :] * scale).astype(BF16)
        outs = []
        for sub in range(heads_per_block):
            kh, vh = kv[(c * heads_per_block + sub) // group]
            qh = qc[:, sub * HEAD_DIM:(sub + 1) * HEAD_DIM]
            s = lax.dot_general(qh, kh, (((1,), (1,)), ((), ())), preferred_element_type=F32)
            m = jnp.max(s, axis=-1, keepdims=True)
            p = jnp.exp2(s - m)
            l = jnp.sum(p, axis=-1, keepdims=True)
            outs.append(jnp.dot(p.astype(BF16), vh, preferred_element_type=F32) / l)
        o_refs[c][q_rows, :] = jnp.concatenate(outs, axis=1)


def _attn_prompt_kernel(*refs):
    q_refs, (k_ref, v_ref), o_refs = refs[:Q_BLOCKS], refs[Q_BLOCKS:Q_BLOCKS + 2], refs[Q_BLOCKS + 2:]
    rows = pl.ds(pl.program_id(1), SEQ, stride=SUB)
    _attend(q_refs, o_refs, rows, k_ref[rows, :], v_ref[rows, :])


def _attn_sample_kernel(*refs):
    q_refs, (k_ref, v_ref, ck_ref, cv_ref), o_refs = refs[:Q_BLOCKS], refs[Q_BLOCKS:Q_BLOCKS + 4], refs[Q_BLOCKS + 4:]
    b = pl.program_id(1)
    kv_rows = pl.ds(b, DEC_SEQ, stride=SUB)
    k_all = jnp.concatenate([ck_ref[...], k_ref[kv_rows, :]], axis=0)
    v_all = jnp.concatenate([cv_ref[...], v_ref[kv_rows, :]], axis=0)
    _attend(q_refs, o_refs, pl.ds(b, SEQ, stride=SUB), k_all, v_all)


def _attention(qs, k, v, cache_k_l, cache_v_l):
    chunk = pl.BlockSpec((R_PROMPT, LANES), lambda g, b: (g, 0))
    out_shape = [jax.ShapeDtypeStruct((N_PROMPT, LANES), F32)] * Q_BLOCKS
    att_p = pl.pallas_call(
        _attn_prompt_kernel,
        grid=(P_GROUPS, SUB),
        in_specs=[chunk] * (Q_BLOCKS + 2),
        out_specs=[chunk] * Q_BLOCKS,
        out_shape=out_shape,
        compiler_params=_params(2),
        name="attn_prompt",
    )(*qs, k, v)
    q_chunks = DEC_SEQ // SEQ
    q_chunk = pl.BlockSpec((R_PROMPT, LANES), lambda c, b: (P_GROUPS + c, 0))
    kv_all = pl.BlockSpec((R_SAMPLE, LANES), lambda c, b: (1, 0))
    cache = pl.BlockSpec((None, PAST_LEN, D_KV), lambda c, b: (b, 0, 0))
    att_s = pl.pallas_call(
        _attn_sample_kernel,
        grid=(q_chunks, DEC_BATCH),
        in_specs=[q_chunk] * Q_BLOCKS + [kv_all, kv_all, cache, cache],
        out_specs=[pl.BlockSpec((R_PROMPT, LANES), lambda c, b: (c, 0))] * Q_BLOCKS,
        out_shape=out_shape,
        compiler_params=_params(2),
        name="attn_sample",
    )(*qs, k, v, cache_k_l, cache_v_l)
    return att_p, att_s


CV_CHUNK = 128
CV_PAD = (C_CONV // 2) * SUB


def _conv_kernel(u_ref, w_ref, b_ref, y_ref, xpad, *, rows):
    n_copy = rows // 256
    xpad[0:CV_PAD, :] = jnp.zeros((CV_PAD, LANES), F32)
    xpad[CV_PAD + rows:CV_PAD + rows + CV_PAD, :] = jnp.zeros((CV_PAD, LANES), F32)

    def copy_chunk(c, carry):
        r = pl.multiple_of(c * 256, 256)
        xpad[pl.ds(CV_PAD + r, 256), :] = u_ref[pl.ds(r, 256), :]
        return carry

    lax.fori_loop(0, n_copy, copy_chunk, 0)

    def chunk(c, carry):
        r = pl.multiple_of(c * CV_CHUNK, CV_CHUNK)
        acc = jnp.zeros((CV_CHUNK, LANES), F32) + b_ref[...]
        for k in range(C_CONV):
            acc = acc + w_ref[k:k + 1, :] * xpad[pl.ds(r + SUB * k, CV_CHUNK), :]
        y_ref[pl.ds(r, CV_CHUNK), :] = acc
        return carry

    lax.fori_loop(0, rows // CV_CHUNK, chunk, 0)


def _conv_call(glu, w, b, *, rows, groups, row_block0):
    n_cb = D_C // LANES
    in_specs = [
        pl.BlockSpec((rows, LANES), lambda g, j: (row_block0 + g, j)),
        pl.BlockSpec((C_CONV, LANES), lambda g, j: (0, j)),
        pl.BlockSpec((1, LANES), lambda g, j: (0, j)),
    ]
    return pl.pallas_call(
        functools.partial(_conv_kernel, rows=rows),
        grid=(groups, n_cb),
        in_specs=in_specs,
        out_specs=pl.BlockSpec((rows, LANES), lambda g, j: (g, j)),
        out_shape=jax.ShapeDtypeStruct((groups * rows, D_C), F32),
        scratch_shapes=[pltpu.VMEM((rows + 2 * CV_PAD, LANES), F32)],
        compiler_params=_params(2),
        name=f"conv_{rows}",
    )(glu, w, b)


def _odd_out_kernel(x_ref, hcp_ref, hcs_ref, *refs):
    att_refs, refs = refs[:2 * Q_BLOCKS], refs[2 * Q_BLOCKS:]
    (lng_ref, lnb_ref, wc_ref, wa_ref, gate_ref, g2_ref, sh_ref, sc_ref, wrh_ref, wrl_ref, br_ref, tril_ref,
     x1_ref, xloc_ref, route_ref, cnt_ref) = refs
    att = jnp.concatenate([_pick_stream(att_refs[c], att_refs[Q_BLOCKS + c]) for c in range(Q_BLOCKS)], axis=1)
    hc = _pick_stream(hcp_ref, hcs_ref)
    mu = jnp.mean(hc, axis=-1, keepdims=True)
    xc = hc - mu
    var = jnp.mean(xc * xc, axis=-1, keepdims=True)
    ln = xc * lax.rsqrt(var + EPS) * lng_ref[...] + lnb_ref[...]
    yc = (ln * jax.nn.sigmoid(ln)).astype(BF16)
    y = jnp.dot(yc, wc_ref[...], preferred_element_type=F32)
    y = y + jnp.dot(att.astype(BF16), wa_ref[...], preferred_element_type=F32)
    x1 = _gated_add(x_ref[...], gate_ref[...], y)
    x1_ref[...] = x1
    h2 = _modulate(_rms(x1, g2_ref[...]), sh_ref[...], sc_ref[...])
    h2_hi = h2.astype(BF16)
    h2_lo = (h2 - h2_hi.astype(F32)).astype(BF16)
    logits = (jnp.dot(h2_hi, wrh_ref[...], preferred_element_type=F32)
              + jnp.dot(h2_lo, wrh_ref[...], preferred_element_type=F32)
              + jnp.dot(h2_hi, wrl_ref[...], preferred_element_type=F32)) + br_ref[...]
    lane = lax.broadcasted_iota(jnp.int32, logits.shape, 1).astype(F32)
    lg = jnp.where(lane < N_EXPERTS, logits, NEG_BIG)
    m1 = jnp.max(lg, axis=-1, keepdims=True)
    i1 = jnp.min(jnp.where(lg == m1, lane, float(LANES)), axis=-1, keepdims=True)
    lg2 = jnp.where(lane == i1, NEG_BIG, lg)
    m2 = jnp.max(lg2, axis=-1, keepdims=True)
    i2 = jnp.min(jnp.where(lg2 == m2, lane, float(LANES)), axis=-1, keepdims=True)
    e = jnp.exp(m2 - m1)
    den = 1.0 + e

    e1 = jnp.where(lane == i1, 1.0, 0.0)
    e2 = jnp.where(lane == i2, 1.0, 0.0)
    before1 = jnp.dot(tril_ref[...], e1.astype(BF16), preferred_element_type=F32)
    before2 = jnp.dot(tril_ref[...], e2.astype(BF16), preferred_element_type=F32)
    cnt1 = jnp.sum(e1, axis=0, keepdims=True)
    cnt2 = jnp.sum(e2, axis=0, keepdims=True)
    start = (jnp.sum(jnp.where(i1 < lane, 1.0, 0.0), axis=0, keepdims=True)
             + jnp.sum(jnp.where(i2 < lane, 1.0, 0.0), axis=0, keepdims=True))
    pos0 = jnp.sum(e1 * (start + before1), axis=-1, keepdims=True)
    pos1 = jnp.sum(e2 * (start + cnt1 + before2), axis=-1, keepdims=True)
    route = jnp.where(lane == 0.0, i1, jnp.where(lane == 1.0, i2,
                      jnp.where(lane == 2.0, 1.0 / den, jnp.where(lane == 3.0, e / den,
                                jnp.where(lane == 4.0, pos0, jnp.where(lane == 5.0, pos1, 0.0))))))
    route_ref[...] = route
    cnt_ref[...] = jnp.broadcast_to(cnt1 + cnt2, (SUB, LANES))

    q = lax.broadcasted_iota(jnp.int32, (TM, 2 * TM), 1).astype(F32)
    sel = jnp.where(jnp.logical_or(q == pos0, q == pos1), 1.0, 0.0).astype(BF16)
    xloc = lax.dot_general(sel, h2_hi, (((0,), (0,)), ((), ())), preferred_element_type=F32)
    for j in range(SLAB):
        xloc_ref[pl.ds(j, 2 * TM, stride=SLAB), :] = xloc[:, j * LANES:(j + 1) * LANES]


def _odd_out(x, hc, att, lng, lnb, wc, wa, gate, g2, shift2, scale2, wr_hi, wr_lo, br_pad):
    half = D // 2
    att_specs = [_stream_specs(LANES)[0]] * Q_BLOCKS + [_stream_specs(LANES)[1]] * Q_BLOCKS
    return pl.pallas_call(
        _odd_out_kernel,
        grid=(N_TOK // TM,),
        in_specs=[_row_spec(D)] + _stream_specs(half) + att_specs +
                 [_full_spec((1, half)), _full_spec((1, half)),
                  _full_spec((half, D)), _full_spec((half, D)), _mod_spec(),
                  _full_spec((1, D)), _mod_spec(), _mod_spec(),
                  _full_spec((D, LANES)), _full_spec((D, LANES)), _full_spec((1, LANES)), _full_spec((TM, TM))],
        out_specs=[_row_spec(D), pl.BlockSpec((2 * TM * SLAB, LANES), lambda i: (i, 0)), _row_spec(LANES),
                   pl.BlockSpec((SUB, LANES), lambda i: (i, 0))],
        out_shape=[jax.ShapeDtypeStruct((N_TOK, D), F32), jax.ShapeDtypeStruct((2 * N_TOK * SLAB, LANES), F32),
                   jax.ShapeDtypeStruct((N_TOK, LANES), F32),
                   jax.ShapeDtypeStruct((N_CHUNKS * SUB, LANES), F32)],
        compiler_params=_params(1),
        name="odd_out",
    )(x, *hc, *att[0], *att[1], lng, lnb, wc, wa, gate, g2, shift2, scale2, wr_hi, wr_lo, br_pad,
      jnp.tril(jnp.ones((TM, TM), BF16), -1))


def _moe_kernel(te_ref, nv_ref, nu_ref, c0_ref, c1_ref, src_ref, dst_ref, len_ref,
                xloc_hbm, w13_ref, w2_ref, yloc_hbm, xbuf, obuf, gsem, ssem):
    del te_ref
    i = pl.program_id(0)
    n_used = nu_ref[0]
    slot = i % 2
    other = 1 - slot

    def rows(start, n):
        return pl.ds(pl.multiple_of(start * SLAB, SLAB), n * SLAB)

    def for_pieces(tile, fn):
        first = c0_ref[tile]

        def body(c, carry):
            k = tile * N_CHUNKS + first + c
            n = len_ref[k]

            @pl.when(n > 0)
            def _():
                fn(src_ref[k], dst_ref[k], n)

            return carry

        lax.fori_loop(0, c1_ref[tile] - first, body, 0)

    def start_gather(tile, s):
        for_pieces(tile, lambda src, dst, n: pltpu.make_async_copy(
            xloc_hbm.at[rows(src, n), :], xbuf.at[s, rows(dst, n), :], gsem.at[s]).start())

    def start_scatter(tile, s):
        for_pieces(tile, lambda src, dst, n: pltpu.make_async_copy(
            obuf.at[s, rows(dst, n), :], yloc_hbm.at[rows(src, n), :], ssem.at[s]).start())

    def wait_gather(s, n):
        pltpu.make_async_copy(xloc_hbm.at[rows(0, n), :], xbuf.at[s, rows(0, n), :], gsem.at[s]).wait()

    def wait_scatter(s, n):
        pltpu.make_async_copy(obuf.at[s, rows(0, n), :], yloc_hbm.at[rows(0, n), :], ssem.at[s]).wait()

    @pl.when(i == 0)
    def _():
        xbuf[...] = jnp.zeros(xbuf.shape, F32)
        start_gather(0, 0)

    @pl.when(i < n_used)
    def _():
        wait_gather(slot, nv_ref[i])

        @pl.when(i + 1 < n_used)
        def _():
            start_gather(i + 1, other)

        @pl.when(i >= 2)
        def _():
            wait_scatter(slot, nv_ref[jnp.maximum(i - 2, 0)])

        x = jnp.concatenate([xbuf[slot, pl.ds(j, MOE_TM, stride=SLAB), :] for j in range(SLAB)], axis=1)
        h = x.astype(BF16)
        ab = jnp.dot(h, w13_ref[...], preferred_element_type=F32)
        a = ab[:, :D_FF_E]
        u = (a * jax.nn.sigmoid(a) * ab[:, D_FF_E:]).astype(BF16)
        y = jnp.dot(u, w2_ref[...], preferred_element_type=F32)
        for j in range(SLAB):
            obuf[slot, pl.ds(j, MOE_TM, stride=SLAB), :] = y[:, j * LANES:(j + 1) * LANES]
        start_scatter(i, slot)

        @pl.when(i == n_used - 1)
        def _():
            @pl.when(i >= 1)
            def _():
                wait_scatter(other, nv_ref[jnp.maximum(i - 1, 0)])

            wait_scatter(slot, nv_ref[i])


def _moe_experts(plan, xloc, w13, w2):
    def weight(rows, cols):
        return pl.BlockSpec((None, rows, cols), lambda i, te, *_: (te[i], 0, 0))

    grid_spec = pltpu.PrefetchScalarGridSpec(
        num_scalar_prefetch=len(plan),
        grid=(MOE_TILES,),
        in_specs=[pl.BlockSpec(memory_space=pl.ANY), weight(D, 2 * D_FF_E), weight(D_FF_E, D)],
        out_specs=pl.BlockSpec(memory_space=pl.ANY),
        scratch_shapes=[
            pltpu.VMEM((2, MOE_TM * SLAB, LANES), F32),
            pltpu.VMEM((2, MOE_TM * SLAB, LANES), F32),
            pltpu.SemaphoreType.DMA((2,)),
            pltpu.SemaphoreType.DMA((2,)),
        ],
    )
    return pl.pallas_call(
        _moe_kernel,
        grid_spec=grid_spec,
        out_shape=jax.ShapeDtypeStruct((2 * N_TOK * SLAB, LANES), F32),
        compiler_params=_params(1),
        name="moe_experts",
    )(*plan, xloc, w13, w2)


def _route_plan(counts):
    cnt = counts.reshape(N_CHUNKS, SUB, LANES)[:, 0, :N_EXPERTS].astype(jnp.int32)
    run_local = jnp.cumsum(cnt, axis=1) - cnt
    run_group = jnp.cumsum(cnt, axis=0) - cnt
    total = jnp.sum(cnt, axis=0)
    tiles = (total + MOE_TM - 1) // MOE_TM
    tile_end = jnp.cumsum(tiles)
    tile_ids = jnp.arange(MOE_TILES, dtype=jnp.int32)
    tile_expert = jnp.minimum(jnp.sum((tile_ids[:, None] >= tile_end[None, :]).astype(jnp.int32), axis=1),
                              N_EXPERTS - 1)
    first = (tile_ids - (tile_end - tiles)[tile_expert]) * MOE_TM
    used = tile_ids < tile_end[-1]
    tile_valid = jnp.where(used, jnp.clip(total[tile_expert] - first, 0, MOE_TM), 0).astype(jnp.int32)
    a = run_group[:, tile_expert].T
    n = cnt[:, tile_expert].T
    lo = jnp.maximum(a, first[:, None])
    hi = jnp.minimum(a + n, first[:, None] + MOE_TM)
    length = jnp.where(used[:, None], jnp.maximum(hi - lo, 0), 0)
    chunk_row0 = jnp.arange(N_CHUNKS, dtype=jnp.int32)[None, :] * (2 * TM)
    src = jnp.where(length > 0, chunk_row0 + run_local[:, tile_expert].T + (lo - a), 0)
    dst = jnp.where(length > 0, lo - first[:, None], 0)
    n_used = tile_end[-1:].astype(jnp.int32)
    seen = jnp.cumsum((length > 0).astype(jnp.int32), axis=1)
    c_first = jnp.sum((seen == 0).astype(jnp.int32), axis=1)
    c_stop = N_CHUNKS - jnp.sum(jnp.logical_and(seen == seen[:, -1:], length == 0).astype(jnp.int32), axis=1)
    flat = lambda t: t.astype(jnp.int32).reshape(-1)
    return tile_expert, tile_valid, n_used, c_first, c_stop, flat(src), flat(dst), flat(length)


def _moe_combined(x_ref, y_ref, route_ref, gate_ref):
    y = jnp.concatenate([y_ref[pl.ds(j, 2 * TM, stride=SLAB), :] for j in range(SLAB)], axis=1).astype(BF16)
    q = lax.broadcasted_iota(jnp.int32, (TM, 2 * TM), 1).astype(F32)
    pick = jnp.where(q == route_ref[:, 4:5], route_ref[:, 2:3],
                     jnp.where(q == route_ref[:, 5:6], route_ref[:, 3:4], 0.0)).astype(BF16)
    mix = jnp.dot(pick, y, preferred_element_type=F32)
    return _gated_add(x_ref[...], gate_ref[...], mix)


def _moe_pending_specs(tile0=0, stream=None):
    gate = _mod_spec() if stream is None else pl.BlockSpec((None, SUB, D), lambda i: (stream, 0, 0))
    return [pl.BlockSpec((TM, D), lambda i: (tile0 + i, 0)),
            pl.BlockSpec((TM * 2 * SLAB, LANES), lambda i: (tile0 + i, 0)),
            pl.BlockSpec((TM, LANES), lambda i: (tile0 + i, 0)), gate]


def _even_in_moe_kernel(x1_ref, y_ref, route_ref, gate_ref, g_ref, sh_ref, sc_ref, w_ref, z_ref, x_ref):
    x = _moe_combined(x1_ref, y_ref, route_ref, gate_ref)
    x_ref[...] = x
    h = _modulate(_rms(x, g_ref[...]), sh_ref[...], sc_ref[...])
    z_ref[...] = jnp.dot(h.astype(BF16), w_ref[...], preferred_element_type=F32)


def _even_in_moe(pending, g, shift, scale, w_bf16):
    return pl.pallas_call(
        _even_in_moe_kernel,
        grid=(N_TOK // TM,),
        in_specs=_moe_pending_specs() + [_full_spec((1, D)), _mod_spec(), _mod_spec(), _full_spec((D, D_IN_EVEN))],
        out_specs=[_row_spec(D_IN_EVEN), _row_spec(D)],
        out_shape=[jax.ShapeDtypeStruct((N_TOK, D_IN_EVEN), F32), jax.ShapeDtypeStruct((N_TOK, D), F32)],
        compiler_params=_params(1),
        name="even_in_moe",
    )(*pending, g, shift, scale, w_bf16)


def _final_kernel(x1_ref, y_ref, route_ref, gate_ref, g_ref, o_ref, cols):
    y = _rms(_moe_combined(x1_ref, y_ref, route_ref, gate_ref), g_ref[...])
    for c in range(SLAB):
        cols[c] = y[:, c * LANES:(c + 1) * LANES]
    for b in range(SUB):
        for c in range(SLAB):
            o_ref[b, :, c * LANES:(c + 1) * LANES] = cols[c, pl.ds(b, T_TILE, stride=SUB), :]


def _final_norm(pending, g, *, batch, steps, stream):
    t_tiles = steps // T_TILE
    return pl.pallas_call(
        _final_kernel,
        grid=(batch * steps // TM,),
        in_specs=_moe_pending_specs(stream * TILES_PER_STREAM, stream) + [_full_spec((1, D))],
        out_specs=pl.BlockSpec((SUB, T_TILE, D), lambda i: (i // t_tiles, i % t_tiles, 0)),
        out_shape=jax.ShapeDtypeStruct((batch, steps, D), F32),
        scratch_shapes=[pltpu.VMEM((SLAB, TM, LANES), F32)],
        compiler_params=_params(1),
        name=f"final_norm_{steps}",
    )(*pending, g)


def _block_diag_gates(w_r, w_i, b_r, b_i):
    def bd(w):
        w4 = w.reshape(4, 2, A_BS, A_BS)
        z = jnp.zeros((4, A_BS, A_BS), w.dtype)
        top = jnp.concatenate([w4[:, 0], z], axis=2)
        bot = jnp.concatenate([z, w4[:, 1]], axis=2)
        return jnp.concatenate([top, bot], axis=1)

    wg = jnp.concatenate([bd(w_r[0]), bd(w_i[0]), bd(w_r[1]), bd(w_i[1])], axis=2)
    bg = jnp.concatenate([b_r[0].reshape(4, 1, LANES), b_i[0].reshape(4, 1, LANES),
                          b_r[1].reshape(4, 1, LANES), b_i[1].reshape(4, 1, LANES)], axis=2)
    return wg.astype(BF16), bg


def _head_mean_matrix(width):
    idx = jnp.arange(width) // HEAD_DIM
    return ((idx[:, None] == idx[None, :]).astype(F32) / HEAD_DIM).astype(BF16)


def _rope_tables():
    pos = jnp.arange(DEC_SEQ)
    row = (pos // GRID_W).astype(F32)
    col = (pos % GRID_W).astype(F32)
    n_freq = HEAD_DIM // 4
    inv = ROPE_THETA ** (-jnp.arange(n_freq, dtype=F32) / n_freq)
    ang = jnp.stack([row[:, None] * inv, col[:, None] * inv], axis=1)
    cos = jnp.cos(ang)
    sin = jnp.sin(ang)
    cos_h = jnp.stack([cos, cos], axis=2).reshape(DEC_SEQ, HEAD_DIM)
    sin_h = jnp.stack([-sin, sin], axis=2).reshape(DEC_SEQ, HEAD_DIM)
    cos_t = jnp.tile(cos_h, (1, 2))
    sin_t = jnp.tile(sin_h, (1, 2))
    cos_t = jnp.broadcast_to(cos_t[:, None, :], (DEC_SEQ, SUB, D_KV)).reshape(N_SAMPLE, D_KV)
    sin_t = jnp.broadcast_to(sin_t[:, None, :], (DEC_SEQ, SUB, D_KV)).reshape(N_SAMPLE, D_KV)
    return cos_t, sin_t


def _prompt_to_batch_major(a):
    w = a.shape[-1]
    return a[:N_PROMPT].reshape(P_GROUPS, SEQ, SUB, w).transpose(0, 2, 1, 3).reshape(BATCH, SEQ, w)


def kernel(x_prompt, x_sample, c, state_rglru, cache_k, cache_v, c_ctx, w_mod, b_mod, norm1, norm2, ev_w_in, a_conv_w, a_conv_b, a_w_r, a_b_r, a_w_i, a_b_i, a_lam, b_w_pool, b_scale, ev_w_out, od_w_in, c_conv_w, c_conv_b, c_ln_g, c_ln_b, q_norm, k_norm, od_w_out, ff_w1, ff_w3, ff_w2, moe_w_router, moe_b_router, moe_w1, moe_w3, moe_w2, norm_f):
    cond16 = jnp.concatenate([c_ctx[None, :], c, jnp.zeros((16 - 1 - DEC_BATCH, D), F32)], axis=0)
    mods = _ada_params(cond16, w_mod, b_mod)
    mods = jnp.stack([jnp.broadcast_to(mods[:, :, 0:1], (DEPTH, 6, SUB, D)), mods[:, :, 1:1 + SUB]], axis=2)

    cos_t, sin_t = _rope_tables()
    bdq = _head_mean_matrix(D_ATT)
    bdk = _head_mean_matrix(D_KV)
    cache_k4 = cache_k.reshape(DEC_BATCH, DEPTH // 2, PAST_LEN, D_KV)
    cache_v4 = cache_v.reshape(DEC_BATCH, DEPTH // 2, PAST_LEN, D_KV)
    ff_w1b, ff_w3b, ff_w2b = ff_w1.astype(BF16), ff_w3.astype(BF16), ff_w2.astype(BF16)

    x = pending = moe_wb = None
    new_states, new_k, new_v = [], [], []
    for layer in range(DEPTH):
        li = layer // 2
        shift1, scale1, gate1, shift2, scale2, gate2 = [mods[layer, j] for j in range(6)]
        g1 = norm1[layer].reshape(1, D)
        g2 = norm2[layer].reshape(1, D)
        if layer % 2 == 0:
            if layer == 0:
                z, x = _first_in(x_prompt, x_sample, g1, shift1, scale1, ev_w_in[li].astype(BF16))
            else:
                z, x = _even_in_moe(pending, g1, shift1, scale1, ev_w_in[li].astype(BF16))
            wg, bg = _block_diag_gates(a_w_r[li], a_w_i[li], a_b_r[li], a_b_i[li])
            lam = jnp.concatenate([a_lam[li, 0].reshape(4, 1, LANES), a_lam[li, 1].reshape(4, 1, LANES)], axis=2)
            cb = a_conv_b[li].reshape(1, D_A)
            h0_p = jnp.zeros((P_GROUPS, 2, SUB, D_A), F32)
            h0_s = state_rglru[:, li].transpose(1, 0, 2)[None]
            ya_p, h_last = _rglru_call(z, a_conv_w[li], cb, wg, bg, lam, h0_p,
                                       rows=R_PROMPT, groups=P_GROUPS, row_block0=0)
            ya_s, _ = _rglru_call(z, a_conv_w[li], cb, wg, bg, lam, h0_s,
                                  rows=R_SAMPLE, groups=1, row_block0=1)
            wp = b_w_pool[li].astype(BF16)
            sp = b_scale[li].reshape(1, D_B)
            yb_p = _pool_call(z, wp, sp, rows=R_PROMPT, groups=P_GROUPS, row_block0=0)
            yb_s = _pool_call(z, wp, sp, rows=R_SAMPLE, groups=1, row_block0=1)
            w_out = ev_w_out[li].astype(BF16)
            x, moe_wb = _even_tail(x, (ya_p, ya_s), (yb_p, yb_s), w_out[:D_A], w_out[D_A:], gate1,
                                   g2, shift2, scale2, gate2, ff_w1b, ff_w3b, ff_w2b, li, (moe_w1, moe_w3, moe_w2))
            new_states.append(h_last.transpose(0, 2, 1, 3).reshape(BATCH, 2, D_A))
        else:
            qg = jnp.tile(q_norm[li], N_Q_HEADS).reshape(1, D_ATT)
            kg = jnp.tile(k_norm[li], N_KV_HEADS).reshape(1, D_KV)
            glu, *qs, k, v = _odd_in(x, g1, shift1, scale1, od_w_in[li].astype(BF16), bdq, bdk, qg, kg,
                                     cos_t, sin_t)
            att = _attention(qs, k, v, cache_k4[:, li], cache_v4[:, li])
            cw = c_conv_w[li]
            cb = c_conv_b[li].reshape(1, D_C)
            hc = (_conv_call(glu, cw, cb, rows=R_PROMPT, groups=P_GROUPS, row_block0=0),
                  _conv_call(glu, cw, cb, rows=R_SAMPLE, groups=1, row_block0=1))
            w_out = od_w_out[li].astype(BF16)
            wr = jnp.zeros((D, LANES), F32).at[:, :N_EXPERTS].set(moe_w_router[li])
            br = jnp.zeros((1, LANES), F32).at[0, :N_EXPERTS].set(moe_b_router[li])
            wr_hi = wr.astype(BF16)
            wr_lo = (wr - wr_hi.astype(F32)).astype(BF16)
            x1, xloc, route, counts = _odd_out(x, hc, att, c_ln_g[li].reshape(1, D_C), c_ln_b[li].reshape(1, D_C),
                                     w_out[:D_C], w_out[D_C:], gate1, g2, shift2, scale2, wr_hi, wr_lo, br)
            y2 = _moe_experts(_route_plan(counts), xloc, *moe_wb)
            pending = (x1, y2, route, gate2)
            new_k.append(_prompt_to_batch_major(k).reshape(BATCH, SEQ, N_KV_HEADS, HEAD_DIM))
            new_v.append(_prompt_to_batch_major(v).reshape(BATCH, SEQ, N_KV_HEADS, HEAD_DIM))

    gf = norm_f.reshape(1, D)
    y_prompt = _final_norm(pending, gf, batch=BATCH, steps=SEQ, stream=0)
    y_sample = _final_norm(pending, gf, batch=DEC_BATCH, steps=DEC_SEQ, stream=1)
    return (y_prompt, y_sample, jnp.stack(new_states, axis=1), jnp.stack(new_k, axis=1), jnp.stack(new_v, axis=1))
```

```python
import functools

import jax
import jax.numpy as jnp
from jax import lax
from jax.experimental import pallas as pl
from jax.experimental.pallas import tpu as pltpu

F32 = jnp.float32
BF16 = jnp.bfloat16

D = 1024
BATCH = 32
SEQ = 256
DEPTH = 4
DEC_BATCH = 8
DEC_SEQ = 1024
PAST_LEN = 256
GRID_W = 64
EPS = 1e-6
D_A = 512
A_BLOCKS = 8
A_BS = 64
A_CONV = 4
A_C = 8.0
D_B = 512
POOL_WINDOWS = (2, 4, 8, 16)
B_GS = 128
D_C = 512
C_CONV = 31
HEAD_DIM = 64
N_Q_HEADS = 8
N_KV_HEADS = 2
D_ATT = 512
D_KV = 128
ROPE_THETA = 10000.0
D_FF = 2816
N_EXPERTS = 8
D_FF_E = 1408
D_IN_EVEN = 1536
D_IN_ODD = 1792

SUB = 8
LANES = 128
N_PROMPT = BATCH * SEQ
N_SAMPLE = DEC_BATCH * DEC_SEQ
N_TOK = N_PROMPT + N_SAMPLE
P_GROUPS = BATCH // SUB
R_PROMPT = SEQ * SUB
R_SAMPLE = DEC_SEQ * SUB
TM = 512
TILES_PER_STREAM = N_PROMPT // TM
MOE_TM = 512
MOE_TILES = 2 * N_TOK // MOE_TM + N_EXPERTS
N_CHUNKS = N_TOK // TM
SLAB = D // LANES
NEG_BIG = -3.0e38
VMEM_LIMIT = 56 * 1024 * 1024


def _params(n_axes, vmem=VMEM_LIMIT):
    return pltpu.CompilerParams(dimension_semantics=("arbitrary",) * n_axes, vmem_limit_bytes=vmem)


def _rms(x, g):
    ms = jnp.mean(x * x, axis=-1, keepdims=True)
    return x * lax.rsqrt(ms + EPS) * g


def _modulate(xn, shift, scale):
    tm = xn.shape[0]
    h = xn.reshape(tm // SUB, SUB, D) * (1.0 + scale)[None] + shift[None]
    return h.reshape(tm, D)


def _gated_add(x, gate, y):
    tm = x.shape[0]
    return x + (y.reshape(tm // SUB, SUB, D) * gate[None]).reshape(tm, D)


def _mod_spec():
    return pl.BlockSpec((None, SUB, D), lambda i, *_: (i // TILES_PER_STREAM, 0, 0))


def _row_spec(width, tm=TM):
    return pl.BlockSpec((tm, width), lambda i, *_: (i, 0))


def _full_spec(shape):
    nd = len(shape)
    return pl.BlockSpec(shape, lambda i, *_: (0,) * nd)


def _ada_kernel(c_ref, w_ref, b_ref, o_ref):
    c = c_ref[...]
    s = (c * jax.nn.sigmoid(c)).astype(BF16)
    o_ref[...] = jnp.dot(s, w_ref[...].astype(BF16), preferred_element_type=F32) + b_ref[...]


def _ada_params(cond16, w_mod, b_mod):
    return pl.pallas_call(
        _ada_kernel,
        grid=(DEPTH, 6),
        in_specs=[
            pl.BlockSpec((16, D), lambda l, j: (0, 0)),
            pl.BlockSpec((None, D, D), lambda l, j: (l, 0, j)),
            pl.BlockSpec((None, None, 1, D), lambda l, j: (l, j, 0, 0)),
        ],
        out_specs=pl.BlockSpec((None, None, 16, D), lambda l, j: (l, j, 0, 0)),
        out_shape=jax.ShapeDtypeStruct((DEPTH, 6, 16, D), F32),
        compiler_params=_params(2),
        name="ada_params",
    )(cond16, w_mod, b_mod.reshape(DEPTH, 6, 1, D))


T_TILE = TM // SUB


def _first_in_kernel(xp_ref, xs_ref, g_ref, sh_ref, sc_ref, w_ref, z_ref, x_ref, cols):
    is_sample = pl.program_id(0) >= TILES_PER_STREAM
    for b in range(SUB):
        xb = jnp.where(is_sample, xs_ref[b], xp_ref[b])
        for c in range(SLAB):
            cols[c, pl.ds(b, T_TILE, stride=SUB), :] = xb[:, c * LANES:(c + 1) * LANES]
    x = jnp.concatenate([cols[c] for c in range(SLAB)], axis=1)
    x_ref[...] = x
    h = _modulate(_rms(x, g_ref[...]), sh_ref[...], sc_ref[...])
    z_ref[...] = jnp.dot(h.astype(BF16), w_ref[...], preferred_element_type=F32)


def _first_in(x_prompt, x_sample, g, shift, scale, w_bf16):
    t_tiles = SEQ // T_TILE
    last_p = TILES_PER_STREAM - 1
    return pl.pallas_call(
        _first_in_kernel,
        grid=(N_TOK // TM,),
        in_specs=[
            pl.BlockSpec((SUB, T_TILE, D), lambda i: (jnp.minimum(i, last_p) // t_tiles,
                                                      jnp.minimum(i, last_p) % t_tiles, 0)),
            pl.BlockSpec((SUB, T_TILE, D), lambda i: (0, jnp.maximum(i - TILES_PER_STREAM, 0), 0)),
            _full_spec((1, D)), _mod_spec(), _mod_spec(), _full_spec((D, D_IN_EVEN))],
        out_specs=[_row_spec(D_IN_EVEN), _row_spec(D)],
        out_shape=[jax.ShapeDtypeStruct((N_TOK, D_IN_EVEN), F32), jax.ShapeDtypeStruct((N_TOK, D), F32)],
        scratch_shapes=[pltpu.VMEM((SLAB, TM, LANES), F32)],
        compiler_params=_params(1),
        name="first_in",
    )(x_prompt, x_sample, g, shift, scale, w_bf16)


RG_CHUNK = 256
RG_PAD = 16


def _rglru_kernel(u_ref, ga_ref, cw_ref, cb_ref, wg_ref, bg_ref, lam_ref, h0_ref,
                  y_ref, hl_ref, xpad, a0, b0, a1, b1, *, rows):
    steps = rows // SUB
    n_chunks = rows // RG_CHUNK

    xpad[0:RG_PAD, :] = jnp.zeros((RG_PAD, LANES), F32)
    xpad[RG_PAD + rows:RG_PAD + rows + RG_PAD, :] = jnp.zeros((RG_PAD, LANES), F32)

    def copy_chunk(c, carry):
        r = pl.multiple_of(c * RG_CHUNK, RG_CHUNK)
        xpad[pl.ds(RG_PAD + r, RG_CHUNK), :] = u_ref[pl.ds(r, RG_CHUNK), :]
        return carry

    lax.fori_loop(0, n_chunks, copy_chunk, 0)

    lam = lam_ref[...]
    softplus_neg = jnp.maximum(-lam, 0.0) + jnp.log1p(jnp.exp(-jnp.abs(lam)))
    decay = A_C * softplus_neg
    decay_log2 = decay * (-1.4426950408889634)
    a_refs = (a0, a1)
    b_refs = (b0, b1)

    def gate_chunk(c, carry):
        r = pl.multiple_of(c * RG_CHUNK, RG_CHUNK)
        xc = jnp.zeros((RG_CHUNK, LANES), F32) + cb_ref[...]
        for k in range(A_CONV):
            xc = xc + cw_ref[k:k + 1, :] * xpad[pl.ds(r + SUB * k, RG_CHUNK), :]
        pre = jnp.dot(xc.astype(BF16), wg_ref[...], preferred_element_type=F32) + bg_ref[...]
        for d in range(2):
            rg = 0.5 * jnp.tanh(0.5 * pre[:, d * 256:d * 256 + LANES]) + 0.5
            ig = 0.5 * jnp.tanh(0.5 * pre[:, d * 256 + LANES:(d + 1) * 256]) + 0.5
            a = jnp.exp2(rg * decay_log2[:, d * LANES:(d + 1) * LANES])
            one_minus_a2 = jnp.tanh(rg * decay[:, d * LANES:(d + 1) * LANES]) * (a * a + 1.0)
            root = one_minus_a2 * lax.rsqrt(jnp.maximum(one_minus_a2, 1e-30))
            a_refs[d][pl.ds(r, RG_CHUNK), :] = a
            b_refs[d][pl.ds(r, RG_CHUNK), :] = root * (ig * xc)
        return carry

    lax.fori_loop(0, n_chunks, gate_chunk, 0, unroll=2)

    def step(t, carry):
        hf, hb = carry
        rf = pl.multiple_of(t * SUB, SUB)
        rb = pl.multiple_of((steps - 1 - t) * SUB, SUB)
        hf = a0[pl.ds(rf, SUB), :] * hf + b0[pl.ds(rf, SUB), :]
        b0[pl.ds(rf, SUB), :] = hf
        hb = a1[pl.ds(rb, SUB), :] * hb + b1[pl.ds(rb, SUB), :]
        b1[pl.ds(rb, SUB), :] = hb
        return hf, hb

    hf, hb = lax.fori_loop(0, steps, step, (h0_ref[0], h0_ref[1]), unroll=8)
    hl_ref[0] = hf
    hl_ref[1] = hb

    def out_chunk(c, carry):
        r = pl.multiple_of(c * RG_CHUNK, RG_CHUNK)
        y = (b0[pl.ds(r, RG_CHUNK), :] + b1[pl.ds(r, RG_CHUNK), :]) * jax.nn.gelu(ga_ref[pl.ds(r, RG_CHUNK), :])
        y_ref[pl.ds(r, RG_CHUNK), :] = y.astype(BF16)
        return carry

    lax.fori_loop(0, n_chunks, out_chunk, 0)


def _rglru_call(z, cw, cb, wg, bg, lam, h0, *, rows, groups, row_block0):
    n_cb = D_A // LANES
    in_specs = [
        pl.BlockSpec((rows, LANES), lambda g, j: (row_block0 + g, j)),
        pl.BlockSpec((rows, LANES), lambda g, j: (row_block0 + g, n_cb + j)),
        pl.BlockSpec((A_CONV, LANES), lambda g, j: (0, j)),
        pl.BlockSpec((1, LANES), lambda g, j: (0, j)),
        pl.BlockSpec((None, LANES, 4 * LANES), lambda g, j: (j, 0, 0)),
        pl.BlockSpec((None, 1, 4 * LANES), lambda g, j: (j, 0, 0)),
        pl.BlockSpec((None, 1, 2 * LANES), lambda g, j: (j, 0, 0)),
        pl.BlockSpec((None, 2, SUB, LANES), lambda g, j: (g, 0, 0, j)),
    ]
    return pl.pallas_call(
        functools.partial(_rglru_kernel, rows=rows),
        grid=(groups, n_cb),
        in_specs=in_specs,
        out_specs=[
            pl.BlockSpec((rows, LANES), lambda g, j: (g, j)),
            pl.BlockSpec((None, 2, SUB, LANES), lambda g, j: (g, 0, 0, j)),
        ],
        out_shape=[
            jax.ShapeDtypeStruct((groups * rows, D_A), BF16),
            jax.ShapeDtypeStruct((groups, 2, SUB, D_A), F32),
        ],
        scratch_shapes=[pltpu.VMEM((rows + 2 * RG_PAD, LANES), F32)] + [pltpu.VMEM((rows, LANES), F32)] * 4,
        compiler_params=_params(2),
        name=f"rglru_{rows}",
    )(z, z, cw, cb, wg, bg, lam, h0)


POOL_CHUNK = 256
POOL_PAD = 64


def _pool_kernel(u_ref, w_ref, s_ref, y_ref, xpad, *, rows):
    steps = rows // SUB
    n_chunks = rows // POOL_CHUNK
    j = pl.program_id(1)

    xpad[0:POOL_PAD, :] = jnp.zeros((POOL_PAD, LANES), F32)
    xpad[POOL_PAD + rows:POOL_PAD + rows + POOL_PAD, :] = jnp.zeros((POOL_PAD, LANES), F32)

    def copy_chunk(c, carry):
        r = pl.multiple_of(c * POOL_CHUNK, POOL_CHUNK)
        xpad[pl.ds(POOL_PAD + r, POOL_CHUNK), :] = u_ref[pl.ds(r, POOL_CHUNK), :]
        return carry

    lax.fori_loop(0, n_chunks, copy_chunk, 0)

    for gi, win in enumerate(POOL_WINDOWS):
        half = win // 2

        @pl.when(j == gi)
        def _(half=half, win=win):
            def chunk(c, carry):
                r = pl.multiple_of(c * POOL_CHUNK, POOL_CHUNK)
                acc = xpad[pl.ds(POOL_PAD + r - SUB * half, POOL_CHUNK), :]
                for s in range(1, win):
                    acc = acc + xpad[pl.ds(POOL_PAD + r + SUB * (s - half), POOL_CHUNK), :]
                row = r + lax.broadcasted_iota(jnp.int32, (POOL_CHUNK, LANES), 0)
                t = lax.shift_right_logical(row, SUB.bit_length() - 1)
                cnt = jnp.minimum(t + half, steps) - jnp.maximum(t - half, 0)
                dlt = acc / cnt.astype(F32) - xpad[pl.ds(POOL_PAD + r, POOL_CHUNK), :]
                y = jnp.dot(dlt.astype(BF16), w_ref[...], preferred_element_type=F32) * s_ref[...]
                y_ref[pl.ds(r, POOL_CHUNK), :] = y.astype(BF16)
                return carry

            lax.fori_loop(0, n_chunks, chunk, 0, unroll=4)


def _pool_call(z, w_pool_bf16, s_pool, *, rows, groups, row_block0):
    n_cb = D_B // LANES
    col0 = 2 * D_A // LANES
    in_specs = [
        pl.BlockSpec((rows, LANES), lambda g, j: (row_block0 + g, col0 + j)),
        pl.BlockSpec((None, B_GS, B_GS), lambda g, j: (j, 0, 0)),
        pl.BlockSpec((1, LANES), lambda g, j: (0, j)),
    ]
    return pl.pallas_call(
        functools.partial(_pool_kernel, rows=rows),
        grid=(groups, n_cb),
        in_specs=in_specs,
        out_specs=pl.BlockSpec((rows, LANES), lambda g, j: (g, j)),
        out_shape=jax.ShapeDtypeStruct((groups * rows, D_B), BF16),
        scratch_shapes=[pltpu.VMEM((rows + 2 * POOL_PAD, LANES), F32)],
        compiler_params=_params(2),
        name=f"pool_{rows}",
    )(z, w_pool_bf16, s_pool)


def _stream_specs(width):
    prompt = pl.BlockSpec((TM, width), lambda i, *_: (jnp.minimum(i, TILES_PER_STREAM - 1), 0))
    sample = pl.BlockSpec((TM, width), lambda i, *_: (jnp.maximum(i - TILES_PER_STREAM, 0), 0))
    return [prompt, sample]


def _pick_stream(prompt_ref, sample_ref):
    return jnp.where(pl.program_id(0) >= TILES_PER_STREAM, sample_ref[...], prompt_ref[...])


def _even_tail_kernel(x_ref, yap_ref, yas_ref, ybp_ref, ybs_ref, wa_ref, wb_ref, gate1_ref,
                      g2_ref, sh_ref, sc_ref, gate2_ref, w1_ref, w3_ref, w2_ref, *rest):
    e1_ref, e3_ref, e2_ref, o_ref, e13_out, e2_out = rest
    e13_out[:, :D_FF_E] = e1_ref[...].astype(BF16)
    e13_out[:, D_FF_E:] = e3_ref[...].astype(BF16)
    e2_out[...] = e2_ref[...].astype(BF16)
    y = jnp.dot(_pick_stream(yap_ref, yas_ref), wa_ref[...], preferred_element_type=F32)
    y = y + jnp.dot(_pick_stream(ybp_ref, ybs_ref), wb_ref[...], preferred_element_type=F32)
    x1 = _gated_add(x_ref[...], gate1_ref[...], y)
    h = _modulate(_rms(x1, g2_ref[...]), sh_ref[...], sc_ref[...]).astype(BF16)
    a = jnp.dot(h, w1_ref[...], preferred_element_type=F32)
    b = jnp.dot(h, w3_ref[...], preferred_element_type=F32)
    u = (a * jax.nn.sigmoid(a) * b).astype(BF16)
    f = jnp.dot(u, w2_ref[...], preferred_element_type=F32)
    o_ref[...] = _gated_add(x1, gate2_ref[...], f)


def _even_tail(x, ya, yb, wa, wb, gate1, g2, shift2, scale2, gate2, w1, w3, w2, li, moe_w):
    half = D // 2
    steps = N_TOK // TM
    resident = pl.Buffered(1)
    cast_in_specs, cast_args = [], []
    for w in moe_w:
        _, n_e, rows, cols = w.shape
        cast_args.append(w.reshape(w.shape[0], n_e * rows, cols))
        cast_in_specs.append(pl.BlockSpec((None, n_e * rows // steps, cols), lambda i: (li, i, 0)))
    r13, r2 = N_EXPERTS * D // steps, N_EXPERTS * D_FF_E // steps
    cast_out_specs = [pl.BlockSpec((r13, 2 * D_FF_E), lambda i: (i, 0)), pl.BlockSpec((r2, D), lambda i: (i, 0))]
    cast_shapes = [jax.ShapeDtypeStruct((N_EXPERTS * D, 2 * D_FF_E), BF16),
                   jax.ShapeDtypeStruct((N_EXPERTS * D_FF_E, D), BF16)]
    outs = pl.pallas_call(
        _even_tail_kernel,
        grid=(steps,),
        in_specs=[_row_spec(D)] + _stream_specs(half) + _stream_specs(half) +
                 [pl.BlockSpec((half, D), lambda i: (0, 0), pipeline_mode=resident),
                  pl.BlockSpec((half, D), lambda i: (0, 0), pipeline_mode=resident),
                  _mod_spec(), _full_spec((1, D)), _mod_spec(), _mod_spec(), _mod_spec(),
                  pl.BlockSpec((None, D, D_FF), lambda i: (li, 0, 0), pipeline_mode=resident),
                  pl.BlockSpec((None, D, D_FF), lambda i: (li, 0, 0), pipeline_mode=resident),
                  pl.BlockSpec((None, D_FF, D), lambda i: (li, 0, 0), pipeline_mode=resident)] + cast_in_specs,
        out_specs=[_row_spec(D)] + cast_out_specs,
        out_shape=[jax.ShapeDtypeStruct((N_TOK, D), F32)] + cast_shapes,
        compiler_params=_params(1),
        name="even_tail",
    )(x, *ya, *yb, wa, wb, gate1, g2, shift2, scale2, gate2, w1, w3, w2, *cast_args)
    return outs[0], (outs[1].reshape(N_EXPERTS, D, 2 * D_FF_E), outs[2].reshape(N_EXPERTS, D_FF_E, D))


def _head_rms(x, ones_bd, g):
    sq = x * x
    hi = sq.astype(BF16)
    lo = (sq - hi.astype(F32)).astype(BF16)
    ms = jnp.dot(hi, ones_bd, preferred_element_type=F32) + jnp.dot(lo, ones_bd, preferred_element_type=F32)
    return x * lax.rsqrt(ms + EPS) * g


def _rope(x, cos, sin_signed):
    w = x.shape[-1]
    lane = lax.broadcasted_iota(jnp.int32, x.shape, 1)
    first = (lane % 32) < 16
    partner = jnp.where(first, pltpu.roll(x, w - 16, 1), pltpu.roll(x, 16, 1))
    return x * cos + partner * sin_signed


Q_BLOCKS = D_ATT // LANES


def _odd_in_kernel(x_ref, g_ref, sh_ref, sc_ref, w_ref, bdq_ref, bdk_ref, qg_ref, kg_ref, cos_ref, sin_ref,
                   glu_ref, q0_ref, q1_ref, q2_ref, q3_ref, k_ref, v_ref):
    i = pl.program_id(0)
    h = _modulate(_rms(x_ref[...], g_ref[...]), sh_ref[...], sc_ref[...])
    z = jnp.dot(h.astype(BF16), w_ref[...], preferred_element_type=F32)
    glu_ref[...] = z[:, :D_C] * jax.nn.sigmoid(z[:, D_C:2 * D_C])
    o1 = 2 * D_C
    o2 = o1 + D_ATT
    o3 = o2 + D_KV
    q = _head_rms(z[:, o1:o2], bdq_ref[...], qg_ref[...])
    k = _head_rms(z[:, o2:o3], bdk_ref[...], kg_ref[...])
    cos = cos_ref[...]
    sin = sin_ref[...]
    is_sample = i >= TILES_PER_STREAM
    q_r = _rope(q, jnp.concatenate([cos] * 4, axis=1), jnp.concatenate([sin] * 4, axis=1))
    k_r = _rope(k, cos, sin)
    q = jnp.where(is_sample, q_r, q)
    for c, q_ref in enumerate((q0_ref, q1_ref, q2_ref, q3_ref)):
        q_ref[...] = q[:, c * LANES:(c + 1) * LANES]
    k_ref[...] = jnp.where(is_sample, k_r, k)
    v_ref[...] = z[:, o3:]


def _odd_in(x, g, shift, scale, w_bf16, bdq, bdk, qg, kg, cos_t, sin_t):
    rope_spec = pl.BlockSpec((TM, D_KV), lambda i: (jnp.maximum(i - TILES_PER_STREAM, 0), 0))
    return pl.pallas_call(
        _odd_in_kernel,
        grid=(N_TOK // TM,),
        in_specs=[_row_spec(D), _full_spec((1, D)), _mod_spec(), _mod_spec(), _full_spec((D, D_IN_ODD)),
                  _full_spec((D_ATT, D_ATT)), _full_spec((D_KV, D_KV)), _full_spec((1, D_ATT)),
                  _full_spec((1, D_KV)), rope_spec, rope_spec],
        out_specs=[_row_spec(D_C)] + [_row_spec(LANES)] * (Q_BLOCKS + 2),
        out_shape=[jax.ShapeDtypeStruct((N_TOK, D_C), F32)] +
                  [jax.ShapeDtypeStruct((N_TOK, LANES), F32)] * (Q_BLOCKS + 2),
        compiler_params=_params(1),
        name="odd_in",
    )(x, g, shift, scale, w_bf16, bdq, bdk, qg, kg, cos_t, sin_t)


def _attend(q_refs, o_refs, q_rows, k_all, v_all):
    scale = HEAD_DIM ** -0.5 * 1.4426950408889634
    heads_per_block = LANES // HEAD_DIM
    group = N_Q_HEADS // N_KV_HEADS
    kv = []
    for h in range(N_KV_HEADS):
        kv.append((k_all[:, h * HEAD_DIM:(h + 1) * HEAD_DIM].astype(BF16),
                   v_all[:, h * HEAD_DIM:(h + 1) * HEAD_DIM].astype(BF16)))
    for c in range(Q_BLOCKS):
        qc = (q_refs[c][q_rows, :] * scale).astype(BF16)
        outs = []
        for sub in range(heads_per_block):
            kh, vh = kv[(c * heads_per_block + sub) // group]
            qh = qc[:, sub * HEAD_DIM:(sub + 1) * HEAD_DIM]
            s = lax.dot_general(qh, kh, (((1,), (1,)), ((), ())), preferred_element_type=F32)
            m = jnp.max(s, axis=-1, keepdims=True)
            p = jnp.exp2(s - m)
            l = jnp.sum(p, axis=-1, keepdims=True)
            outs.append(jnp.dot(p.astype(BF16), vh, preferred_element_type=F32) / l)
        o_refs[c][q_rows, :] = jnp.concatenate(outs, axis=1)


def _attn_prompt_kernel(*refs):
    q_refs, (k_ref, v_ref), o_refs = refs[:Q_BLOCKS], refs[Q_BLOCKS:Q_BLOCKS + 2], refs[Q_BLOCKS + 2:]
    rows = pl.ds(pl.program_id(1), SEQ, stride=SUB)
    _attend(q_refs, o_refs, rows, k_ref[rows, :], v_ref[rows, :])


def _attn_sample_kernel(*refs):
    q_refs, (k_ref, v_ref, ck_ref, cv_ref), o_refs = refs[:Q_BLOCKS], refs[Q_BLOCKS:Q_BLOCKS + 4], refs[Q_BLOCKS + 4:]
    b = pl.program_id(1)
    kv_rows = pl.ds(b, DEC_SEQ, stride=SUB)
    k_all = jnp.concatenate([ck_ref[...], k_ref[kv_rows, :]], axis=0)
    v_all = jnp.concatenate([cv_ref[...], v_ref[kv_rows, :]], axis=0)
    _attend(q_refs, o_refs, pl.ds(b, SEQ, stride=SUB), k_all, v_all)


def _attention(qs, k, v, cache_k_l, cache_v_l):
    chunk = pl.BlockSpec((R_PROMPT, LANES), lambda g, b: (g, 0))
    out_shape = [jax.ShapeDtypeStruct((N_PROMPT, LANES), F32)] * Q_BLOCKS
    att_p = pl.pallas_call(
        _attn_prompt_kernel,
        grid=(P_GROUPS, SUB),
        in_specs=[chunk] * (Q_BLOCKS + 2),
        out_specs=[chunk] * Q_BLOCKS,
        out_shape=out_shape,
        compiler_params=_params(2),
        name="attn_prompt",
    )(*qs, k, v)
    q_chunks = DEC_SEQ // SEQ
    q_chunk = pl.BlockSpec((R_PROMPT, LANES), lambda c, b: (P_GROUPS + c, 0))
    kv_all = pl.BlockSpec((R_SAMPLE, LANES), lambda c, b: (1, 0))
    cache = pl.BlockSpec((None, PAST_LEN, D_KV), lambda c, b: (b, 0, 0))
    att_s = pl.pallas_call(
        _attn_sample_kernel,
        grid=(q_chunks, DEC_BATCH),
        in_specs=[q_chunk] * Q_BLOCKS + [kv_all, kv_all, cache, cache],
        out_specs=[pl.BlockSpec((R_PROMPT, LANES), lambda c, b: (c, 0))] * Q_BLOCKS,
        out_shape=out_shape,
        compiler_params=_params(2),
        name="attn_sample",
    )(*qs, k, v, cache_k_l, cache_v_l)
    return att_p, att_s


CV_CHUNK = 128
CV_PAD = (C_CONV // 2) * SUB


def _conv_kernel(u_ref, w_ref, b_ref, y_ref, xpad, *, rows):
    n_copy = rows // 256
    xpad[0:CV_PAD, :] = jnp.zeros((CV_PAD, LANES), F32)
    xpad[CV_PAD + rows:CV_PAD + rows + CV_PAD, :] = jnp.zeros((CV_PAD, LANES), F32)

    def copy_chunk(c, carry):
        r = pl.multiple_of(c * 256, 256)
        xpad[pl.ds(CV_PAD + r, 256), :] = u_ref[pl.ds(r, 256), :]
        return carry

    lax.fori_loop(0, n_copy, copy_chunk, 0)

    def chunk(c, carry):
        r = pl.multiple_of(c * CV_CHUNK, CV_CHUNK)
        acc = jnp.zeros((CV_CHUNK, LANES), F32) + b_ref[...]
        for k in range(C_CONV):
            acc = acc + w_ref[k:k + 1, :] * xpad[pl.ds(r + SUB * k, CV_CHUNK), :]
        y_ref[pl.ds(r, CV_CHUNK), :] = acc
        return carry

    lax.fori_loop(0, rows // CV_CHUNK, chunk, 0)


def _conv_call(glu, w, b, *, rows, groups, row_block0):
    n_cb = D_C // LANES
    in_specs = [
        pl.BlockSpec((rows, LANES), lambda g, j: (row_block0 + g, j)),
        pl.BlockSpec((C_CONV, LANES), lambda g, j: (0, j)),
        pl.BlockSpec((1, LANES), lambda g, j: (0, j)),
    ]
    return pl.pallas_call(
        functools.partial(_conv_kernel, rows=rows),
        grid=(groups, n_cb),
        in_specs=in_specs,
        out_specs=pl.BlockSpec((rows, LANES), lambda g, j: (g, j)),
        out_shape=jax.ShapeDtypeStruct((groups * rows, D_C), F32),
        scratch_shapes=[pltpu.VMEM((rows + 2 * CV_PAD, LANES), F32)],
        compiler_params=_params(2),
        name=f"conv_{rows}",
    )(glu, w, b)


def _odd_out_kernel(x_ref, hcp_ref, hcs_ref, *refs):
    att_refs, refs = refs[:2 * Q_BLOCKS], refs[2 * Q_BLOCKS:]
    (lng_ref, lnb_ref, wc_ref, wa_ref, gate_ref, g2_ref, sh_ref, sc_ref, wrh_ref, wrl_ref, br_ref, tril_ref,
     x1_ref, xloc_ref, route_ref, cnt_ref) = refs
    att = jnp.concatenate([_pick_stream(att_refs[c], att_refs[Q_BLOCKS + c]) for c in range(Q_BLOCKS)], axis=1)
    hc = _pick_stream(hcp_ref, hcs_ref)
    mu = jnp.mean(hc, axis=-1, keepdims=True)
    xc = hc - mu
    var = jnp.mean(xc * xc, axis=-1, keepdims=True)
    ln = xc * lax.rsqrt(var + EPS) * lng_ref[...] + lnb_ref[...]
    yc = (ln * jax.nn.sigmoid(ln)).astype(BF16)
    y = jnp.dot(yc, wc_ref[...], preferred_element_type=F32)
    y = y + jnp.dot(att.astype(BF16), wa_ref[...], preferred_element_type=F32)
    x1 = _gated_add(x_ref[...], gate_ref[...], y)
    x1_ref[...] = x1
    h2 = _modulate(_rms(x1, g2_ref[...]), sh_ref[...], sc_ref[...])
    h2_hi = h2.astype(BF16)
    h2_lo = (h2 - h2_hi.astype(F32)).astype(BF16)
    logits = (jnp.dot(h2_hi, wrh_ref[...], preferred_element_type=F32)
              + jnp.dot(h2_lo, wrh_ref[...], preferred_element_type=F32)
              + jnp.dot(h2_hi, wrl_ref[...], preferred_element_type=F32)) + br_ref[...]
    lane = lax.broadcasted_iota(jnp.int32, logits.shape, 1).astype(F32)
    lg = jnp.where(lane < N_EXPERTS, logits, NEG_BIG)
    m1 = jnp.max(lg, axis=-1, keepdims=True)
    i1 = jnp.min(jnp.where(lg == m1, lane, float(LANES)), axis=-1, keepdims=True)
    lg2 = jnp.where(lane == i1, NEG_BIG, lg)
    m2 = jnp.max(lg2, axis=-1, keepdims=True)
    i2 = jnp.min(jnp.where(lg2 == m2, lane, float(LANES)), axis=-1, keepdims=True)
    e = jnp.exp(m2 - m1)
    den = 1.0 + e

    e1 = jnp.where(lane == i1, 1.0, 0.0)
    e2 = jnp.where(lane == i2, 1.0, 0.0)
    before1 = jnp.dot(tril_ref[...], e1.astype(BF16), preferred_element_type=F32)
    before2 = jnp.dot(tril_ref[...], e2.astype(BF16), preferred_element_type=F32)
    cnt1 = jnp.sum(e1, axis=0, keepdims=True)
    cnt2 = jnp.sum(e2, axis=0, keepdims=True)
    start = (jnp.sum(jnp.where(i1 < lane, 1.0, 0.0), axis=0, keepdims=True)
             + jnp.sum(jnp.where(i2 < lane, 1.0, 0.0), axis=0, keepdims=True))
    pos0 = jnp.sum(e1 * (start + before1), axis=-1, keepdims=True)
    pos1 = jnp.sum(e2 * (start + cnt1 + before2), axis=-1, keepdims=True)
    route = jnp.where(lane == 0.0, i1, jnp.where(lane == 1.0, i2,
                      jnp.where(lane == 2.0, 1.0 / den, jnp.where(lane == 3.0, e / den,
                                jnp.where(lane == 4.0, pos0, jnp.where(lane == 5.0, pos1, 0.0))))))
    route_ref[...] = route
    cnt_ref[...] = jnp.broadcast_to(cnt1 + cnt2, (SUB, LANES))

    q = lax.broadcasted_iota(jnp.int32, (TM, 2 * TM), 1).astype(F32)
    sel = jnp.where(jnp.logical_or(q == pos0, q == pos1), 1.0, 0.0).astype(BF16)
    xloc = lax.dot_general(sel, h2_hi, (((0,), (0,)), ((), ())), preferred_element_type=F32)
    for j in range(SLAB):
        xloc_ref[pl.ds(j, 2 * TM, stride=SLAB), :] = xloc[:, j * LANES:(j + 1) * LANES]


def _odd_out(x, hc, att, lng, lnb, wc, wa, gate, g2, shift2, scale2, wr_hi, wr_lo, br_pad):
    half = D // 2
    att_specs = [_stream_specs(LANES)[0]] * Q_BLOCKS + [_stream_specs(LANES)[1]] * Q_BLOCKS
    return pl.pallas_call(
        _odd_out_kernel,
        grid=(N_TOK // TM,),
        in_specs=[_row_spec(D)] + _stream_specs(half) + att_specs +
                 [_full_spec((1, half)), _full_spec((1, half)),
                  _full_spec((half, D)), _full_spec((half, D)), _mod_spec(),
                  _full_spec((1, D)), _mod_spec(), _mod_spec(),
                  _full_spec((D, LANES)), _full_spec((D, LANES)), _full_spec((1, LANES)), _full_spec((TM, TM))],
        out_specs=[_row_spec(D), pl.BlockSpec((2 * TM * SLAB, LANES), lambda i: (i, 0)), _row_spec(LANES),
                   pl.BlockSpec((SUB, LANES), lambda i: (i, 0))],
        out_shape=[jax.ShapeDtypeStruct((N_TOK, D), F32), jax.ShapeDtypeStruct((2 * N_TOK * SLAB, LANES), F32),
                   jax.ShapeDtypeStruct((N_TOK, LANES), F32),
                   jax.ShapeDtypeStruct((N_CHUNKS * SUB, LANES), F32)],
        compiler_params=_params(1),
        name="odd_out",
    )(x, *hc, *att[0], *att[1], lng, lnb, wc, wa, gate, g2, shift2, scale2, wr_hi, wr_lo, br_pad,
      jnp.tril(jnp.ones((TM, TM), BF16), -1))


def _moe_kernel(te_ref, nv_ref, nu_ref, c0_ref, c1_ref, src_ref, dst_ref, len_ref,
                xloc_hbm, w13_ref, w2_ref, yloc_hbm, xbuf, obuf, gsem, ssem):
    del te_ref
    i = pl.program_id(0)
    n_used = nu_ref[0]
    slot = i % 2
    other = 1 - slot

    def rows(start, n):
        return pl.ds(pl.multiple_of(start * SLAB, SLAB), n * SLAB)

    def for_pieces(tile, fn):
        first = c0_ref[tile]

        def body(c, carry):
            k = tile * N_CHUNKS + first + c
            n = len_ref[k]

            @pl.when(n > 0)
            def _():
                fn(src_ref[k], dst_ref[k], n)

            return carry

        lax.fori_loop(0, c1_ref[tile] - first, body, 0)

    def start_gather(tile, s):
        for_pieces(tile, lambda src, dst, n: pltpu.make_async_copy(
            xloc_hbm.at[rows(src, n), :], xbuf.at[s, rows(dst, n), :], gsem.at[s]).start())

    def start_scatter(tile, s):
        for_pieces(tile, lambda src, dst, n: pltpu.make_async_copy(
            obuf.at[s, rows(dst, n), :], yloc_hbm.at[rows(src, n), :], ssem.at[s]).start())

    def wait_gather(s, n):
        pltpu.make_async_copy(xloc_hbm.at[rows(0, n), :], xbuf.at[s, rows(0, n), :], gsem.at[s]).wait()

    def wait_scatter(s, n):
        pltpu.make_async_copy(obuf.at[s, rows(0, n), :], yloc_hbm.at[rows(0, n), :], ssem.at[s]).wait()

    @pl.when(i == 0)
    def _():
        xbuf[...] = jnp.zeros(xbuf.shape, F32)
        start_gather(0, 0)

    @pl.when(i < n_used)
    def _():
        wait_gather(slot, nv_ref[i])

        @pl.when(i + 1 < n_used)
        def _():
            start_gather(i + 1, other)

        @pl.when(i >= 2)
        def _():
            wait_scatter(slot, nv_ref[jnp.maximum(i - 2, 0)])

        x = jnp.concatenate([xbuf[slot, pl.ds(j, MOE_TM, stride=SLAB), :] for j in range(SLAB)], axis=1)
        h = x.astype(BF16)
        ab = jnp.dot(h, w13_ref[...], preferred_element_type=F32)
        a = ab[:, :D_FF_E]
        u = (a * jax.nn.sigmoid(a) * ab[:, D_FF_E:]).astype(BF16)
        y = jnp.dot(u, w2_ref[...], preferred_element_type=F32)
        for j in range(SLAB):
            obuf[slot, pl.ds(j, MOE_TM, stride=SLAB), :] = y[:, j * LANES:(j + 1) * LANES]
        start_scatter(i, slot)

        @pl.when(i == n_used - 1)
        def _():
            @pl.when(i >= 1)
            def _():
                wait_scatter(other, nv_ref[jnp.maximum(i - 1, 0)])

            wait_scatter(slot, nv_ref[i])


def _moe_experts(plan, xloc, w13, w2):
    def weight(rows, cols):
        return pl.BlockSpec((None, rows, cols), lambda i, te, *_: (te[i], 0, 0))

    grid_spec = pltpu.PrefetchScalarGridSpec(
        num_scalar_prefetch=len(plan),
        grid=(MOE_TILES,),
        in_specs=[pl.BlockSpec(memory_space=pl.ANY), weight(D, 2 * D_FF_E), weight(D_FF_E, D)],
        out_specs=pl.BlockSpec(memory_space=pl.ANY),
        scratch_shapes=[
            pltpu.VMEM((2, MOE_TM * SLAB, LANES), F32),
            pltpu.VMEM((2, MOE_TM * SLAB, LANES), F32),
            pltpu.SemaphoreType.DMA((2,)),
            pltpu.SemaphoreType.DMA((2,)),
        ],
    )
    return pl.pallas_call(
        _moe_kernel,
        grid_spec=grid_spec,
        out_shape=jax.ShapeDtypeStruct((2 * N_TOK * SLAB, LANES), F32),
        compiler_params=_params(1),
        name="moe_experts",
    )(*plan, xloc, w13, w2)


def _route_plan(counts):
    cnt = counts.reshape(N_CHUNKS, SUB, LANES)[:, 0, :N_EXPERTS].astype(jnp.int32)
    run_local = jnp.cumsum(cnt, axis=1) - cnt
    run_group = jnp.cumsum(cnt, axis=0) - cnt
    total = jnp.sum(cnt, axis=0)
    tiles = (total + MOE_TM - 1) // MOE_TM
    tile_end = jnp.cumsum(tiles)
    tile_ids = jnp.arange(MOE_TILES, dtype=jnp.int32)
    tile_expert = jnp.minimum(jnp.sum((tile_ids[:, None] >= tile_end[None, :]).astype(jnp.int32), axis=1),
                              N_EXPERTS - 1)
    first = (tile_ids - (tile_end - tiles)[tile_expert]) * MOE_TM
    used = tile_ids < tile_end[-1]
    tile_valid = jnp.where(used, jnp.clip(total[tile_expert] - first, 0, MOE_TM), 0).astype(jnp.int32)
    a = run_group[:, tile_expert].T
    n = cnt[:, tile_expert].T
    lo = jnp.maximum(a, first[:, None])
    hi = jnp.minimum(a + n, first[:, None] + MOE_TM)
    length = jnp.where(used[:, None], jnp.maximum(hi - lo, 0), 0)
    chunk_row0 = jnp.arange(N_CHUNKS, dtype=jnp.int32)[None, :] * (2 * TM)
    src = jnp.where(length > 0, chunk_row0 + run_local[:, tile_expert].T + (lo - a), 0)
    dst = jnp.where(length > 0, lo - first[:, None], 0)
    n_used = tile_end[-1:].astype(jnp.int32)
    seen = jnp.cumsum((length > 0).astype(jnp.int32), axis=1)
    c_first = jnp.sum((seen == 0).astype(jnp.int32), axis=1)
    c_stop = N_CHUNKS - jnp.sum(jnp.logical_and(seen == seen[:, -1:], length == 0).astype(jnp.int32), axis=1)
    flat = lambda t: t.astype(jnp.int32).reshape(-1)
    return tile_expert, tile_valid, n_used, c_first, c_stop, flat(src), flat(dst), flat(length)


def _moe_combined(x_ref, y_ref, route_ref, gate_ref):
    y = jnp.concatenate([y_ref[pl.ds(j, 2 * TM, stride=SLAB), :] for j in range(SLAB)], axis=1).astype(BF16)
    q = lax.broadcasted_iota(jnp.int32, (TM, 2 * TM), 1).astype(F32)
    pick = jnp.where(q == route_ref[:, 4:5], route_ref[:, 2:3],
                     jnp.where(q == route_ref[:, 5:6], route_ref[:, 3:4], 0.0)).astype(BF16)
    mix = jnp.dot(pick, y, preferred_element_type=F32)
    return _gated_add(x_ref[...], gate_ref[...], mix)


def _moe_pending_specs(tile0=0, stream=None):
    gate = _mod_spec() if stream is None else pl.BlockSpec((None, SUB, D), lambda i: (stream, 0, 0))
    return [pl.BlockSpec((TM, D), lambda i: (tile0 + i, 0)),
            pl.BlockSpec((TM * 2 * SLAB, LANES), lambda i: (tile0 + i, 0)),
            pl.BlockSpec((TM, LANES), lambda i: (tile0 + i, 0)), gate]


def _even_in_moe_kernel(x1_ref, y_ref, route_ref, gate_ref, g_ref, sh_ref, sc_ref, w_ref, z_ref, x_ref):
    x = _moe_combined(x1_ref, y_ref, route_ref, gate_ref)
    x_ref[...] = x
    h = _modulate(_rms(x, g_ref[...]), sh_ref[...], sc_ref[...])
    z_ref[...] = jnp.dot(h.astype(BF16), w_ref[...], preferred_element_type=F32)


def _even_in_moe(pending, g, shift, scale, w_bf16):
    return pl.pallas_call(
        _even_in_moe_kernel,
        grid=(N_TOK // TM,),
        in_specs=_moe_pending_specs() + [_full_spec((1, D)), _mod_spec(), _mod_spec(), _full_spec((D, D_IN_EVEN))],
        out_specs=[_row_spec(D_IN_EVEN), _row_spec(D)],
        out_shape=[jax.ShapeDtypeStruct((N_TOK, D_IN_EVEN), F32), jax.ShapeDtypeStruct((N_TOK, D), F32)],
        compiler_params=_params(1),
        name="even_in_moe",
    )(*pending, g, shift, scale, w_bf16)


def _final_kernel(x1_ref, y_ref, route_ref, gate_ref, g_ref, o_ref, cols):
    y = _rms(_moe_combined(x1_ref, y_ref, route_ref, gate_ref), g_ref[...])
    for c in range(SLAB):
        cols[c] = y[:, c * LANES:(c + 1) * LANES]
    for b in range(SUB):
        for c in range(SLAB):
            o_ref[b, :, c * LANES:(c + 1) * LANES] = cols[c, pl.ds(b, T_TILE, stride=SUB), :]


def _final_norm(pending, g, *, batch, steps, stream):
    t_tiles = steps // T_TILE
    return pl.pallas_call(
        _final_kernel,
        grid=(batch * steps // TM,),
        in_specs=_moe_pending_specs(stream * TILES_PER_STREAM, stream) + [_full_spec((1, D))],
        out_specs=pl.BlockSpec((SUB, T_TILE, D), lambda i: (i // t_tiles, i % t_tiles, 0)),
        out_shape=jax.ShapeDtypeStruct((batch, steps, D), F32),
        scratch_shapes=[pltpu.VMEM((SLAB, TM, LANES), F32)],
        compiler_params=_params(1),
        name=f"final_norm_{steps}",
    )(*pending, g)


def _block_diag_gates(w_r, w_i, b_r, b_i):
    def bd(w):
        w4 = w.reshape(4, 2, A_BS, A_BS)
        z = jnp.zeros((4, A_BS, A_BS), w.dtype)
        top = jnp.concatenate([w4[:, 0], z], axis=2)
        bot = jnp.concatenate([z, w4[:, 1]], axis=2)
        return jnp.concatenate([top, bot], axis=1)

    wg = jnp.concatenate([bd(w_r[0]), bd(w_i[0]), bd(w_r[1]), bd(w_i[1])], axis=2)
    bg = jnp.concatenate([b_r[0].reshape(4, 1, LANES), b_i[0].reshape(4, 1, LANES),
                          b_r[1].reshape(4, 1, LANES), b_i[1].reshape(4, 1, LANES)], axis=2)
    return wg.astype(BF16), bg


def _head_mean_matrix(width):
    idx = jnp.arange(width) // HEAD_DIM
    return ((idx[:, None] == idx[None, :]).astype(F32) / HEAD_DIM).astype(BF16)


def _rope_tables():
    pos = jnp.arange(DEC_SEQ)
    row = (pos // GRID_W).astype(F32)
    col = (pos % GRID_W).astype(F32)
    n_freq = HEAD_DIM // 4
    inv = ROPE_THETA ** (-jnp.arange(n_freq, dtype=F32) / n_freq)
    ang = jnp.stack([row[:, None] * inv, col[:, None] * inv], axis=1)
    cos = jnp.cos(ang)
    sin = jnp.sin(ang)
    cos_h = jnp.stack([cos, cos], axis=2).reshape(DEC_SEQ, HEAD_DIM)
    sin_h = jnp.stack([-sin, sin], axis=2).reshape(DEC_SEQ, HEAD_DIM)
    cos_t = jnp.tile(cos_h, (1, 2))
    sin_t = jnp.tile(sin_h, (1, 2))
    cos_t = jnp.broadcast_to(cos_t[:, None, :], (DEC_SEQ, SUB, D_KV)).reshape(N_SAMPLE, D_KV)
    sin_t = jnp.broadcast_to(sin_t[:, None, :], (DEC_SEQ, SUB, D_KV)).reshape(N_SAMPLE, D_KV)
    return cos_t, sin_t


def _prompt_to_batch_major(a):
    w = a.shape[-1]
    return a[:N_PROMPT].reshape(P_GROUPS, SEQ, SUB, w).transpose(0, 2, 1, 3).reshape(BATCH, SEQ, w)


def kernel(x_prompt, x_sample, c, state_rglru, cache_k, cache_v, c_ctx, w_mod, b_mod, norm1, norm2, ev_w_in, a_conv_w, a_conv_b, a_w_r, a_b_r, a_w_i, a_b_i, a_lam, b_w_pool, b_scale, ev_w_out, od_w_in, c_conv_w, c_conv_b, c_ln_g, c_ln_b, q_norm, k_norm, od_w_out, ff_w1, ff_w3, ff_w2, moe_w_router, moe_b_router, moe_w1, moe_w3, moe_w2, norm_f):
    cond16 = jnp.concatenate([c_ctx[None, :], c, jnp.zeros((16 - 1 - DEC_BATCH, D), F32)], axis=0)
    mods = _ada_params(cond16, w_mod, b_mod)
    mods = jnp.stack([jnp.broadcast_to(mods[:, :, 0:1], (DEPTH, 6, SUB, D)), mods[:, :, 1:1 + SUB]], axis=2)

    cos_t, sin_t = _rope_tables()
    bdq = _head_mean_matrix(D_ATT)
    bdk = _head_mean_matrix(D_KV)
    cache_k4 = cache_k.reshape(DEC_BATCH, DEPTH // 2, PAST_LEN, D_KV)
    cache_v4 = cache_v.reshape(DEC_BATCH, DEPTH // 2, PAST_LEN, D_KV)
    ff_w1b, ff_w3b, ff_w2b = ff_w1.astype(BF16), ff_w3.astype(BF16), ff_w2.astype(BF16)

    x = pending = moe_wb = None
    new_states, new_k, new_v = [], [], []
    for layer in range(DEPTH):
        li = layer // 2
        shift1, scale1, gate1, shift2, scale2, gate2 = [mods[layer, j] for j in range(6)]
        g1 = norm1[layer].reshape(1, D)
        g2 = norm2[layer].reshape(1, D)
        if layer % 2 == 0:
            if layer == 0:
                z, x = _first_in(x_prompt, x_sample, g1, shift1, scale1, ev_w_in[li].astype(BF16))
            else:
                z, x = _even_in_moe(pending, g1, shift1, scale1, ev_w_in[li].astype(BF16))
            wg, bg = _block_diag_gates(a_w_r[li], a_w_i[li], a_b_r[li], a_b_i[li])
            lam = jnp.concatenate([a_lam[li, 0].reshape(4, 1, LANES), a_lam[li, 1].reshape(4, 1, LANES)], axis=2)
            cb = a_conv_b[li].reshape(1, D_A)
            h0_p = jnp.zeros((P_GROUPS, 2, SUB, D_A), F32)
            h0_s = state_rglru[:, li].transpose(1, 0, 2)[None]
            ya_p, h_last = _rglru_call(z, a_conv_w[li], cb, wg, bg, lam, h0_p,
                                       rows=R_PROMPT, groups=P_GROUPS, row_block0=0)
            ya_s, _ = _rglru_call(z, a_conv_w[li], cb, wg, bg, lam, h0_s,
                                  rows=R_SAMPLE, groups=1, row_block0=1)
            wp = b_w_pool[li].astype(BF16)
            sp = b_scale[li].reshape(1, D_B)
            yb_p = _pool_call(z, wp, sp, rows=R_PROMPT, groups=P_GROUPS, row_block0=0)
            yb_s = _pool_call(z, wp, sp, rows=R_SAMPLE, groups=1, row_block0=1)
            w_out = ev_w_out[li].astype(BF16)
            x, moe_wb = _even_tail(x, (ya_p, ya_s), (yb_p, yb_s), w_out[:D_A], w_out[D_A:], gate1,
                                   g2, shift2, scale2, gate2, ff_w1b, ff_w3b, ff_w2b, li, (moe_w1, moe_w3, moe_w2))
            new_states.append(h_last.transpose(0, 2, 1, 3).reshape(BATCH, 2, D_A))
        else:
            qg = jnp.tile(q_norm[li], N_Q_HEADS).reshape(1, D_ATT)
            kg = jnp.tile(k_norm[li], N_KV_HEADS).reshape(1, D_KV)
            glu, *qs, k, v = _odd_in(x, g1, shift1, scale1, od_w_in[li].astype(BF16), bdq, bdk, qg, kg,
                                     cos_t, sin_t)
            att = _attention(qs, k, v, cache_k4[:, li], cache_v4[:, li])
            cw = c_conv_w[li]
            cb = c_conv_b[li].reshape(1, D_C)
            hc = (_conv_call(glu, cw, cb, rows=R_PROMPT, groups=P_GROUPS, row_block0=0),
                  _conv_call(glu, cw, cb, rows=R_SAMPLE, groups=1, row_block0=1))
            w_out = od_w_out[li].astype(BF16)
            wr = jnp.zeros((D, LANES), F32).at[:, :N_EXPERTS].set(moe_w_router[li])
            br = jnp.zeros((1, LANES), F32).at[0, :N_EXPERTS].set(moe_b_router[li])
            wr_hi = wr.astype(BF16)
            wr_lo = (wr - wr_hi.astype(F32)).astype(BF16)
            x1, xloc, route, counts = _odd_out(x, hc, att, c_ln_g[li].reshape(1, D_C), c_ln_b[li].reshape(1, D_C),
                                     w_out[:D_C], w_out[D_C:], gate1, g2, shift2, scale2, wr_hi, wr_lo, br)
            y2 = _moe_experts(_route_plan(counts), xloc, *moe_wb)
            pending = (x1, y2, route, gate2)
            new_k.append(_prompt_to_batch_major(k).reshape(BATCH, SEQ, N_KV_HEADS, HEAD_DIM))
            new_v.append(_prompt_to_batch_major(v).reshape(BATCH, SEQ, N_KV_HEADS, HEAD_DIM))

    gf = norm_f.reshape(1, D)
    y_prompt = _final_norm(pending, gf, batch=BATCH, steps=SEQ, stream=0)
    y_sample = _final_norm(pending, gf, batch=DEC_BATCH, steps=DEC_SEQ, stream=1)
    return (y_prompt, y_sample, jnp.stack(new_states, axis=1), jnp.stack(new_k, axis=1), jnp.stack(new_v, axis=1))
```
